```python
import math
import jax, jax.numpy as jnp
from jax import lax
import numpy as np

D_MODEL = 2048
BATCH = 4
SEQ = 2048
DEPTH = 1

GRID_W = 64
CTX_LEN = 256
EPS = 1e-6
N_MOD = 6

D_MIX = D_MODEL
D_FOURIER = D_MIX // 2
N_FOURIER_GROUPS = 4
FOURIER_GROUP = D_FOURIER // N_FOURIER_GROUPS
D_SSM = D_MIX - D_FOURIER
SSM_HEAD_DIM = 64
SSM_HEADS = D_SSM // SSM_HEAD_DIM
SSM_GROUPS = 2
SSM_STATE = 128
SSM_CONV = 3
SSM_CHUNK = 128
D_BC = SSM_GROUPS * SSM_STATE
D_XBC = D_SSM + 2 * D_BC
OFF_Z = D_FOURIER
OFF_XBC = OFF_Z + D_SSM
OFF_DT = OFF_XBC + D_XBC
D_IN_PROJ = OFF_DT + 2 * SSM_HEADS

PEER_HEADS = 8
PEER_TOPK = 16
PEER_N_KEYS = 128
PEER_EXPERTS = PEER_N_KEYS * PEER_N_KEYS
PEER_KEY_DIM = 128
PEER_HALF = PEER_KEY_DIM // 2
PEER_TOKEN_BLOCK = 128

kernel_name = 'fourier_ssd_peer_hybrid_dit'


def rms_norm(u, g):
    uf = u.astype(jnp.float32)
    y = uf * lax.rsqrt(jnp.mean(uf * uf, axis=-1, keepdims=True) + EPS)
    return (y * g.astype(jnp.float32)).astype(u.dtype)


def group_rms_norm(u, g, n_groups):
    shp = u.shape
    ug = u.reshape(*shp[:-1], n_groups, shp[-1] // n_groups).astype(jnp.float32)
    y = ug * lax.rsqrt(jnp.mean(ug * ug, axis=-1, keepdims=True) + EPS)
    return (y.reshape(shp) * g.astype(jnp.float32)).astype(u.dtype)


def adaln_params(cond, w_mod, b_mod):
    m = jax.nn.silu(cond) @ w_mod + b_mod
    return jnp.split(m[:, None, :], N_MOD, axis=-1)


def modulate(h, shift, scale):
    return h * (1 + scale) + shift


def centred_dwconv(u, w, b):
    k = w.shape[0]
    pad = k // 2
    length = u.shape[1]
    up = jnp.pad(u, ((0, 0), (pad, k - 1 - pad), (0, 0)))
    return sum(up[:, i:i + length] * w[i] for i in range(k)) + b


def segsum(a):
    t = a.shape[-1]
    cs = jnp.cumsum(a, axis=-1)
    diff = cs[..., :, None] - cs[..., None, :]
    mask = jnp.tril(jnp.ones((t, t), dtype=bool))
    return jnp.where(mask, diff, -jnp.inf)


def ssd_chunked(xs, dt, a, bm, cm, init):
    bsz, length, h, p = xs.shape
    nc = length // SSM_CHUNK
    ch = lambda t: t.reshape(bsz, nc, SSM_CHUNK, *t.shape[2:])
    xdt = ch(xs * dt[..., None])
    bc, cc = ch(bm), ch(cm)
    a_dt = jnp.moveaxis(ch(dt * a), -1, 1)
    a_cum = jnp.cumsum(a_dt, axis=-1)
    l_mat = jnp.exp(segsum(a_dt))
    y_diag = jnp.einsum('bclhn,bcshn,bhcls,bcshp->bclhp', cc, bc, l_mat, xdt)
    decay_states = jnp.exp(a_cum[..., -1:] - a_cum)
    states = jnp.einsum('bclhn,bhcl,bclhp->bchpn', bc, decay_states, xdt)
    states = jnp.concatenate([init[:, None], states], axis=1)
    chunk_decay = jnp.exp(segsum(jnp.pad(a_cum[..., -1], ((0, 0), (0, 0), (1, 0)))))
    new_states = jnp.einsum('bhzc,bchpn->bzhpn', chunk_decay, states)
    states, final = new_states[:, :-1], new_states[:, -1]
    y_off = jnp.einsum('bclhn,bchpn,bhcl->bclhp', cc, states, jnp.exp(a_cum))
    return (y_diag + y_off).reshape(bsz, length, h, p), final


def fourier_branch(proj, w_fmix):
    bsz, length, _ = proj.shape
    f = proj[..., :OFF_Z].reshape(bsz, length, N_FOURIER_GROUPS, FOURIER_GROUP)
    fr = jnp.fft.fftn(f.astype(jnp.float32), axes=(1, 3), norm='ortho').real.astype(proj.dtype)
    return jnp.einsum('blgc,gcd->blgd', fr, w_fmix).reshape(bsz, length, D_FOURIER)


def ssd_branch(proj, conv_w, conv_b, dt_bias_f, dt_bias_b, a_log_f, a_log_b,
               d_skip_f, d_skip_b, ssm_norm_w, init_f, init_b):
    bsz, length, _ = proj.shape
    f32 = jnp.float32
    z = proj[..., OFF_Z:OFF_XBC]
    xbc = jax.nn.silu(centred_dwconv(proj[..., OFF_XBC:OFF_DT], conv_w, conv_b)).astype(f32)
    dt_raw = proj[..., OFF_DT:].astype(f32)
    rep = SSM_HEADS // SSM_GROUPS
    xs = xbc[..., :D_SSM].reshape(bsz, length, SSM_HEADS, SSM_HEAD_DIM)
    bm = jnp.repeat(xbc[..., D_SSM:D_SSM + D_BC].reshape(bsz, length, SSM_GROUPS, SSM_STATE), rep, axis=2)
    cm = jnp.repeat(xbc[..., D_SSM + D_BC:].reshape(bsz, length, SSM_GROUPS, SSM_STATE), rep, axis=2)
    dt_f = jax.nn.softplus(dt_raw[..., :SSM_HEADS] + dt_bias_f.astype(f32))
    dt_b = jax.nn.softplus(dt_raw[..., SSM_HEADS:] + dt_bias_b.astype(f32))
    a_f = -jnp.exp(a_log_f.astype(f32))
    a_b = -jnp.exp(a_log_b.astype(f32))
    flip = lambda t: jnp.flip(t, axis=1)
    y_f, fin_f = ssd_chunked(xs, dt_f, a_f, bm, cm, init_f)
    y_b, fin_b = ssd_chunked(flip(xs), flip(dt_b), a_b, flip(bm), flip(cm), init_b)
    y = y_f + flip(y_b) + (d_skip_f + d_skip_b).astype(f32)[:, None] * xs
    y = y.reshape(bsz, length, D_SSM).astype(proj.dtype)
    y = group_rms_norm(y * jax.nn.silu(z), ssm_norm_w, SSM_GROUPS)
    return y, fin_f, fin_b


def peer_ffn(h, w_query, sub_keys_1, sub_keys_2, expert_u, expert_v):
    bsz, length, d = h.shape
    tokens = h.reshape(-1, PEER_TOKEN_BLOCK, d)

    def block(tb):
        tbn = tb.shape[0]
        q = (tb @ w_query).reshape(tbn, PEER_HEADS, PEER_KEY_DIM)
        s1 = jnp.einsum('thd,hkd->thk', q[..., :PEER_HALF], sub_keys_1)
        s2 = jnp.einsum('thd,hkd->thk', q[..., PEER_HALF:], sub_keys_2)
        v1, i1 = lax.top_k(s1, PEER_TOPK)
        v2, i2 = lax.top_k(s2, PEER_TOPK)
        cand = (v1[..., :, None] + v2[..., None, :]).reshape(tbn, PEER_HEADS, PEER_TOPK * PEER_TOPK)
        cand_idx = (i1[..., :, None] * PEER_N_KEYS + i2[..., None, :]).reshape(tbn, PEER_HEADS, PEER_TOPK * PEER_TOPK)
        top_s, pos = lax.top_k(cand, PEER_TOPK)
        idx = jnp.take_along_axis(cand_idx, pos, axis=-1)
        gate = jax.nn.softmax(top_s.astype(jnp.float32), axis=-1).astype(tb.dtype)
        u = expert_u[idx]
        v = expert_v[idx]
        act = jax.nn.gelu(jnp.einsum('td,thkd->thk', tb, u), approximate=False)
        return jnp.einsum('thk,thkd->td', gate * act, v)

    out = lax.map(block, tokens)
    return out.reshape(bsz, length, d)


def setup_inputs(seed: int = 0) -> dict:
    key = jax.random.key(seed)
    ks = jax.random.split(key, 32)
    nrm = lambda k, shp, s: jax.random.normal(k, shp, jnp.float32) * s
    dt0 = jnp.exp(jax.random.uniform(ks[14], (DEPTH, SSM_HEADS), minval=math.log(1e-3), maxval=math.log(1e-1)))
    dt1 = jnp.exp(jax.random.uniform(ks[15], (DEPTH, SSM_HEADS), minval=math.log(1e-3), maxval=math.log(1e-1)))
    return {
        'x': nrm(ks[0], (BATCH, SEQ, D_MODEL), 1.0),
        'c': nrm(ks[1], (BATCH, D_MODEL), 1.0),
        'ctx': nrm(ks[2], (BATCH, CTX_LEN, D_MODEL), 1.0),
        'c_ctx': nrm(ks[3], (D_MODEL,), 1.0),
        'w_mod': nrm(ks[4], (DEPTH, D_MODEL, N_MOD * D_MODEL), 0.5 * D_MODEL ** -0.5),
        'b_mod': nrm(ks[5], (DEPTH, N_MOD * D_MODEL), 0.01),
        'g_pre_mix': 1.0 + nrm(ks[6], (DEPTH, D_MODEL), 0.01),
        'g_post_mix': 1.0 + nrm(ks[7], (DEPTH, D_MODEL), 0.01),
        'g_pre_ffn': 1.0 + nrm(ks[8], (DEPTH, D_MODEL), 0.01),
        'g_post_ffn': 1.0 + nrm(ks[9], (DEPTH, D_MODEL), 0.01),
        'w_in': nrm(ks[10], (DEPTH, D_MODEL, D_IN_PROJ), D_MODEL ** -0.5),
        'w_fmix': nrm(ks[11], (DEPTH, N_FOURIER_GROUPS, FOURIER_GROUP, FOURIER_GROUP), FOURIER_GROUP ** -0.5),
        'conv_w': nrm(ks[12], (DEPTH, SSM_CONV, D_XBC), SSM_CONV ** -0.5),
        'conv_b': nrm(ks[13], (DEPTH, D_XBC), 0.01),
        'dt_bias_f': dt0 + jnp.log(-jnp.expm1(-dt0)),
        'dt_bias_b': dt1 + jnp.log(-jnp.expm1(-dt1)),
        'a_log_f': jnp.log(jax.random.uniform(ks[16], (DEPTH, SSM_HEADS), minval=1.0, maxval=16.0)),
        'a_log_b': jnp.log(jax.random.uniform(ks[17], (DEPTH, SSM_HEADS), minval=1.0, maxval=16.0)),
        'd_skip_f': 1.0 + nrm(ks[18], (DEPTH, SSM_HEADS), 0.1),
        'd_skip_b': 1.0 + nrm(ks[19], (DEPTH, SSM_HEADS), 0.1),
        'ssm_norm_w': 1.0 + nrm(ks[20], (DEPTH, D_SSM), 0.01),
        'w_out': nrm(ks[21], (DEPTH, D_MIX, D_MODEL), D_MIX ** -0.5),
        'w_query': nrm(ks[22], (DEPTH, D_MODEL, PEER_HEADS * PEER_KEY_DIM), D_MODEL ** -0.5),
        'sub_keys_1': nrm(ks[23], (DEPTH, PEER_HEADS, PEER_N_KEYS, PEER_HALF), PEER_HALF ** -0.5),
        'sub_keys_2': nrm(ks[24], (DEPTH, PEER_HEADS, PEER_N_KEYS, PEER_HALF), PEER_HALF ** -0.5),
        'expert_u': nrm(ks[25], (DEPTH, PEER_EXPERTS, D_MODEL), D_MODEL ** -0.5),
        'expert_v': nrm(ks[26], (DEPTH, PEER_EXPERTS, D_MODEL), D_MODEL ** -0.5),
    }


def reference(x, c, ctx, c_ctx, w_mod, b_mod, g_pre_mix, g_post_mix, g_pre_ffn, g_post_ffn,
              w_in, w_fmix, conv_w, conv_b, dt_bias_f, dt_bias_b, a_log_f, a_log_b,
              d_skip_f, d_skip_b, ssm_norm_w, w_out, w_query, sub_keys_1, sub_keys_2,
              expert_u, expert_v):
    bsz = x.shape[0]
    for layer in range(DEPTH):
        sm_x, cm_x, gm_x, sf_x, cf_x, gf_x = adaln_params(c, w_mod[layer], b_mod[layer])
        sm_c, cm_c, gm_c, sf_c, cf_c, gf_c = adaln_params(c_ctx[None], w_mod[layer], b_mod[layer])
        ssd_args = (conv_w[layer], conv_b[layer], dt_bias_f[layer], dt_bias_b[layer], a_log_f[layer],
                    a_log_b[layer], d_skip_f[layer], d_skip_b[layer], ssm_norm_w[layer])

        hc = modulate(rms_norm(ctx, g_pre_mix[layer]), sm_c, cm_c)
        proj_c = hc @ w_in[layer]
        zero_state = jnp.zeros((bsz, SSM_HEADS, SSM_HEAD_DIM, SSM_STATE), jnp.float32)
        y_c_ssm, fin_f, fin_b = ssd_branch(proj_c, *ssd_args, zero_state, zero_state)

        hx = modulate(rms_norm(x, g_pre_mix[layer]), sm_x, cm_x)
        proj_x = hx @ w_in[layer]
        y_x_ssm, _, _ = ssd_branch(proj_x, *ssd_args, fin_f, fin_b)
        y_x = jnp.concatenate([fourier_branch(proj_x, w_fmix[layer]), y_x_ssm], axis=-1) @ w_out[layer]
        x = x + gm_x * rms_norm(y_x, g_post_mix[layer])

        if layer + 1 < DEPTH:
            y_c = jnp.concatenate([fourier_branch(proj_c, w_fmix[layer]), y_c_ssm], axis=-1) @ w_out[layer]
            ctx = ctx + gm_c * rms_norm(y_c, g_post_mix[layer])
            hc2 = modulate(rms_norm(ctx, g_pre_ffn[layer]), sf_c, cf_c)
            ctx = ctx + gf_c * rms_norm(peer_ffn(hc2, w_query[layer], sub_keys_1[layer], sub_keys_2[layer],
                                                 expert_u[layer], expert_v[layer]), g_post_ffn[layer])

        hx2 = modulate(rms_norm(x, g_pre_ffn[layer]), sf_x, cf_x)
        y_f = peer_ffn(hx2, w_query[layer], sub_keys_1[layer], sub_keys_2[layer], expert_u[layer], expert_v[layer])
        x = x + gf_x * rms_norm(y_f, g_post_ffn[layer])
    return x
```

```python
import functools
import math

import jax
import jax.numpy as jnp
import numpy as np
from jax import lax
from jax.experimental import pallas as pl
from jax.experimental.pallas import tpu as pltpu

F32 = jnp.float32
BF16 = jnp.bfloat16

EPS = 1e-6
N_MOD = 6
N_FOURIER_GROUPS = 4
SSM_HEAD_DIM = 64
SSM_GROUPS = 2
SSM_STATE = 128
SSM_CHUNK = 128
PEER_HEADS = 8
PEER_TOPK = 16
PEER_N_KEYS = 128
PEER_KEY_DIM = 128
PEER_HALF = PEER_KEY_DIM // 2

LANE = 128
VMEM_LIMIT = 56 * 1024 * 1024


def _cparams(sem):
    return pltpu.CompilerParams(dimension_semantics=sem, vmem_limit_bytes=VMEM_LIMIT)


def _bdot(a, b):
    return jnp.dot(a.astype(BF16), b.astype(BF16), preferred_element_type=F32)


def _dot_nt(a, b, precision=None):
    return lax.dot_general(a, b, (((1,), (1,)), ((), ())), precision=precision,
                           preferred_element_type=F32)


def _rms(u, g):
    return u * lax.rsqrt(jnp.mean(u * u, axis=-1, keepdims=True) + EPS) * g


def _silu(u):
    return u * (1.0 / (1.0 + jnp.exp(-u)))


def _adaln_kernel(c_ref, w_ref, b_ref, o_ref):
    a = _silu(c_ref[...])
    o_ref[...] = _bdot(a, w_ref[...]) + b_ref[...]


def _adaln(cond, w_mod, b_mod, tn=1024):
    rows, d = cond.shape
    n = w_mod.shape[1]
    return pl.pallas_call(
        _adaln_kernel,
        grid=(n // tn,),
        in_specs=[pl.BlockSpec((rows, d), lambda j: (0, 0)),
                  pl.BlockSpec((d, tn), lambda j: (0, j)),
                  pl.BlockSpec((1, tn), lambda j: (0, j))],
        out_specs=pl.BlockSpec((rows, tn), lambda j: (0, j)),
        out_shape=jax.ShapeDtypeStruct((rows, n), F32),
        compiler_params=_cparams(("arbitrary",)),
        name="adaln",
    )(cond, w_mod, b_mod.reshape(1, n))


def _fold_kernel(w_ref, fm_ref, cc_ref, sc_ref, oc_ref, os_ref):
    hp = lax.Precision.HIGHEST
    w = w_ref[...]
    fm = fm_ref[0]
    mc = jnp.dot(cc_ref[...], fm, precision=hp, preferred_element_type=F32)
    ms = jnp.dot(sc_ref[...], fm, precision=hp, preferred_element_type=F32)
    oc_ref[...] = jnp.dot(w, mc, precision=hp, preferred_element_type=F32).astype(BF16)
    os_ref[...] = jnp.dot(w, ms, precision=hp, preferred_element_type=F32).astype(BF16)


def _fold_fourier(w_in, w_fmix, cos_c, sin_c):
    d = w_in.shape[0]
    ng, gc, _ = w_fmix.shape
    out = jax.ShapeDtypeStruct((d, ng * gc), BF16)
    return pl.pallas_call(
        _fold_kernel,
        grid=(ng,),
        in_specs=[pl.BlockSpec((d, gc), lambda g: (0, g)),
                  pl.BlockSpec((1, gc, gc), lambda g: (g, 0, 0)),
                  pl.BlockSpec((gc, gc), lambda g: (0, 0)),
                  pl.BlockSpec((gc, gc), lambda g: (0, 0))],
        out_specs=[pl.BlockSpec((d, gc), lambda g: (0, g)),
                   pl.BlockSpec((d, gc), lambda g: (0, g))],
        out_shape=[out, out],
        compiler_params=_cparams(("arbitrary",)),
        name="fourier_fold",
    )(w_in, w_fmix, cos_c, sin_c)


def _inproj_kernel(x_ref, g_ref, sh_ref, sc_ref, w_ref, o_ref, h_ref):
    @pl.when(pl.program_id(2) == 0)
    def _():
        slab = min(LANE, h_ref.shape[0])

        def body(r, carry):
            rows = pl.ds(pl.multiple_of(r * slab, slab), slab)
            h = _rms(x_ref[0, rows, :], g_ref[...]) * (1.0 + sc_ref[0]) + sh_ref[0]
            h_ref[rows, :] = h.astype(BF16)
            return carry

        lax.fori_loop(0, h_ref.shape[0] // slab, body, 0)

    o_ref[0] = jnp.dot(h_ref[...], w_ref[...], preferred_element_type=F32)


def _inproj(xin, g, shift, scale, w_all, col0, ncols, tm, tn):
    b, l, d = xin.shape
    jb = col0 // tn
    return pl.pallas_call(
        _inproj_kernel,
        grid=(b, l // tm, ncols // tn),
        in_specs=[pl.BlockSpec((1, tm, d), lambda bi, i, j: (bi, i, 0)),
                  pl.BlockSpec((1, d), lambda bi, i, j: (0, 0)),
                  pl.BlockSpec((1, 1, d), lambda bi, i, j: (bi, 0, 0)),
                  pl.BlockSpec((1, 1, d), lambda bi, i, j: (bi, 0, 0)),
                  pl.BlockSpec((d, tn), lambda bi, i, j: (0, j + jb))],
        out_specs=pl.BlockSpec((1, tm, tn), lambda bi, i, j: (bi, i, j)),
        out_shape=jax.ShapeDtypeStruct((b, l, ncols), F32),
        scratch_shapes=[pltpu.VMEM((tm, d), BF16)],
        compiler_params=_cparams(("arbitrary", "arbitrary", "arbitrary")),
        name="inproj",
    )(xin, g.reshape(1, d), shift, scale, w_all)


def _seqdft_kernel(a_ref, x_ref, o_ref, acc_ref):
    k = pl.program_id(1)

    @pl.when(k == 0)
    def _():
        acc_ref[...] = jnp.zeros_like(acc_ref)

    acc_ref[...] += jnp.dot(a_ref[...], x_ref[0].astype(BF16), preferred_element_type=F32)

    @pl.when(k == pl.num_programs(1) - 1)
    def _():
        o_ref[0] = acc_ref[...].astype(BF16)


def _seqdft(dft, proj, n_f, tk=512):
    b, l, _ = proj.shape
    kl = l // tk
    return pl.pallas_call(
        _seqdft_kernel,
        grid=(b, 2 * kl),
        in_specs=[pl.BlockSpec((l, tk), lambda bi, k: (0, k)),
                  pl.BlockSpec((1, tk, n_f), lambda bi, k: (bi, k % kl, k // kl))],
        out_specs=pl.BlockSpec((1, l, n_f), lambda bi, k: (bi, 0, 0)),
        out_shape=jax.ShapeDtypeStruct((b, l, n_f), BF16),
        scratch_shapes=[pltpu.VMEM((l, n_f), F32)],
        compiler_params=_cparams(("arbitrary", "arbitrary")),
        name="seq_dft",
    )(dft, proj)


def _conv_kernel(u_ref, w_ref, b_ref, o_ref):
    u = u_ref[0]
    l = u.shape[0]
    row = lax.broadcasted_iota(jnp.int32, u.shape, 0)
    prev = jnp.where(row == 0, 0.0, pltpu.roll(u, 1, 0))
    nxt = jnp.where(row == l - 1, 0.0, pltpu.roll(u, l - 1, 0))
    w = w_ref[...]
    o_ref[0] = _silu(prev * w[0:1] + u * w[1:2] + nxt * w[2:3] + b_ref[...])


def _conv_silu(proj, col0, conv_w, conv_b, tc=256):
    b, l, _ = proj.shape
    n = conv_w.shape[1]
    jb = col0 // tc
    return pl.pallas_call(
        _conv_kernel,
        grid=(b, n // tc),
        in_specs=[pl.BlockSpec((1, l, tc), lambda bi, j: (bi, 0, j + jb)),
                  pl.BlockSpec((3, tc), lambda bi, j: (0, j)),
                  pl.BlockSpec((1, tc), lambda bi, j: (0, j))],
        out_specs=pl.BlockSpec((1, l, tc), lambda bi, j: (bi, 0, j)),
        out_shape=jax.ShapeDtypeStruct((b, l, n), F32),
        compiler_params=_cparams(("arbitrary", "arbitrary")),
        name="conv_silu",
    )(proj, conv_w, conv_b.reshape(1, n))


def _ssd_direction(reverse, dt_raw, bias, a_neg, xs_ref, bm_ref, cm_ref, st_ref, y_ref, n_heads):
    hp = lax.Precision.HIGHEST
    t = SSM_CHUNK
    col0 = n_heads if reverse else 0
    r = lax.broadcasted_iota(jnp.int32, (t, t), 0)
    c = lax.broadcasted_iota(jnp.int32, (t, t), 1)
    low = r >= c
    mask = (r <= c) if reverse else low
    tri = mask.astype(F32)
    tri_t = ((r >= c) if reverse else (r <= c)).astype(F32)

    z = dt_raw + bias
    dtv = jnp.maximum(z, 0.0) + jnp.log(1.0 + jnp.exp(-jnp.abs(z)))
    a = dtv * a_neg
    cs = jnp.dot(tri, a, precision=hp, preferred_element_type=F32)
    cs_t = jnp.dot(a.T, tri_t, precision=hp, preferred_element_type=F32)
    tot = jnp.sum(a, axis=0, keepdims=True)

    heads_per_group = n_heads // SSM_GROUPS
    for g in range(SSM_GROUPS):
        bg = bm_ref[0, :, g * SSM_STATE:(g + 1) * SSM_STATE]
        cg = cm_ref[0, :, g * SSM_STATE:(g + 1) * SSM_STATE]
        bg16 = bg.astype(BF16)
        cg16 = cg.astype(BF16)
        gmat = _dot_nt(cg16, bg16)
        bgt16 = bg.T.astype(BF16)
        for hh in range(heads_per_group):
            h = g * heads_per_group + hh
            j = col0 + h
            col = cs[:, j:j + 1]
            rowv = cs_t[j:j + 1, :]
            lmat = jnp.exp(jnp.where(mask, col - rowv, -jnp.inf))
            m16 = (gmat * lmat).astype(BF16)
            xdt = xs_ref[0, :, h * SSM_HEAD_DIM:(h + 1) * SSM_HEAD_DIM] * dtv[:, j:j + 1]
            st = st_ref[h]
            if y_ref is not None:
                y = jnp.dot(m16, xdt.astype(BF16), preferred_element_type=F32)
                y = y + jnp.dot(cg16, st.astype(BF16), preferred_element_type=F32) * jnp.exp(col)
                y_ref[0, :, h * SSM_HEAD_DIM:(h + 1) * SSM_HEAD_DIM] = y
            tj = tot[:, j:j + 1]
            xw = (xdt * jnp.exp(tj - col)).astype(BF16)
            st_ref[h] = jnp.exp(tj) * st + jnp.dot(bgt16, xw, preferred_element_type=F32)


def _ssd_kernel(n_heads, emit_y, *refs):
    (dtf_ref, dtb_ref, bias_ref, a_ref, xsf_ref, xsb_ref, bmf_ref, bmb_ref, cmf_ref, cmb_ref,
     init_ref) = refs[:11]
    if emit_y:
        yf_ref, yb_ref, fin_ref, stf_ref, stb_ref = refs[11:]
    else:
        yf_ref = yb_ref = None
        fin_ref, stf_ref, stb_ref = refs[11:]
    ci = pl.program_id(1)

    @pl.when(ci == 0)
    def _():
        stf_ref[...] = init_ref[0, 0]
        stb_ref[...] = init_ref[0, 1]

    bias = bias_ref[...]
    a_neg = a_ref[...]
    _ssd_direction(False, dtf_ref[0], bias, a_neg, xsf_ref, bmf_ref, cmf_ref, stf_ref, yf_ref, n_heads)
    _ssd_direction(True, dtb_ref[0], bias, a_neg, xsb_ref, bmb_ref, cmb_ref, stb_ref, yb_ref, n_heads)

    @pl.when(ci == pl.num_programs(1) - 1)
    def _():
        fin_ref[0, 0] = stf_ref[...]
        fin_ref[0, 1] = stb_ref[...]


def _ssd(proj, dt_col0, xbc, bias_row, a_row, init, emit_y):
    b, l, _ = proj.shape
    d_ssm = xbc.shape[2] - 2 * SSM_GROUPS * SSM_STATE
    d_bc = SSM_GROUPS * SSM_STATE
    n_heads = d_ssm // SSM_HEAD_DIM
    t = SSM_CHUNK
    nc = l // t
    dtb = dt_col0 // LANE
    fwd = lambda bi, c: (bi, c, 0)
    bwd = lambda bi, c: (bi, nc - 1 - c, 0)
    off = lambda f, o: (lambda bi, c: f(bi, c)[:2] + (o,))
    st_shape = (n_heads, SSM_STATE, SSM_HEAD_DIM)
    in_specs = [
        pl.BlockSpec((1, t, LANE), off(fwd, dtb)), pl.BlockSpec((1, t, LANE), off(bwd, dtb)),
        pl.BlockSpec((1, LANE), lambda bi, c: (0, 0)), pl.BlockSpec((1, LANE), lambda bi, c: (0, 0)),
        pl.BlockSpec((1, t, d_ssm), fwd), pl.BlockSpec((1, t, d_ssm), bwd),
        pl.BlockSpec((1, t, d_bc), off(fwd, d_ssm // d_bc)), pl.BlockSpec((1, t, d_bc), off(bwd, d_ssm // d_bc)),
        pl.BlockSpec((1, t, d_bc), off(fwd, d_ssm // d_bc + 1)), pl.BlockSpec((1, t, d_bc), off(bwd, d_ssm // d_bc + 1)),
        pl.BlockSpec((1, 2) + st_shape, lambda bi, c: (bi, 0, 0, 0, 0)),
    ]
    fin_spec = pl.BlockSpec((1, 2) + st_shape, lambda bi, c: (bi, 0, 0, 0, 0))
    fin_shape = jax.ShapeDtypeStruct((b, 2) + st_shape, F32)
    if emit_y:
        y_shape = jax.ShapeDtypeStruct((b, l, d_ssm), F32)
        out_specs = [pl.BlockSpec((1, t, d_ssm), fwd), pl.BlockSpec((1, t, d_ssm), bwd), fin_spec]
        out_shape = [y_shape, y_shape, fin_shape]
    else:
        out_specs = [fin_spec]
        out_shape = [fin_shape]
    return pl.pallas_call(
        functools.partial(_ssd_kernel, n_heads, emit_y),
        grid=(b, nc),
        in_specs=in_specs,
        out_specs=out_specs,
        out_shape=out_shape,
        scratch_shapes=[pltpu.VMEM(st_shape, F32), pltpu.VMEM(st_shape, F32)],
        compiler_params=_cparams(("arbitrary", "arbitrary")),
        name="ssd_scan_y" if emit_y else "ssd_scan_state",
    )(proj, proj, bias_row, a_row, xbc, xbc, xbc, xbc, xbc, xbc, init)


def _outproj_kernel(of_ref, yf_ref, yb_ref, xs_ref, z_ref, x_ref, dsk_ref, nw_ref, wo_ref, gpost_ref,
                    gm_ref, gpre_ref, cf_ref, sf_ref, wq_ref, x1_ref, h2_ref, q_ref):
    d_f = of_ref.shape[2]
    y = yf_ref[0] + yb_ref[0] + dsk_ref[...] * xs_ref[0]
    y = y * _silu(z_ref[0])
    gw = y.shape[1] // SSM_GROUPS
    nw = nw_ref[...]
    parts = [_rms(y[:, g * gw:(g + 1) * gw], nw[:, g * gw:(g + 1) * gw]).astype(BF16) for g in range(SSM_GROUPS)]
    yx = jnp.dot(of_ref[0], wo_ref[:d_f, :], preferred_element_type=F32)
    for g in range(SSM_GROUPS):
        yx = yx + jnp.dot(parts[g], wo_ref[d_f + g * gw:d_f + (g + 1) * gw, :], preferred_element_type=F32)
    x1 = x_ref[0] + gm_ref[0] * _rms(yx, gpost_ref[...])
    x1_ref[0] = x1
    h2 = (_rms(x1, gpre_ref[...]) * (1.0 + cf_ref[0]) + sf_ref[0]).astype(BF16)
    h2_ref[0] = h2
    q_ref[0] = jnp.dot(h2, wq_ref[...], preferred_element_type=F32)


def _outproj(of, yf, yb, xbc, proj, z_col0, x, dsk, nw, w_out, g_post, gm, g_pre, cf, sf, w_query, tm=256):
    b, l, d = x.shape
    d_f = of.shape[2]
    d_s = yf.shape[2]
    nq = w_query.shape[1]
    zb = z_col0 // d_s
    row = lambda bi, i: (bi, i, 0)
    vec = lambda n: pl.BlockSpec((1, n), lambda bi, i: (0, 0))
    bvec = lambda n: pl.BlockSpec((1, 1, n), lambda bi, i: (bi, 0, 0))
    return pl.pallas_call(
        _outproj_kernel,
        grid=(b, l // tm),
        in_specs=[pl.BlockSpec((1, tm, d_f), row), pl.BlockSpec((1, tm, d_s), row), pl.BlockSpec((1, tm, d_s), row),
                  pl.BlockSpec((1, tm, d_s), row), pl.BlockSpec((1, tm, d_s), lambda bi, i: (bi, i, zb)),
                  pl.BlockSpec((1, tm, d), row), vec(d_s), vec(d_s),
                  pl.BlockSpec((d_f + d_s, d), lambda bi, i: (0, 0)), vec(d), bvec(d), vec(d), bvec(d), bvec(d),
                  pl.BlockSpec((d, nq), lambda bi, i: (0, 0))],
        out_specs=[pl.BlockSpec((1, tm, d), row), pl.BlockSpec((1, tm, d), row), pl.BlockSpec((1, tm, nq), row)],
        out_shape=[jax.ShapeDtypeStruct((b, l, d), F32), jax.ShapeDtypeStruct((b, l, d), BF16),
                   jax.ShapeDtypeStruct((b, l, nq), F32)],
        compiler_params=_cparams(("arbitrary", "arbitrary")),
        name="outproj_prenorm_query",
    )(of, yf, yb, xbc, proj, x, dsk, nw, w_out, g_post, gm, g_pre, cf, sf, w_query)


def _top16_ranked(s):
    iota = lax.broadcasted_iota(jnp.int32, s.shape, 0)
    cur = s
    rank = jnp.full(s.shape, float(PEER_TOPK), F32)
    vals = []
    for k in range(PEER_TOPK):
        m = jnp.max(cur, axis=0, keepdims=True)
        idx = jnp.min(jnp.where(cur == m, iota, s.shape[0]), axis=0, keepdims=True)
        hit = iota == idx
        rank = jnp.where(hit, float(k), rank)
        cur = jnp.where(hit, -jnp.inf, cur)
        vals.append(m)
    return vals, rank


def _peer_route_kernel(q_ref, k1_ref, k2_ref, rank2_ref, cnt_ref, e1_ref, e2_ref):
    hp = lax.Precision.HIGHEST
    kk = PEER_TOPK
    qh = q_ref[...]
    s1 = _dot_nt(k1_ref[0], qh[:, :PEER_HALF], precision=hp)
    s2 = _dot_nt(k2_ref[0], qh[:, PEER_HALF:], precision=hp)
    v1, rank1 = _top16_ranked(s1)
    v2, rank2 = _top16_ranked(s2)

    sub = lax.broadcasted_iota(jnp.int32, (8, s1.shape[1]), 0)

    def stack8(vals):
        out = jnp.zeros(sub.shape, F32)
        for k, v in enumerate(vals):
            out = jnp.where(sub == k, v, out)
        return out

    v2_lo, v2_hi, v1_hi = stack8(v2[:8]), stack8(v2[8:]), stack8(v1[8:])
    cands = [v1[0] + v2_lo, v1[0] + v2_hi] + [v1[a] + v2_lo for a in range(1, 8)] + [v1_hi + v2[0]]
    poses = [sub, sub + 8] + [a * kk + sub for a in range(1, 8)] + [(sub + 8) * kk]
    sels = [jnp.zeros(sub.shape, F32) for _ in cands]
    m0 = v1[0] + v2[0]
    zsum = jnp.zeros_like(m0)
    big = kk * kk
    for _ in range(kk):
        m = jnp.max(functools.reduce(jnp.maximum, cands), axis=0, keepdims=True)
        firsts = [jnp.where(cd == m, ps, big) for cd, ps in zip(cands, poses)]
        p = jnp.min(functools.reduce(jnp.minimum, firsts), axis=0, keepdims=True)
        hits = [ps == p for ps in poses]
        sels = [jnp.where(ht, 1.0, sl) for ht, sl in zip(hits, sels)]
        cands = [jnp.where(ht, -jnp.inf, cd) for ht, cd in zip(hits, cands)]
        zsum = zsum + jnp.exp(m - m0)
    n_rows = [jnp.sum(sels[0] + sels[1], axis=0, keepdims=True)]
    n_rows += [jnp.sum(sels[a + 1], axis=0, keepdims=True) for a in range(1, 8)]
    n_rows += [jnp.sum(jnp.where(sub == r, sels[9], 0.0), axis=0, keepdims=True) for r in range(8)]
    cnt = jnp.zeros(rank1.shape, F32)
    for a in range(kk):
        cnt = jnp.where(rank1 == float(a), n_rows[a], cnt)
    rank2_ref[0] = rank2
    cnt_ref[0] = cnt
    e1_ref[0] = jnp.exp(s1 - v1[0])
    e2_ref[0] = jnp.exp(s2 - v2[0]) / zsum


def _peer_route(q, k1, k2, tq=256):
    t, _ = q.shape
    nh, nk, hd = k1.shape
    out = jax.ShapeDtypeStruct((nh, nk, t), F32)
    spec = pl.BlockSpec((1, nk, tq), lambda i, h: (h, 0, i))
    return pl.pallas_call(
        _peer_route_kernel,
        grid=(t // tq, nh),
        in_specs=[pl.BlockSpec((tq, PEER_KEY_DIM), lambda i, h: (i, h)),
                  pl.BlockSpec((1, nk, hd), lambda i, h: (h, 0, 0)),
                  pl.BlockSpec((1, nk, hd), lambda i, h: (h, 0, 0))],
        out_specs=[spec, spec, spec, spec],
        out_shape=[out, out, out, out],
        compiler_params=_cparams(("arbitrary", "arbitrary")),
        name="peer_route",
    )(q, k1, k2)


def _peer_dense_kernel(i_per_step, h2_ref, u_ref, vt_ref, rank2_ref, cnt_ref, e1_ref, e2_ref, x1_ref, gf_ref,
                       gpost_ref, o_ref, acc_ref, g_ref):
    j = pl.program_id(1)

    @pl.when(j == 0)
    def _():
        acc_ref[...] = jnp.zeros_like(acc_ref)

    at = _dot_nt(u_ref[...], h2_ref[...])
    nk = PEER_N_KEYS
    for il in range(i_per_step):
        a = at[il * nk:(il + 1) * nk, :]
        w = jnp.zeros(a.shape, F32)
        for h in range(PEER_HEADS):
            row = pl.ds(j * i_per_step + il, 1)
            w = w + jnp.where(rank2_ref[h] < cnt_ref[h, row, :], e2_ref[h], 0.0) * e1_ref[h, row, :]
        gelu = 0.5 * a * (1.0 + lax.erf(a * (1.0 / math.sqrt(2.0))))
        g_ref[il * nk:(il + 1) * nk, :] = (gelu * w).astype(BF16)
    acc_ref[...] += jnp.dot(vt_ref[...], g_ref[...], preferred_element_type=F32)

    @pl.when(j == pl.num_programs(1) - 1)
    def _():
        y = acc_ref[...].T
        o_ref[...] = x1_ref[...] + gf_ref[0] * _rms(y, gpost_ref[...])


def _peer_dense(h2, u16, vt16, rank2, cnt, e1, e2, x1, gf, g_post, seq_len, tm=512, te=512):
    t, d = h2.shape
    ne = u16.shape[0]
    nh, nk, _ = rank2.shape
    i_per_step = te // nk
    aux = pl.BlockSpec((nh, nk, tm), lambda i, j: (0, 0, i))
    blocks_per_seq = seq_len // tm
    return pl.pallas_call(
        functools.partial(_peer_dense_kernel, i_per_step),
        grid=(t // tm, ne // te),
        in_specs=[pl.BlockSpec((tm, d), lambda i, j: (i, 0)),
                  pl.BlockSpec((te, d), lambda i, j: (j, 0)),
                  pl.BlockSpec((d, te), lambda i, j: (0, j)),
                  aux, aux, aux, aux,
                  pl.BlockSpec((tm, d), lambda i, j: (i, 0)),
                  pl.BlockSpec((1, 1, d), lambda i, j: (i // blocks_per_seq, 0, 0)),
                  pl.BlockSpec((1, d), lambda i, j: (0, 0))],
        out_specs=pl.BlockSpec((tm, d), lambda i, j: (i, 0)),
        out_shape=jax.ShapeDtypeStruct((t, d), F32),
        scratch_shapes=[pltpu.VMEM((d, tm), F32), pltpu.VMEM((te, tm), BF16)],
        compiler_params=_cparams(("arbitrary", "arbitrary")),
        name="peer_dense",
    )(h2, u16, vt16, rank2, cnt, e1, e2, x1, gf, g_post)


def _dft_tables(n):
    k = np.arange(n, dtype=np.int64)
    ph = (np.outer(k, k) % n).astype(np.float64) * (2.0 * np.pi / n)
    return np.cos(ph), np.sin(ph)


def kernel(x, c, ctx, c_ctx, w_mod, b_mod, g_pre_mix, g_post_mix, g_pre_ffn, g_post_ffn, w_in, w_fmix, conv_w, conv_b, dt_bias_f, dt_bias_b, a_log_f, a_log_b, d_skip_f, d_skip_b, ssm_norm_w, w_out, w_query, sub_keys_1, sub_keys_2, expert_u, expert_v):
    bsz, seq, d = x.shape
    ctx_len = ctx.shape[1]
    layer = 0
    n_heads = dt_bias_f.shape[1]
    d_ssm = n_heads * SSM_HEAD_DIM
    d_f = d - d_ssm
    gc = d_f // N_FOURIER_GROUPS
    d_xbc = conv_w.shape[2]
    assert w_mod.shape[0] == 1, "single-layer kernel"

    cond = jnp.zeros((8, d), F32).at[:bsz].set(c).at[bsz].set(c_ctx)
    mod = _adaln(cond, w_mod[layer], b_mod[layer])
    mods = [mod[:, i * d:(i + 1) * d] for i in range(N_MOD)]
    sm_x, cm_x, gm_x, sf_x, cf_x, gf_x = [m[:bsz, None, :] for m in mods]
    sm_c = jnp.broadcast_to(mods[0][bsz][None, None, :], (bsz, 1, d))
    cm_c = jnp.broadcast_to(mods[1][bsz][None, None, :], (bsz, 1, d))

    cos_l, sin_l = _dft_tables(seq)
    cos_c, sin_c = _dft_tables(gc)
    scale = 1.0 / math.sqrt(seq * gc)
    wfc, wfs = _fold_fourier(w_in[layer], w_fmix[layer],
                             jnp.asarray(cos_c * scale, F32), jnp.asarray(sin_c * scale, F32))
    dt_pad = 512
    w_rest = w_in[layer][:, d_f:].astype(BF16)
    w_all = jnp.concatenate(
        [wfc, wfs, w_rest, jnp.zeros((d, dt_pad - 2 * n_heads), BF16)], axis=1)
    col_z = 2 * d_f
    col_xbc = col_z + d_ssm
    col_dt = col_xbc + d_xbc
    n_all = col_dt + dt_pad
    dft = jnp.asarray(np.concatenate([cos_l, -sin_l], axis=1), F32).astype(BF16)

    pad32 = lambda f, b_: jnp.zeros((1, LANE), F32).at[0, :n_heads].set(f).at[0, n_heads:2 * n_heads].set(b_)
    bias_row = pad32(dt_bias_f[layer], dt_bias_b[layer])
    a_row = pad32(-jnp.exp(a_log_f[layer]), -jnp.exp(a_log_b[layer]))

    proj_c = _inproj(ctx, g_pre_mix[layer], sm_c, cm_c, w_all, col_xbc, n_all - col_xbc, tm=ctx_len, tn=512)
    xbc_c = _conv_silu(proj_c, 0, conv_w[layer], conv_b[layer])
    zero_state = jnp.zeros((bsz, 2, n_heads, SSM_STATE, SSM_HEAD_DIM), F32)
    (fin_c,) = _ssd(proj_c, d_xbc, xbc_c, bias_row, a_row, zero_state, emit_y=False)

    proj_x = _inproj(x, g_pre_mix[layer], sm_x, cm_x, w_all, 0, n_all, tm=min(1024, seq), tn=512)
    o_f = _seqdft(dft, proj_x, d_f)
    xbc_x = _conv_silu(proj_x, col_xbc, conv_w[layer], conv_b[layer])
    y_f, y_b, _ = _ssd(proj_x, col_dt, xbc_x, bias_row, a_row, fin_c, emit_y=True)

    dsk = jnp.repeat(d_skip_f[layer] + d_skip_b[layer], SSM_HEAD_DIM)[None, :]
    x1, h2, q = _outproj(o_f, y_f, y_b, xbc_x, proj_x, col_z, x, dsk, ssm_norm_w[layer][None, :],
                         w_out[layer].astype(BF16), g_post_mix[layer][None, :], gm_x,
                         g_pre_ffn[layer][None, :], cf_x, sf_x, w_query[layer].astype(BF16))

    t = bsz * seq
    rank2, cnt, e1, e2 = _peer_route(q.reshape(t, -1), sub_keys_1[layer], sub_keys_2[layer])
    u16 = expert_u[layer].astype(BF16)
    vt16 = expert_v[layer].T.astype(BF16)
    out = _peer_dense(h2.reshape(t, d), u16, vt16, rank2, cnt, e1, e2, x1.reshape(t, d), gf_x,
                      g_post_ffn[layer][None, :], seq)
    return out.reshape(bsz, seq, d)
```

```python
import functools
import math

import jax
import jax.numpy as jnp
import numpy as np
from jax import lax
from jax.experimental import pallas as pl
from jax.experimental.pallas import tpu as pltpu

F32 = jnp.float32
BF16 = jnp.bfloat16

EPS = 1e-6
N_MOD = 6
N_FOURIER_GROUPS = 4
SSM_HEAD_DIM = 64
SSM_GROUPS = 2
SSM_STATE = 128
SSM_CHUNK = 128
PEER_HEADS = 8
PEER_TOPK = 16
PEER_N_KEYS = 128
PEER_KEY_DIM = 128
PEER_HALF = PEER_KEY_DIM // 2

LANE = 128
VMEM_LIMIT = 56 * 1024 * 1024


def _cparams(sem):
    return pltpu.CompilerParams(dimension_semantics=sem, vmem_limit_bytes=VMEM_LIMIT)


def _bdot(a, b):
    return jnp.dot(a.astype(BF16), b.astype(BF16), preferred_element_type=F32)


def _dot_nt(a, b, precision=None):
    return lax.dot_general(a, b, (((1,), (1,)), ((), ())), precision=precision,
                           preferred_element_type=F32)


def _rms(u, g):
    return u * lax.rsqrt(jnp.mean(u * u, axis=-1, keepdims=True) + EPS) * g


def _silu(u):
    return u * (1.0 / (1.0 + jnp.exp(-u)))


def _adaln_kernel(c_ref, w_ref, b_ref, o_ref):
    a = _silu(c_ref[...])
    o_ref[...] = _bdot(a, w_ref[...]) + b_ref[...]


def _adaln(cond, w_mod, b_mod, tn=1024):
    rows, d = cond.shape
    n = w_mod.shape[1]
    return pl.pallas_call(
        _adaln_kernel,
        grid=(n // tn,),
        in_specs=[pl.BlockSpec((rows, d), lambda j: (0, 0)),
                  pl.BlockSpec((d, tn), lambda j: (0, j)),
                  pl.BlockSpec((1, tn), lambda j: (0, j))],
        out_specs=pl.BlockSpec((rows, tn), lambda j: (0, j)),
        out_shape=jax.ShapeDtypeStruct((rows, n), F32),
        compiler_params=_cparams(("arbitrary",)),
        name="adaln",
    )(cond, w_mod, b_mod.reshape(1, n))


def _fold_kernel(w_ref, fm_ref, cc_ref, sc_ref, oc_ref, os_ref):
    hp = lax.Precision.HIGHEST
    w = w_ref[...]
    fm = fm_ref[0]
    mc = jnp.dot(cc_ref[...], fm, precision=hp, preferred_element_type=F32)
    ms = jnp.dot(sc_ref[...], fm, precision=hp, preferred_element_type=F32)
    oc_ref[...] = jnp.dot(w, mc, precision=hp, preferred_element_type=F32).astype(BF16)
    os_ref[...] = jnp.dot(w, ms, precision=hp, preferred_element_type=F32).astype(BF16)


def _fold_fourier(w_in, w_fmix, cos_c, sin_c):
    d = w_in.shape[0]
    ng, gc, _ = w_fmix.shape
    out = jax.ShapeDtypeStruct((d, ng * gc), BF16)
    return pl.pallas_call(
        _fold_kernel,
        grid=(ng,),
        in_specs=[pl.BlockSpec((d, gc), lambda g: (0, g)),
                  pl.BlockSpec((1, gc, gc), lambda g: (g, 0, 0)),
                  pl.BlockSpec((gc, gc), lambda g: (0, 0)),
                  pl.BlockSpec((gc, gc), lambda g: (0, 0))],
        out_specs=[pl.BlockSpec((d, gc), lambda g: (0, g)),
                   pl.BlockSpec((d, gc), lambda g: (0, g))],
        out_shape=[out, out],
        compiler_params=_cparams(("arbitrary",)),
        name="fourier_fold",
    )(w_in, w_fmix, cos_c, sin_c)


def _inproj_kernel(x_ref, g_ref, sh_ref, sc_ref, w_ref, o_ref, h_ref):
    @pl.when(pl.program_id(2) == 0)
    def _():
        slab = min(LANE, h_ref.shape[0])

        def body(r, carry):
            rows = pl.ds(pl.multiple_of(r * slab, slab), slab)
            h = _rms(x_ref[0, rows, :], g_ref[...]) * (1.0 + sc_ref[0]) + sh_ref[0]
            h_ref[rows, :] = h.astype(BF16)
            return carry

        lax.fori_loop(0, h_ref.shape[0] // slab, body, 0)

    o_ref[0] = jnp.dot(h_ref[...], w_ref[...], preferred_element_type=F32)


def _inproj(xin, g, shift, scale, w_all, col0, ncols, tm, tn):
    b, l, d = xin.shape
    jb = col0 // tn
    return pl.pallas_call(
        _inproj_kernel,
        grid=(b, l // tm, ncols // tn),
        in_specs=[pl.BlockSpec((1, tm, d), lambda bi, i, j: (bi, i, 0)),
                  pl.BlockSpec((1, d), lambda bi, i, j: (0, 0)),
                  pl.BlockSpec((1, 1, d), lambda bi, i, j: (bi, 0, 0)),
                  pl.BlockSpec((1, 1, d), lambda bi, i, j: (bi, 0, 0)),
                  pl.BlockSpec((d, tn), lambda bi, i, j: (0, j + jb))],
        out_specs=pl.BlockSpec((1, tm, tn), lambda bi, i, j: (bi, i, j)),
        out_shape=jax.ShapeDtypeStruct((b, l, ncols), F32),
        scratch_shapes=[pltpu.VMEM((tm, d), BF16)],
        compiler_params=_cparams(("arbitrary", "arbitrary", "arbitrary")),
        name="inproj",
    )(xin, g.reshape(1, d), shift, scale, w_all)


def _seqdft_kernel(a_ref, x_ref, o_ref, acc_ref):
    k = pl.program_id(1)

    @pl.when(k == 0)
    def _():
        acc_ref[...] = jnp.zeros_like(acc_ref)

    acc_ref[...] += jnp.dot(a_ref[...], x_ref[0].astype(BF16), preferred_element_type=F32)

    @pl.when(k == pl.num_programs(1) - 1)
    def _():
        o_ref[0] = acc_ref[...].astype(BF16)


def _seqdft(dft, proj, n_f, tk=512):
    b, l, _ = proj.shape
    kl = l // tk
    return pl.pallas_call(
        _seqdft_kernel,
        grid=(b, 2 * kl),
        in_specs=[pl.BlockSpec((l, tk), lambda bi, k: (0, k)),
                  pl.BlockSpec((1, tk, n_f), lambda bi, k: (bi, k % kl, k // kl))],
        out_specs=pl.BlockSpec((1, l, n_f), lambda bi, k: (bi, 0, 0)),
        out_shape=jax.ShapeDtypeStruct((b, l, n_f), BF16),
        scratch_shapes=[pltpu.VMEM((l, n_f), F32)],
        compiler_params=_cparams(("arbitrary", "arbitrary")),
        name="seq_dft",
    )(dft, proj)


def _conv_kernel(u_ref, w_ref, b_ref, o_ref):
    u = u_ref[0]
    l = u.shape[0]
    row = lax.broadcasted_iota(jnp.int32, u.shape, 0)
    prev = jnp.where(row == 0, 0.0, pltpu.roll(u, 1, 0))
    nxt = jnp.where(row == l - 1, 0.0, pltpu.roll(u, l - 1, 0))
    w = w_ref[...]
    o_ref[0] = _silu(prev * w[0:1] + u * w[1:2] + nxt * w[2:3] + b_ref[...])


def _conv_silu(proj, col0, conv_w, conv_b, tc=256):
    b, l, _ = proj.shape
    n = conv_w.shape[1]
    jb = col0 // tc
    return pl.pallas_call(
        _conv_kernel,
        grid=(b, n // tc),
        in_specs=[pl.BlockSpec((1, l, tc), lambda bi, j: (bi, 0, j + jb)),
                  pl.BlockSpec((3, tc), lambda bi, j: (0, j)),
                  pl.BlockSpec((1, tc), lambda bi, j: (0, j))],
        out_specs=pl.BlockSpec((1, l, tc), lambda bi, j: (bi, 0, j)),
        out_shape=jax.ShapeDtypeStruct((b, l, n), F32),
        compiler_params=_cparams(("arbitrary", "arbitrary")),
        name="conv_silu",
    )(proj, conv_w, conv_b.reshape(1, n))


def _ssd_direction(reverse, dt_raw, bias, a_neg, xs_ref, bm_ref, cm_ref, st_ref, y_ref, n_heads):
    hp = lax.Precision.HIGHEST
    t = SSM_CHUNK
    p2 = 2 * SSM_HEAD_DIM
    col0 = n_heads if reverse else 0
    r = lax.broadcasted_iota(jnp.int32, (t, t), 0)
    c = lax.broadcasted_iota(jnp.int32, (t, t), 1)
    mask = (r <= c) if reverse else (r >= c)
    tri = mask.astype(F32)
    tri_t = ((r >= c) if reverse else (r <= c)).astype(F32)
    first_y = lax.broadcasted_iota(jnp.int32, (t, p2), 1) < SSM_HEAD_DIM
    first_s = lax.broadcasted_iota(jnp.int32, (SSM_STATE, p2), 1) < SSM_HEAD_DIM

    z = dt_raw + bias
    dtv = jnp.maximum(z, 0.0) + jnp.log(1.0 + jnp.exp(-jnp.abs(z)))
    a = dtv * a_neg
    a_t = a.T
    dtv_t = dtv.T
    cs = jnp.dot(tri, a, precision=hp, preferred_element_type=F32)
    cs_t = jnp.dot(a_t, tri_t, precision=hp, preferred_element_type=F32)
    w_t = dtv_t * jnp.exp(jnp.sum(a_t, axis=1, keepdims=True) - cs_t)
    etot = jnp.exp(jnp.sum(a, axis=0, keepdims=True))

    ppg = n_heads // SSM_GROUPS // 2
    for g in range(SSM_GROUPS):
        bg = bm_ref[0, :, g * SSM_STATE:(g + 1) * SSM_STATE]
        cg16 = cm_ref[0, :, g * SSM_STATE:(g + 1) * SSM_STATE].astype(BF16)
        gmat = _dot_nt(cg16, bg.astype(BF16))
        bgt = bg.T
        if y_ref is not None:
            st_g16 = jnp.concatenate([st_ref[g * ppg + k] for k in range(ppg)], axis=1).astype(BF16)
            yoff_g = jnp.dot(cg16, st_g16, preferred_element_type=F32)
        for k in range(ppg):
            pi = g * ppg + k
            xs16 = xs_ref[0, :, pi * p2:(pi + 1) * p2].astype(BF16)
            ys, ss, ecols, etots = [], [], [], []
            for e in range(2):
                j = col0 + 2 * pi + e
                col_b = jnp.broadcast_to(cs[:, j:j + 1], (t, t))
                lmat = jnp.exp(jnp.where(mask, col_b - cs_t[j:j + 1, :], -jnp.inf))
                m16 = (gmat * lmat * dtv_t[j:j + 1, :]).astype(BF16)
                if y_ref is not None:
                    ys.append(jnp.dot(m16, xs16, preferred_element_type=F32))
                    ecols.append(jnp.exp(col_b))
                bw16 = (bgt * w_t[j:j + 1, :]).astype(BF16)
                ss.append(jnp.dot(bw16, xs16, preferred_element_type=F32))
                etots.append(etot[:, j:j + 1])
            if y_ref is not None:
                y_off = yoff_g[:, k * p2:(k + 1) * p2] * jnp.where(first_y, ecols[0], ecols[1])
                y_ref[0, :, pi * p2:(pi + 1) * p2] = jnp.where(first_y, ys[0], ys[1]) + y_off
            st_ref[pi] = (jnp.where(first_s, etots[0], etots[1]) * st_ref[pi]
                          + jnp.where(first_s, ss[0], ss[1]))


def _ssd_kernel(n_heads, emit_y, *refs):
    (dtf_ref, dtb_ref, bias_ref, a_ref, xsf_ref, xsb_ref, bmf_ref, bmb_ref, cmf_ref, cmb_ref,
     init_ref) = refs[:11]
    if emit_y:
        yf_ref, yb_ref, fin_ref, stf_ref, stb_ref = refs[11:]
    else:
        yf_ref = yb_ref = None
        fin_ref, stf_ref, stb_ref = refs[11:]
    ci = pl.program_id(1)

    @pl.when(ci == 0)
    def _():
        stf_ref[...] = init_ref[0, 0]
        stb_ref[...] = init_ref[0, 1]

    bias = bias_ref[...]
    a_neg = a_ref[...]
    _ssd_direction(False, dtf_ref[0], bias, a_neg, xsf_ref, bmf_ref, cmf_ref, stf_ref, yf_ref, n_heads)
    _ssd_direction(True, dtb_ref[0], bias, a_neg, xsb_ref, bmb_ref, cmb_ref, stb_ref, yb_ref, n_heads)

    @pl.when(ci == pl.num_programs(1) - 1)
    def _():
        fin_ref[0, 0] = stf_ref[...]
        fin_ref[0, 1] = stb_ref[...]


def _ssd(proj, dt_col0, xbc, bias_row, a_row, init, emit_y):
    b, l, _ = proj.shape
    d_ssm = xbc.shape[2] - 2 * SSM_GROUPS * SSM_STATE
    d_bc = SSM_GROUPS * SSM_STATE
    n_heads = d_ssm // SSM_HEAD_DIM
    t = SSM_CHUNK
    nc = l // t
    dtb = dt_col0 // LANE
    fwd = lambda bi, c: (bi, c, 0)
    bwd = lambda bi, c: (bi, nc - 1 - c, 0)
    off = lambda f, o: (lambda bi, c: f(bi, c)[:2] + (o,))
    st_shape = (n_heads // 2, SSM_STATE, 2 * SSM_HEAD_DIM)
    in_specs = [
        pl.BlockSpec((1, t, LANE), off(fwd, dtb)), pl.BlockSpec((1, t, LANE), off(bwd, dtb)),
        pl.BlockSpec((1, LANE), lambda bi, c: (0, 0)), pl.BlockSpec((1, LANE), lambda bi, c: (0, 0)),
        pl.BlockSpec((1, t, d_ssm), fwd), pl.BlockSpec((1, t, d_ssm), bwd),
        pl.BlockSpec((1, t, d_bc), off(fwd, d_ssm // d_bc)), pl.BlockSpec((1, t, d_bc), off(bwd, d_ssm // d_bc)),
        pl.BlockSpec((1, t, d_bc), off(fwd, d_ssm // d_bc + 1)), pl.BlockSpec((1, t, d_bc), off(bwd, d_ssm // d_bc + 1)),
        pl.BlockSpec((1, 2) + st_shape, lambda bi, c: (bi, 0, 0, 0, 0)),
    ]
    fin_spec = pl.BlockSpec((1, 2) + st_shape, lambda bi, c: (bi, 0, 0, 0, 0))
    fin_shape = jax.ShapeDtypeStruct((b, 2) + st_shape, F32)
    if emit_y:
        y_shape = jax.ShapeDtypeStruct((b, l, d_ssm), F32)
        out_specs = [pl.BlockSpec((1, t, d_ssm), fwd), pl.BlockSpec((1, t, d_ssm), bwd), fin_spec]
        out_shape = [y_shape, y_shape, fin_shape]
    else:
        out_specs = [fin_spec]
        out_shape = [fin_shape]
    return pl.pallas_call(
        functools.partial(_ssd_kernel, n_heads, emit_y),
        grid=(b, nc),
        in_specs=in_specs,
        out_specs=out_specs,
        out_shape=out_shape,
        scratch_shapes=[pltpu.VMEM(st_shape, F32), pltpu.VMEM(st_shape, F32)],
        compiler_params=_cparams(("arbitrary", "arbitrary")),
        name="ssd_scan_y" if emit_y else "ssd_scan_state",
    )(proj, proj, bias_row, a_row, xbc, xbc, xbc, xbc, xbc, xbc, init)


def _outproj_kernel(of_ref, yf_ref, yb_ref, xs_ref, z_ref, x_ref, dsk_ref, nw_ref, wo_ref, gpost_ref,
                    gm_ref, gpre_ref, cf_ref, sf_ref, wq_ref, x1_ref, h2_ref, q_ref):
    d_f = of_ref.shape[2]
    y = yf_ref[0] + yb_ref[0] + dsk_ref[...] * xs_ref[0]
    y = y * _silu(z_ref[0])
    gw = y.shape[1] // SSM_GROUPS
    nw = nw_ref[...]
    parts = [_rms(y[:, g * gw:(g + 1) * gw], nw[:, g * gw:(g + 1) * gw]).astype(BF16) for g in range(SSM_GROUPS)]
    yx = jnp.dot(of_ref[0], wo_ref[:d_f, :], preferred_element_type=F32)
    for g in range(SSM_GROUPS):
        yx = yx + jnp.dot(parts[g], wo_ref[d_f + g * gw:d_f + (g + 1) * gw, :], preferred_element_type=F32)
    x1 = x_ref[0] + gm_ref[0] * _rms(yx, gpost_ref[...])
    x1_ref[0] = x1
    h2 = (_rms(x1, gpre_ref[...]) * (1.0 + cf_ref[0]) + sf_ref[0]).astype(BF16)
    h2_ref[0] = h2
    q_ref[0] = jnp.dot(h2, wq_ref[...], preferred_element_type=F32)


def _outproj(of, yf, yb, xbc, proj, z_col0, x, dsk, nw, w_out, g_post, gm, g_pre, cf, sf, w_query, tm=256):
    b, l, d = x.shape
    d_f = of.shape[2]
    d_s = yf.shape[2]
    nq = w_query.shape[1]
    zb = z_col0 // d_s
    row = lambda bi, i: (bi, i, 0)
    vec = lambda n: pl.BlockSpec((1, n), lambda bi, i: (0, 0))
    bvec = lambda n: pl.BlockSpec((1, 1, n), lambda bi, i: (bi, 0, 0))
    return pl.pallas_call(
        _outproj_kernel,
        grid=(b, l // tm),
        in_specs=[pl.BlockSpec((1, tm, d_f), row), pl.BlockSpec((1, tm, d_s), row), pl.BlockSpec((1, tm, d_s), row),
                  pl.BlockSpec((1, tm, d_s), row), pl.BlockSpec((1, tm, d_s), lambda bi, i: (bi, i, zb)),
                  pl.BlockSpec((1, tm, d), row), vec(d_s), vec(d_s),
                  pl.BlockSpec((d_f + d_s, d), lambda bi, i: (0, 0)), vec(d), bvec(d), vec(d), bvec(d), bvec(d),
                  pl.BlockSpec((d, nq), lambda bi, i: (0, 0))],
        out_specs=[pl.BlockSpec((1, tm, d), row), pl.BlockSpec((1, tm, d), row), pl.BlockSpec((1, tm, nq), row)],
        out_shape=[jax.ShapeDtypeStruct((b, l, d), F32), jax.ShapeDtypeStruct((b, l, d), BF16),
                   jax.ShapeDtypeStruct((b, l, nq), F32)],
        compiler_params=_cparams(("arbitrary", "arbitrary")),
        name="outproj_prenorm_query",
    )(of, yf, yb, xbc, proj, x, dsk, nw, w_out, g_post, gm, g_pre, cf, sf, w_query)


def _top16_ranked(s, exact):
    n = s.shape[0]
    iota = lax.broadcasted_iota(jnp.int32, s.shape, 0).astype(F32) if exact else None
    cur = s
    rank = jnp.full(s.shape, float(PEER_TOPK), F32)
    vals = []
    for k in range(PEER_TOPK):
        m = jnp.max(cur, axis=0, keepdims=True)
        hit = cur == m
        if exact:
            idx = jnp.min(jnp.where(hit, iota, float(n)), axis=0, keepdims=True)
            hit = iota == idx
        rank = jnp.where(hit, float(k), rank)
        cur = jnp.where(hit, -jnp.inf, cur)
        vals.append(m)
    return vals, rank


def _route_columns(s1, s2, exact):
    kk = PEER_TOPK
    v1, rank1 = _top16_ranked(s1, exact)
    v2, rank2 = _top16_ranked(s2, exact)
    sub = lax.broadcasted_iota(jnp.int32, (8, s1.shape[1]), 0)
    subf = sub.astype(F32)

    def stack8(vals):
        out = jnp.zeros(sub.shape, F32)
        for k, v in enumerate(vals):
            out = jnp.where(sub == k, v, out)
        return out

    v2_lo, v2_hi, v1_hi = stack8(v2[:8]), stack8(v2[8:]), stack8(v1[8:])
    cands = [v1[0] + v2_lo, v1[0] + v2_hi] + [v1[a] + v2_lo for a in range(1, 8)] + [v1_hi + v2[0]]
    poses = [subf, subf + 8.0] + [a * kk + subf for a in range(1, 8)] + [(subf + 8.0) * kk]
    sels = [jnp.zeros(sub.shape, F32) for _ in cands]
    m0 = v1[0] + v2[0]
    zsum = jnp.zeros_like(m0)
    for _ in range(kk):
        m = jnp.max(functools.reduce(jnp.maximum, cands), axis=0, keepdims=True)
        hits = [cd == m for cd in cands]
        if exact:
            firsts = [jnp.where(ht, ps, float(kk * kk)) for ht, ps in zip(hits, poses)]
            p = jnp.min(functools.reduce(jnp.minimum, firsts), axis=0, keepdims=True)
            hits = [ps == p for ps in poses]
        sels = [jnp.where(ht, 1.0, sl) for ht, sl in zip(hits, sels)]
        cands = [jnp.where(ht, -jnp.inf, cd) for ht, cd in zip(hits, cands)]
        zsum = zsum + jnp.exp(m - m0)
    n_rows = [jnp.sum(sels[0] + sels[1], axis=0, keepdims=True)]
    n_rows += [jnp.sum(sels[a + 1], axis=0, keepdims=True) for a in range(1, 8)]
    n_rows += [jnp.sum(jnp.where(sub == r, sels[9], 0.0), axis=0, keepdims=True) for r in range(8)]
    cnt = jnp.zeros(rank1.shape, F32)
    for a in range(kk):
        cnt = jnp.where(rank1 == float(a), n_rows[a], cnt)
    removed = lambda rk: jnp.sum(jnp.where(rk < float(kk), 1.0, 0.0), axis=0, keepdims=True)
    ok = ((functools.reduce(jnp.add, n_rows) == float(kk)) & (removed(rank1) == float(kk))
          & (removed(rank2) == float(kk)))
    return rank2, cnt, zsum, v1[0], v2[0], ok


def _peer_route_kernel(q_ref, k1_ref, k2_ref, rank2_ref, cnt_ref, e1_ref, e2_ref, s1_ref, s2_ref):
    hp = lax.Precision.HIGHEST
    qh = q_ref[...]
    s1_ref[...] = _dot_nt(k1_ref[0], qh[:, :PEER_HALF], precision=hp)
    s2_ref[...] = _dot_nt(k2_ref[0], qh[:, PEER_HALF:], precision=hp)

    def column(ci, carry):
        cols = pl.ds(pl.multiple_of(ci * LANE, LANE), LANE)
        s1 = s1_ref[:, cols]
        s2 = s2_ref[:, cols]

        def emit(exact):
            rank2, cnt, zsum, m1, m2, ok = _route_columns(s1, s2, exact)
            rank2_ref[0, :, cols] = rank2.astype(BF16)
            cnt_ref[0, :, cols] = cnt
            e1_ref[0, :, cols] = jnp.exp(s1 - m1)
            e2_ref[0, :, cols] = (jnp.exp(s2 - m2) / zsum).astype(BF16)
            return ok

        ok = emit(False)
        n_bad = jnp.sum(jnp.where(ok, 0.0, 1.0))

        @pl.when(n_bad > 0.0)
        def _():
            emit(True)

        return carry

    lax.fori_loop(0, s1_ref.shape[1] // LANE, column, 0)


def _peer_route(q, k1, k2, tq=1024):
    t, _ = q.shape
    nh, nk, hd = k1.shape
    out = lambda dt: jax.ShapeDtypeStruct((nh, nk, t), dt)
    spec = pl.BlockSpec((1, nk, tq), lambda i, h: (h, 0, i))
    return pl.pallas_call(
        _peer_route_kernel,
        grid=(t // tq, nh),
        in_specs=[pl.BlockSpec((tq, PEER_KEY_DIM), lambda i, h: (i, h)),
                  pl.BlockSpec((1, nk, hd), lambda i, h: (h, 0, 0)),
                  pl.BlockSpec((1, nk, hd), lambda i, h: (h, 0, 0))],
        out_specs=[spec, spec, spec, spec],
        out_shape=[out(BF16), out(F32), out(F32), out(BF16)],
        scratch_shapes=[pltpu.VMEM((nk, tq), F32), pltpu.VMEM((nk, tq), F32)],
        compiler_params=_cparams(("arbitrary", "arbitrary")),
        name="peer_route",
    )(q, k1, k2)


BF16_ROWS = 16


def _gate_weights(w_ref, rank2_ref, e2_ref, cnt_ref, e1_ref):
    nk = PEER_N_KEYS
    tm = w_ref.shape[1]
    for i_loc in range(w_ref.shape[0] // nk):
        w = None
        for h in range(PEER_HEADS):
            cnt_b = jnp.broadcast_to(cnt_ref[h, i_loc:i_loc + 1, :], (BF16_ROWS, tm)).astype(BF16)
            e1_b = jnp.broadcast_to(e1_ref[h, i_loc:i_loc + 1, :], (BF16_ROWS, tm)).astype(BF16)
            wh = jnp.where(rank2_ref[h] < cnt_b[None], e2_ref[h], jnp.zeros((), BF16)) * e1_b[None]
            w = wh if w is None else w + wh
        w_ref[i_loc * nk:(i_loc + 1) * nk, :] = w.reshape(nk, tm)


def _peer_dense_kernel(n_sub, h2_ref, u_ref, v_ref, rank2_ref, e2_ref, cnt_ref, e1_ref,
                       x1_ref, gf_ref, gpost_ref, o_ref, acc_ref, w_ref, *g_refs):
    j = pl.program_id(1)

    @pl.when(j == 0)
    def _():
        acc_ref[...] = jnp.zeros_like(acc_ref)

    nk = PEER_N_KEYS
    ts = u_ref.shape[0] // n_sub
    h2 = h2_ref[...]
    _gate_weights(w_ref, rank2_ref, e2_ref, cnt_ref, e1_ref)
    ats = [_dot_nt(u_ref[s * ts:(s + 1) * ts, :], h2) for s in range(n_sub)]
    for s in range(n_sub):
        at = ats[s]
        for il in range(ts // nk):
            rows = slice(s * ts + il * nk, s * ts + (il + 1) * nk)
            a = at[il * nk:(il + 1) * nk, :]
            gelu = 0.5 * a * (1.0 + lax.erf(a * (1.0 / math.sqrt(2.0))))
            g_refs[s][il * nk:(il + 1) * nk, :] = gelu.astype(BF16) * w_ref[rows, :]
        acc_ref[...] += lax.dot_general(v_ref[s * ts:(s + 1) * ts, :], g_refs[s][...], (((0,), (0,)), ((), ())),
                                        preferred_element_type=F32)

    @pl.when(j == pl.num_programs(1) - 1)
    def _():
        y = acc_ref[...].T
        o_ref[...] = x1_ref[...] + gf_ref[0] * _rms(y, gpost_ref[...])


def _peer_dense(h2, u16, v16, rank2, cnt, e1, e2, x1, gf, g_post, seq_len, tm=512, ts=512, n_sub=2):
    t, d = h2.shape
    ne = u16.shape[0]
    nh, nk, _ = cnt.shape
    te = ts * n_sub
    i_per_step = te // nk
    n_steps = ne // te
    aux_j = pl.BlockSpec((nh, nk // BF16_ROWS, BF16_ROWS, tm), lambda i, j: (0, 0, 0, i))
    aux_i = pl.BlockSpec((nh, i_per_step, tm), lambda i, j: (0, j, i))
    blocks_per_seq = seq_len // tm
    return pl.pallas_call(
        functools.partial(_peer_dense_kernel, n_sub),
        grid=(t // tm, n_steps),
        in_specs=[pl.BlockSpec((tm, d), lambda i, j: (i, 0)),
                  pl.BlockSpec((te, d), lambda i, j: (j, 0)),
                  pl.BlockSpec((te, d), lambda i, j: (j, 0)),
                  aux_j, aux_j, aux_i, aux_i,
                  pl.BlockSpec((tm, d), lambda i, j: (i, 0), pipeline_mode=pl.Buffered(1)),
                  pl.BlockSpec((1, 1, d), lambda i, j: (i // blocks_per_seq, 0, 0)),
                  pl.BlockSpec((1, d), lambda i, j: (0, 0))],
        out_specs=pl.BlockSpec((tm, d), lambda i, j: (i, 0)),
        out_shape=jax.ShapeDtypeStruct((t, d), F32),
        scratch_shapes=[pltpu.VMEM((d, tm), F32), pltpu.VMEM((te, tm), BF16)]
        + [pltpu.VMEM((ts, tm), BF16) for _ in range(n_sub)],
        compiler_params=_cparams(("arbitrary", "arbitrary")),
        name="peer_dense",
    )(h2, u16, v16, rank2, e2, cnt, e1, x1, gf, g_post)


def _dft_tables(n):
    k = np.arange(n, dtype=np.int64)
    ph = (np.outer(k, k) % n).astype(np.float64) * (2.0 * np.pi / n)
    return np.cos(ph), np.sin(ph)


def kernel(x, c, ctx, c_ctx, w_mod, b_mod, g_pre_mix, g_post_mix, g_pre_ffn, g_post_ffn, w_in, w_fmix, conv_w, conv_b, dt_bias_f, dt_bias_b, a_log_f, a_log_b, d_skip_f, d_skip_b, ssm_norm_w, w_out, w_query, sub_keys_1, sub_keys_2, expert_u, expert_v):
    bsz, seq, d = x.shape
    ctx_len = ctx.shape[1]
    layer = 0
    n_heads = dt_bias_f.shape[1]
    d_ssm = n_heads * SSM_HEAD_DIM
    d_f = d - d_ssm
    gc = d_f // N_FOURIER_GROUPS
    d_xbc = conv_w.shape[2]
    assert w_mod.shape[0] == 1, "single-layer kernel"

    cond = jnp.zeros((8, d), F32).at[:bsz].set(c).at[bsz].set(c_ctx)
    mod = _adaln(cond, w_mod[layer], b_mod[layer])
    mods = [mod[:, i * d:(i + 1) * d] for i in range(N_MOD)]
    sm_x, cm_x, gm_x, sf_x, cf_x, gf_x = [m[:bsz, None, :] for m in mods]
    sm_c = jnp.broadcast_to(mods[0][bsz][None, None, :], (bsz, 1, d))
    cm_c = jnp.broadcast_to(mods[1][bsz][None, None, :], (bsz, 1, d))

    cos_l, sin_l = _dft_tables(seq)
    cos_c, sin_c = _dft_tables(gc)
    scale = 1.0 / math.sqrt(seq * gc)
    wfc, wfs = _fold_fourier(w_in[layer], w_fmix[layer],
                             jnp.asarray(cos_c * scale, F32), jnp.asarray(sin_c * scale, F32))
    dt_pad = 512
    w_rest = w_in[layer][:, d_f:].astype(BF16)
    w_all = jnp.concatenate(
        [wfc, wfs, w_rest, jnp.zeros((d, dt_pad - 2 * n_heads), BF16)], axis=1)
    col_z = 2 * d_f
    col_xbc = col_z + d_ssm
    col_dt = col_xbc + d_xbc
    n_all = col_dt + dt_pad
    dft = jnp.asarray(np.concatenate([cos_l, -sin_l], axis=1), F32).astype(BF16)

    pad32 = lambda f, b_: jnp.zeros((1, LANE), F32).at[0, :n_heads].set(f).at[0, n_heads:2 * n_heads].set(b_)
    bias_row = pad32(dt_bias_f[layer], dt_bias_b[layer])
    a_row = pad32(-jnp.exp(a_log_f[layer]), -jnp.exp(a_log_b[layer]))

    proj_c = _inproj(ctx, g_pre_mix[layer], sm_c, cm_c, w_all, col_xbc, n_all - col_xbc, tm=ctx_len, tn=512)
    xbc_c = _conv_silu(proj_c, 0, conv_w[layer], conv_b[layer])
    zero_state = jnp.zeros((bsz, 2, n_heads // 2, SSM_STATE, 2 * SSM_HEAD_DIM), F32)
    (fin_c,) = _ssd(proj_c, d_xbc, xbc_c, bias_row, a_row, zero_state, emit_y=False)

    proj_x = _inproj(x, g_pre_mix[layer], sm_x, cm_x, w_all, 0, n_all, tm=min(1024, seq), tn=512)
    o_f = _seqdft(dft, proj_x, d_f)
    xbc_x = _conv_silu(proj_x, col_xbc, conv_w[layer], conv_b[layer])
    y_f, y_b, _ = _ssd(proj_x, col_dt, xbc_x, bias_row, a_row, fin_c, emit_y=True)

    dsk = jnp.repeat(d_skip_f[layer] + d_skip_b[layer], SSM_HEAD_DIM)[None, :]
    x1, h2, q = _outproj(o_f, y_f, y_b, xbc_x, proj_x, col_z, x, dsk, ssm_norm_w[layer][None, :],
                         w_out[layer].astype(BF16), g_post_mix[layer][None, :], gm_x,
                         g_pre_ffn[layer][None, :], cf_x, sf_x, w_query[layer].astype(BF16))

    t = bsz * seq
    rank2, cnt, e1, e2 = _peer_route(q.reshape(t, -1), sub_keys_1[layer], sub_keys_2[layer],
                                     tq=min(1024, t))
    tiles = lambda a: a.reshape(a.shape[0], a.shape[1] // BF16_ROWS, BF16_ROWS, t)
    u16 = expert_u[layer].astype(BF16)
    v16 = expert_v[layer].astype(BF16)
    out = _peer_dense(h2.reshape(t, d), u16, v16, tiles(rank2), cnt, e1, tiles(e2), x1.reshape(t, d), gf_x,
                      g_post_ffn[layer][None, :], seq)
    return out.reshape(bsz, seq, d)
```

```python
import functools
import math

import jax
import jax.numpy as jnp
import numpy as np
from jax import lax
from jax.experimental import pallas as pl
from jax.experimental.pallas import tpu as pltpu

F32 = jnp.float32
BF16 = jnp.bfloat16

EPS = 1e-6
N_MOD = 6
N_FOURIER_GROUPS = 4
SSM_HEAD_DIM = 64
SSM_GROUPS = 2
SSM_STATE = 128
SSM_CHUNK = 128
PEER_HEADS = 8
PEER_TOPK = 16
PEER_N_KEYS = 128
PEER_KEY_DIM = 128
PEER_HALF = PEER_KEY_DIM // 2

LANE = 128
VMEM_LIMIT = 56 * 1024 * 1024


def _cparams(sem):
    return pltpu.CompilerParams(dimension_semantics=sem, vmem_limit_bytes=VMEM_LIMIT)


def _bdot(a, b):
    return jnp.dot(a.astype(BF16), b.astype(BF16), preferred_element_type=F32)


def _dot_nt(a, b, precision=None):
    return lax.dot_general(a, b, (((1,), (1,)), ((), ())), precision=precision,
                           preferred_element_type=F32)


def _rms(u, g):
    return u * lax.rsqrt(jnp.mean(u * u, axis=-1, keepdims=True) + EPS) * g


def _silu(u):
    return u * (1.0 / (1.0 + jnp.exp(-u)))


def _adaln_kernel(c_ref, w_ref, b_ref, o_ref):
    a = _silu(c_ref[...])
    o_ref[...] = _bdot(a, w_ref[...]) + b_ref[...]


def _adaln(cond, w_mod, b_mod, tn=1024):
    rows, d = cond.shape
    n = w_mod.shape[1]
    return pl.pallas_call(
        _adaln_kernel,
        grid=(n // tn,),
        in_specs=[pl.BlockSpec((rows, d), lambda j: (0, 0)),
                  pl.BlockSpec((d, tn), lambda j: (0, j)),
                  pl.BlockSpec((1, tn), lambda j: (0, j))],
        out_specs=pl.BlockSpec((rows, tn), lambda j: (0, j)),
        out_shape=jax.ShapeDtypeStruct((rows, n), F32),
        compiler_params=_cparams(("arbitrary",)),
        name="adaln",
    )(cond, w_mod, b_mod.reshape(1, n))


def _fold_kernel(w_ref, fm_ref, cc_ref, sc_ref, oc_ref, os_ref):
    hp = lax.Precision.HIGHEST
    w = w_ref[...]
    fm = fm_ref[0]
    mc = jnp.dot(cc_ref[...], fm, precision=hp, preferred_element_type=F32)
    ms = jnp.dot(sc_ref[...], fm, precision=hp, preferred_element_type=F32)
    oc_ref[...] = jnp.dot(w, mc, precision=hp, preferred_element_type=F32).astype(BF16)
    os_ref[...] = jnp.dot(w, ms, precision=hp, preferred_element_type=F32).astype(BF16)


def _fold_fourier(w_in, w_fmix, cos_c, sin_c):
    d = w_in.shape[0]
    ng, gc, _ = w_fmix.shape
    out = jax.ShapeDtypeStruct((d, ng * gc), BF16)
    return pl.pallas_call(
        _fold_kernel,
        grid=(ng,),
        in_specs=[pl.BlockSpec((d, gc), lambda g: (0, g)),
                  pl.BlockSpec((1, gc, gc), lambda g: (g, 0, 0)),
                  pl.BlockSpec((gc, gc), lambda g: (0, 0)),
                  pl.BlockSpec((gc, gc), lambda g: (0, 0))],
        out_specs=[pl.BlockSpec((d, gc), lambda g: (0, g)),
                   pl.BlockSpec((d, gc), lambda g: (0, g))],
        out_shape=[out, out],
        compiler_params=_cparams(("arbitrary",)),
        name="fourier_fold",
    )(w_in, w_fmix, cos_c, sin_c)


def _inproj_kernel(x_ref, g_ref, sh_ref, sc_ref, w_ref, o_ref, h_ref):
    @pl.when(pl.program_id(2) == 0)
    def _():
        slab = min(LANE, h_ref.shape[0])

        def body(r, carry):
            rows = pl.ds(pl.multiple_of(r * slab, slab), slab)
            h = _rms(x_ref[0, rows, :], g_ref[...]) * (1.0 + sc_ref[0]) + sh_ref[0]
            h_ref[rows, :] = h.astype(BF16)
            return carry

        lax.fori_loop(0, h_ref.shape[0] // slab, body, 0)

    o_ref[0] = jnp.dot(h_ref[...], w_ref[...], preferred_element_type=F32)


def _inproj(xin, g, shift, scale, w_all, col0, ncols, tm, tn):
    b, l, d = xin.shape
    jb = col0 // tn
    return pl.pallas_call(
        _inproj_kernel,
        grid=(b, l // tm, ncols // tn),
        in_specs=[pl.BlockSpec((1, tm, d), lambda bi, i, j: (bi, i, 0)),
                  pl.BlockSpec((1, d), lambda bi, i, j: (0, 0)),
                  pl.BlockSpec((1, 1, d), lambda bi, i, j: (bi, 0, 0)),
                  pl.BlockSpec((1, 1, d), lambda bi, i, j: (bi, 0, 0)),
                  pl.BlockSpec((d, tn), lambda bi, i, j: (0, j + jb))],
        out_specs=pl.BlockSpec((1, tm, tn), lambda bi, i, j: (bi, i, j)),
        out_shape=jax.ShapeDtypeStruct((b, l, ncols), F32),
        scratch_shapes=[pltpu.VMEM((tm, d), BF16)],
        compiler_params=_cparams(("arbitrary", "arbitrary", "arbitrary")),
        name="inproj",
    )(xin, g.reshape(1, d), shift, scale, w_all)


def _seqdft_kernel(a_ref, x_ref, o_ref, acc_ref):
    k = pl.program_id(1)

    @pl.when(k == 0)
    def _():
        acc_ref[...] = jnp.zeros_like(acc_ref)

    acc_ref[...] += jnp.dot(a_ref[...], x_ref[0].astype(BF16), preferred_element_type=F32)

    @pl.when(k == pl.num_programs(1) - 1)
    def _():
        o_ref[0] = acc_ref[...].astype(BF16)


def _seqdft(dft, proj, n_f, tk=512):
    b, l, _ = proj.shape
    kl = l // tk
    return pl.pallas_call(
        _seqdft_kernel,
        grid=(b, 2 * kl),
        in_specs=[pl.BlockSpec((l, tk), lambda bi, k: (0, k)),
                  pl.BlockSpec((1, tk, n_f), lambda bi, k: (bi, k % kl, k // kl))],
        out_specs=pl.BlockSpec((1, l, n_f), lambda bi, k: (bi, 0, 0)),
        out_shape=jax.ShapeDtypeStruct((b, l, n_f), BF16),
        scratch_shapes=[pltpu.VMEM((l, n_f), F32)],
        compiler_params=_cparams(("arbitrary", "arbitrary")),
        name="seq_dft",
    )(dft, proj)


def _conv_kernel(u_ref, w_ref, b_ref, o_ref):
    u = u_ref[0]
    l = u.shape[0]
    row = lax.broadcasted_iota(jnp.int32, u.shape, 0)
    prev = jnp.where(row == 0, 0.0, pltpu.roll(u, 1, 0))
    nxt = jnp.where(row == l - 1, 0.0, pltpu.roll(u, l - 1, 0))
    w = w_ref[...]
    o_ref[0] = _silu(prev * w[0:1] + u * w[1:2] + nxt * w[2:3] + b_ref[...])


def _conv_silu(proj, col0, conv_w, conv_b, tc=256):
    b, l, _ = proj.shape
    n = conv_w.shape[1]
    jb = col0 // tc
    return pl.pallas_call(
        _conv_kernel,
        grid=(b, n // tc),
        in_specs=[pl.BlockSpec((1, l, tc), lambda bi, j: (bi, 0, j + jb)),
                  pl.BlockSpec((3, tc), lambda bi, j: (0, j)),
                  pl.BlockSpec((1, tc), lambda bi, j: (0, j))],
        out_specs=pl.BlockSpec((1, l, tc), lambda bi, j: (bi, 0, j)),
        out_shape=jax.ShapeDtypeStruct((b, l, n), F32),
        compiler_params=_cparams(("arbitrary", "arbitrary")),
        name="conv_silu",
    )(proj, conv_w, conv_b.reshape(1, n))


def _ssd_direction(reverse, dt_raw, bias, a_neg, xs_ref, bm_ref, cm_ref, st_ref, y_ref, n_heads):
    hp = lax.Precision.HIGHEST
    t = SSM_CHUNK
    p2 = 2 * SSM_HEAD_DIM
    col0 = n_heads if reverse else 0
    r = lax.broadcasted_iota(jnp.int32, (t, t), 0)
    c = lax.broadcasted_iota(jnp.int32, (t, t), 1)
    mask = (r <= c) if reverse else (r >= c)
    tri = mask.astype(F32)
    tri_t = ((r >= c) if reverse else (r <= c)).astype(F32)
    first_y = lax.broadcasted_iota(jnp.int32, (t, p2), 1) < SSM_HEAD_DIM
    first_s = lax.broadcasted_iota(jnp.int32, (SSM_STATE, p2), 1) < SSM_HEAD_DIM

    z = dt_raw + bias
    dtv = jnp.maximum(z, 0.0) + jnp.log(1.0 + jnp.exp(-jnp.abs(z)))
    a = dtv * a_neg
    a_t = a.T
    dtv_t = dtv.T
    cs = jnp.dot(tri, a, precision=hp, preferred_element_type=F32)
    cs_t = jnp.dot(a_t, tri_t, precision=hp, preferred_element_type=F32)
    w_t = dtv_t * jnp.exp(jnp.sum(a_t, axis=1, keepdims=True) - cs_t)
    etot = jnp.exp(jnp.sum(a, axis=0, keepdims=True))

    ppg = n_heads // SSM_GROUPS // 2
    for g in range(SSM_GROUPS):
        bg = bm_ref[0, :, g * SSM_STATE:(g + 1) * SSM_STATE]
        cg16 = cm_ref[0, :, g * SSM_STATE:(g + 1) * SSM_STATE].astype(BF16)
        gmat = _dot_nt(cg16, bg.astype(BF16))
        bgt = bg.T
        if y_ref is not None:
            st_g16 = jnp.concatenate([st_ref[g * ppg + k] for k in range(ppg)], axis=1).astype(BF16)
            yoff_g = jnp.dot(cg16, st_g16, preferred_element_type=F32)
        for k in range(ppg):
            pi = g * ppg + k
            xs16 = xs_ref[0, :, pi * p2:(pi + 1) * p2].astype(BF16)
            ys, ss, ecols, etots = [], [], [], []
            for e in range(2):
                j = col0 + 2 * pi + e
                col_b = jnp.broadcast_to(cs[:, j:j + 1], (t, t))
                lmat = jnp.exp(jnp.where(mask, col_b - cs_t[j:j + 1, :], -jnp.inf))
                m16 = (gmat * lmat * dtv_t[j:j + 1, :]).astype(BF16)
                if y_ref is not None:
                    ys.append(jnp.dot(m16, xs16, preferred_element_type=F32))
                    ecols.append(jnp.exp(col_b))
                bw16 = (bgt * w_t[j:j + 1, :]).astype(BF16)
                ss.append(jnp.dot(bw16, xs16, preferred_element_type=F32))
                etots.append(etot[:, j:j + 1])
            if y_ref is not None:
                y_off = yoff_g[:, k * p2:(k + 1) * p2] * jnp.where(first_y, ecols[0], ecols[1])
                y_ref[0, :, pi * p2:(pi + 1) * p2] = jnp.where(first_y, ys[0], ys[1]) + y_off
            st_ref[pi] = (jnp.where(first_s, etots[0], etots[1]) * st_ref[pi]
                          + jnp.where(first_s, ss[0], ss[1]))


def _ssd_kernel(n_heads, emit_y, *refs):
    (dtf_ref, dtb_ref, bias_ref, a_ref, xsf_ref, xsb_ref, bmf_ref, bmb_ref, cmf_ref, cmb_ref,
     init_ref) = refs[:11]
    if emit_y:
        yf_ref, yb_ref, fin_ref, stf_ref, stb_ref = refs[11:]
    else:
        yf_ref = yb_ref = None
        fin_ref, stf_ref, stb_ref = refs[11:]
    ci = pl.program_id(1)

    @pl.when(ci == 0)
    def _():
        stf_ref[...] = init_ref[0, 0]
        stb_ref[...] = init_ref[0, 1]

    bias = bias_ref[...]
    a_neg = a_ref[...]
    _ssd_direction(False, dtf_ref[0], bias, a_neg, xsf_ref, bmf_ref, cmf_ref, stf_ref, yf_ref, n_heads)
    _ssd_direction(True, dtb_ref[0], bias, a_neg, xsb_ref, bmb_ref, cmb_ref, stb_ref, yb_ref, n_heads)

    @pl.when(ci == pl.num_programs(1) - 1)
    def _():
        fin_ref[0, 0] = stf_ref[...]
        fin_ref[0, 1] = stb_ref[...]


def _ssd(proj, dt_col0, xbc, bias_row, a_row, init, emit_y):
    b, l, _ = proj.shape
    d_ssm = xbc.shape[2] - 2 * SSM_GROUPS * SSM_STATE
    d_bc = SSM_GROUPS * SSM_STATE
    n_heads = d_ssm // SSM_HEAD_DIM
    t = SSM_CHUNK
    nc = l // t
    dtb = dt_col0 // LANE
    fwd = lambda bi, c: (bi, c, 0)
    bwd = lambda bi, c: (bi, nc - 1 - c, 0)
    off = lambda f, o: (lambda bi, c: f(bi, c)[:2] + (o,))
    st_shape = (n_heads // 2, SSM_STATE, 2 * SSM_HEAD_DIM)
    in_specs = [
        pl.BlockSpec((1, t, LANE), off(fwd, dtb)), pl.BlockSpec((1, t, LANE), off(bwd, dtb)),
        pl.BlockSpec((1, LANE), lambda bi, c: (0, 0)), pl.BlockSpec((1, LANE), lambda bi, c: (0, 0)),
        pl.BlockSpec((1, t, d_ssm), fwd), pl.BlockSpec((1, t, d_ssm), bwd),
        pl.BlockSpec((1, t, d_bc), off(fwd, d_ssm // d_bc)), pl.BlockSpec((1, t, d_bc), off(bwd, d_ssm // d_bc)),
        pl.BlockSpec((1, t, d_bc), off(fwd, d_ssm // d_bc + 1)), pl.BlockSpec((1, t, d_bc), off(bwd, d_ssm // d_bc + 1)),
        pl.BlockSpec((1, 2) + st_shape, lambda bi, c: (bi, 0, 0, 0, 0)),
    ]
    fin_spec = pl.BlockSpec((1, 2) + st_shape, lambda bi, c: (bi, 0, 0, 0, 0))
    fin_shape = jax.ShapeDtypeStruct((b, 2) + st_shape, F32)
    if emit_y:
        y_shape = jax.ShapeDtypeStruct((b, l, d_ssm), F32)
        out_specs = [pl.BlockSpec((1, t, d_ssm), fwd), pl.BlockSpec((1, t, d_ssm), bwd), fin_spec]
        out_shape = [y_shape, y_shape, fin_shape]
    else:
        out_specs = [fin_spec]
        out_shape = [fin_shape]
    return pl.pallas_call(
        functools.partial(_ssd_kernel, n_heads, emit_y),
        grid=(b, nc),
        in_specs=in_specs,
        out_specs=out_specs,
        out_shape=out_shape,
        scratch_shapes=[pltpu.VMEM(st_shape, F32), pltpu.VMEM(st_shape, F32)],
        compiler_params=_cparams(("arbitrary", "arbitrary")),
        name="ssd_scan_y" if emit_y else "ssd_scan_state",
    )(proj, proj, bias_row, a_row, xbc, xbc, xbc, xbc, xbc, xbc, init)


def _outproj_kernel(of_ref, yf_ref, yb_ref, xs_ref, z_ref, x_ref, dsk_ref, nw_ref, wo_ref, gpost_ref,
                    gm_ref, gpre_ref, cf_ref, sf_ref, wq_ref, x1_ref, h2_ref, q_ref):
    d_f = of_ref.shape[2]
    y = yf_ref[0] + yb_ref[0] + dsk_ref[...] * xs_ref[0]
    y = y * _silu(z_ref[0])
    gw = y.shape[1] // SSM_GROUPS
    nw = nw_ref[...]
    parts = [_rms(y[:, g * gw:(g + 1) * gw], nw[:, g * gw:(g + 1) * gw]).astype(BF16) for g in range(SSM_GROUPS)]
    yx = jnp.dot(of_ref[0], wo_ref[:d_f, :], preferred_element_type=F32)
    for g in range(SSM_GROUPS):
        yx = yx + jnp.dot(parts[g], wo_ref[d_f + g * gw:d_f + (g + 1) * gw, :], preferred_element_type=F32)
    x1 = x_ref[0] + gm_ref[0] * _rms(yx, gpost_ref[...])
    x1_ref[0] = x1
    h2 = (_rms(x1, gpre_ref[...]) * (1.0 + cf_ref[0]) + sf_ref[0]).astype(BF16)
    h2_ref[0] = h2
    q_ref[0] = jnp.dot(h2, wq_ref[...], preferred_element_type=F32)


def _outproj(of, yf, yb, xbc, proj, z_col0, x, dsk, nw, w_out, g_post, gm, g_pre, cf, sf, w_query, tm=256):
    b, l, d = x.shape
    d_f = of.shape[2]
    d_s = yf.shape[2]
    nq = w_query.shape[1]
    zb = z_col0 // d_s
    row = lambda bi, i: (bi, i, 0)
    vec = lambda n: pl.BlockSpec((1, n), lambda bi, i: (0, 0))
    bvec = lambda n: pl.BlockSpec((1, 1, n), lambda bi, i: (bi, 0, 0))
    return pl.pallas_call(
        _outproj_kernel,
        grid=(b, l // tm),
        in_specs=[pl.BlockSpec((1, tm, d_f), row), pl.BlockSpec((1, tm, d_s), row), pl.BlockSpec((1, tm, d_s), row),
                  pl.BlockSpec((1, tm, d_s), row), pl.BlockSpec((1, tm, d_s), lambda bi, i: (bi, i, zb)),
                  pl.BlockSpec((1, tm, d), row), vec(d_s), vec(d_s),
                  pl.BlockSpec((d_f + d_s, d), lambda bi, i: (0, 0)), vec(d), bvec(d), vec(d), bvec(d), bvec(d),
                  pl.BlockSpec((d, nq), lambda bi, i: (0, 0))],
        out_specs=[pl.BlockSpec((1, tm, d), row), pl.BlockSpec((1, tm, d), row), pl.BlockSpec((1, tm, nq), row)],
        out_shape=[jax.ShapeDtypeStruct((b, l, d), F32), jax.ShapeDtypeStruct((b, l, d), BF16),
                   jax.ShapeDtypeStruct((b, l, nq), F32)],
        compiler_params=_cparams(("arbitrary", "arbitrary")),
        name="outproj_prenorm_query",
    )(of, yf, yb, xbc, proj, x, dsk, nw, w_out, g_post, gm, g_pre, cf, sf, w_query)


SUBLANES = 8


def _top16_ranked(s):
    n = s.shape[0]
    iota = lax.broadcasted_iota(jnp.int32, s.shape, 0).astype(F32)
    cur = s
    rank = jnp.full(s.shape, float(PEER_TOPK), F32)
    vals = []
    for k in range(PEER_TOPK):
        m = jnp.max(cur, axis=0, keepdims=True)
        idx = jnp.min(jnp.where(cur == m, iota, float(n)), axis=0, keepdims=True)
        hit = iota == idx
        rank = jnp.where(hit, float(k), rank)
        cur = jnp.where(hit, -jnp.inf, cur)
        vals.append(m)
    return vals, rank


def _candidate_pieces(v1, v2, sub):
    def stack8(vals):
        out = jnp.zeros(sub.shape, F32)
        for k, v in enumerate(vals):
            out = jnp.where(sub == k, v, out)
        return out

    v2_lo, v2_hi, v1_hi = stack8(v2[:8]), stack8(v2[8:]), stack8(v1[8:])
    return [v1[0] + v2_lo, v1[0] + v2_hi] + [v1[a] + v2_lo for a in range(1, 8)] + [v1_hi + v2[0]]


def _row_counts(sels, sub):
    n_rows = [jnp.sum(sels[0] + sels[1], axis=0, keepdims=True)]
    n_rows += [jnp.sum(sels[a + 1], axis=0, keepdims=True) for a in range(1, 8)]
    n_rows += [jnp.sum(jnp.where(sub == r, sels[9], 0.0), axis=0, keepdims=True) for r in range(8)]
    return n_rows


def _route_columns_exact(s1, s2):
    kk = PEER_TOPK
    v1, rank1 = _top16_ranked(s1)
    v2, rank2 = _top16_ranked(s2)
    sub = lax.broadcasted_iota(jnp.int32, (SUBLANES, s1.shape[1]), 0)
    subf = sub.astype(F32)
    cands = _candidate_pieces(v1, v2, sub)
    poses = [subf, subf + 8.0] + [a * kk + subf for a in range(1, 8)] + [(subf + 8.0) * kk]
    sels = [jnp.zeros(sub.shape, F32) for _ in cands]
    m0 = v1[0] + v2[0]
    zsum = jnp.zeros_like(m0)
    for _ in range(kk):
        m = jnp.max(functools.reduce(jnp.maximum, cands), axis=0, keepdims=True)
        firsts = [jnp.where(cd == m, ps, float(kk * kk)) for cd, ps in zip(cands, poses)]
        p = jnp.min(functools.reduce(jnp.minimum, firsts), axis=0, keepdims=True)
        hits = [ps == p for ps in poses]
        sels = [jnp.where(ht, 1.0, sl) for ht, sl in zip(hits, sels)]
        cands = [jnp.where(ht, -jnp.inf, cd) for ht, cd in zip(hits, cands)]
        zsum = zsum + jnp.exp(m - m0)
    n_rows = _row_counts(sels, sub)
    cnt = jnp.zeros(rank1.shape, F32)
    for a in range(kk):
        cnt = jnp.where(rank1 == float(a), n_rows[a], cnt)
    return rank2, cnt, zsum, v1[0], v2[0]


def _sorted_top16(blocks):
    v = list(blocks)
    n = len(v)

    def exchange(i, l, descending):
        hi, lo = jnp.maximum(v[i], v[l]), jnp.minimum(v[i], v[l])
        v[i], v[l] = (hi, lo) if descending else (lo, hi)

    k = 2
    while k <= n:
        j = k // 2
        while j >= 1:
            for i in range(n):
                if i ^ j > i:
                    exchange(i, i ^ j, (i & k) == 0)
            j //= 2
        k *= 2
    for shift in (4, 2, 1):
        w = [pltpu.roll(x, shift, 0) for x in v]
        v = [jnp.maximum(v[r], w[n - 1 - r]) for r in range(n)]
        j = n // 2
        while j >= 1:
            for i in range(n):
                if i ^ j > i:
                    exchange(i, i ^ j, True)
            j //= 2
    return v


def _route_columns_sorted(s1, s2):
    kk = PEER_TOPK
    nb = s1.shape[0] // SUBLANES
    tcols = s1.shape[1]
    blocks = lambda s: [s[SUBLANES * r:SUBLANES * (r + 1), :] for r in range(nb)]
    b1, b2 = blocks(s1), blocks(s2)
    v1, v2 = _sorted_top16(b1), _sorted_top16(b2)
    sub = lax.broadcasted_iota(jnp.int32, (SUBLANES, tcols), 0)
    cands = _candidate_pieces(v1, v2, sub)
    neg = jnp.full((SUBLANES, tcols), -jnp.inf, F32)
    top = _sorted_top16(cands + [neg] * (nb - len(cands)))
    tau = top[kk - 1]
    m0 = v1[0] + v2[0]
    picked = [cd >= tau for cd in cands]
    sels = [jnp.where(pk, 1.0, 0.0) for pk in picked]
    zparts = [jnp.where(pk, jnp.exp(cd - m0), 0.0) for pk, cd in zip(picked, cands)]
    zsum = jnp.sum(functools.reduce(jnp.add, zparts), axis=0, keepdims=True)
    n_rows = _row_counts(sels, sub)

    rank2_blocks, cnt_blocks = [], []
    for blk1, blk2 in zip(b1, b2):
        rk = functools.reduce(jnp.add, [jnp.where(v2[k] > blk2, 1.0, 0.0) for k in range(kk)])
        ct = jnp.zeros(blk1.shape, F32)
        for a in range(kk):
            ct = jnp.where(blk1 == v1[a], n_rows[a], ct)
        rank2_blocks.append(rk)
        cnt_blocks.append(ct)
    rank2 = jnp.concatenate(rank2_blocks, axis=0)
    cnt = jnp.concatenate(cnt_blocks, axis=0)

    count = lambda flags: jnp.sum(functools.reduce(jnp.add, flags), axis=0, keepdims=True)
    strict = lambda v: functools.reduce(jnp.logical_and, [v[k] > v[k + 1] for k in range(kk - 1)])[0:1]
    ok = (strict(v1) & strict(v2) & strict(top)
          & (count([jnp.where(blk >= v1[kk - 1], 1.0, 0.0) for blk in b1]) == float(kk))
          & (count([jnp.where(blk >= v2[kk - 1], 1.0, 0.0) for blk in b2]) == float(kk))
          & (functools.reduce(jnp.add, n_rows) == float(kk)))
    return rank2, cnt, zsum, v1[0][0:1], v2[0][0:1], ok


def _peer_route_kernel(q_ref, k1_ref, k2_ref, rank2_ref, cnt_ref, e1_ref, e2_ref, s1_ref, s2_ref):
    hp = lax.Precision.HIGHEST
    qh = q_ref[...]
    s1_ref[...] = _dot_nt(k1_ref[0], qh[:, :PEER_HALF], precision=hp)
    s2_ref[...] = _dot_nt(k2_ref[0], qh[:, PEER_HALF:], precision=hp)

    def column(ci, carry):
        cols = pl.ds(pl.multiple_of(ci * LANE, LANE), LANE)
        s1 = s1_ref[:, cols]
        s2 = s2_ref[:, cols]

        def emit(rank2, cnt, zsum, m1, m2):
            rank2_ref[0, :, cols] = rank2.astype(BF16)
            cnt_ref[0, :, cols] = cnt
            e1_ref[0, :, cols] = jnp.exp(s1 - m1)
            e2_ref[0, :, cols] = (jnp.exp(s2 - m2) / zsum).astype(BF16)

        *fast, ok = _route_columns_sorted(s1, s2)
        emit(*fast)
        n_bad = jnp.sum(jnp.where(ok, 0.0, 1.0))

        @pl.when(n_bad > 0.0)
        def _():
            emit(*_route_columns_exact(s1, s2))

        return carry

    lax.fori_loop(0, s1_ref.shape[1] // LANE, column, 0)


def _peer_route(q, k1, k2, tq=1024):
    t, _ = q.shape
    nh, nk, hd = k1.shape
    out = lambda dt: jax.ShapeDtypeStruct((nh, nk, t), dt)
    spec = pl.BlockSpec((1, nk, tq), lambda i, h: (h, 0, i))
    return pl.pallas_call(
        _peer_route_kernel,
        grid=(t // tq, nh),
        in_specs=[pl.BlockSpec((tq, PEER_KEY_DIM), lambda i, h: (i, h)),
                  pl.BlockSpec((1, nk, hd), lambda i, h: (h, 0, 0)),
                  pl.BlockSpec((1, nk, hd), lambda i, h: (h, 0, 0))],
        out_specs=[spec, spec, spec, spec],
        out_shape=[out(BF16), out(F32), out(F32), out(BF16)],
        scratch_shapes=[pltpu.VMEM((nk, tq), F32), pltpu.VMEM((nk, tq), F32)],
        compiler_params=_cparams(("arbitrary", "arbitrary")),
        name="peer_route",
    )(q, k1, k2)


BF16_ROWS = 16


def _gate_weights(w_ref, rank2_ref, e2_ref, cnt_ref, e1_ref):
    nk = PEER_N_KEYS
    tm = w_ref.shape[1]
    for i_loc in range(w_ref.shape[0] // nk):
        w = None
        for h in range(PEER_HEADS):
            cnt_b = jnp.broadcast_to(cnt_ref[h, i_loc:i_loc + 1, :], (BF16_ROWS, tm)).astype(BF16)
            e1_b = jnp.broadcast_to(e1_ref[h, i_loc:i_loc + 1, :], (BF16_ROWS, tm)).astype(BF16)
            wh = jnp.where(rank2_ref[h] < cnt_b[None], e2_ref[h], jnp.zeros((), BF16)) * e1_b[None]
            w = wh if w is None else w + wh
        w_ref[i_loc * nk:(i_loc + 1) * nk, :] = w.reshape(nk, tm)


def _peer_dense_kernel(n_sub, h2_ref, u_ref, v_ref, rank2_ref, e2_ref, cnt_ref, e1_ref,
                       x1_ref, gf_ref, gpost_ref, o_ref, acc_ref, h2t_ref, w_ref, *g_refs):
    j = pl.program_id(1)

    @pl.when(j == 0)
    def _():
        acc_ref[...] = jnp.zeros_like(acc_ref)
        h2t_ref[...] = h2_ref[...].T

    nk = PEER_N_KEYS
    ts = u_ref.shape[0] // n_sub
    ats = [jnp.dot(u_ref[s * ts:(s + 1) * ts, :], h2t_ref[...], preferred_element_type=F32)
           for s in range(n_sub)]
    _gate_weights(w_ref, rank2_ref, e2_ref, cnt_ref, e1_ref)
    for s in range(n_sub):
        at = ats[s]
        for il in range(ts // nk):
            rows = slice(s * ts + il * nk, s * ts + (il + 1) * nk)
            a = at[il * nk:(il + 1) * nk, :]
            gelu = 0.5 * a * (1.0 + lax.erf(a * (1.0 / math.sqrt(2.0))))
            g_refs[s][il * nk:(il + 1) * nk, :] = gelu.astype(BF16) * w_ref[rows, :]
        acc_ref[...] += lax.dot_general(v_ref[s * ts:(s + 1) * ts, :], g_refs[s][...], (((0,), (0,)), ((), ())),
                                        preferred_element_type=F32)

    @pl.when(j == pl.num_programs(1) - 1)
    def _():
        y = acc_ref[...].T
        o_ref[...] = x1_ref[...] + gf_ref[0] * _rms(y, gpost_ref[...])


def _peer_dense(h2, u16, v16, rank2, cnt, e1, e2, x1, gf, g_post, seq_len, tm=512, ts=512, n_sub=2):
    t, d = h2.shape
    ne = u16.shape[0]
    nh, nk, _ = cnt.shape
    te = ts * n_sub
    i_per_step = te // nk
    n_steps = ne // te
    aux_j = pl.BlockSpec((nh, nk // BF16_ROWS, BF16_ROWS, tm), lambda i, j: (0, 0, 0, i))
    aux_i = pl.BlockSpec((nh, i_per_step, tm), lambda i, j: (0, j, i))
    blocks_per_seq = seq_len // tm
    return pl.pallas_call(
        functools.partial(_peer_dense_kernel, n_sub),
        grid=(t // tm, n_steps),
        in_specs=[pl.BlockSpec((tm, d), lambda i, j: (i, 0)),
                  pl.BlockSpec((te, d), lambda i, j: (j, 0)),
                  pl.BlockSpec((te, d), lambda i, j: (j, 0)),
                  aux_j, aux_j, aux_i, aux_i,
                  pl.BlockSpec((tm, d), lambda i, j: (i, 0), pipeline_mode=pl.Buffered(1)),
                  pl.BlockSpec((1, 1, d), lambda i, j: (i // blocks_per_seq, 0, 0)),
                  pl.BlockSpec((1, d), lambda i, j: (0, 0))],
        out_specs=pl.BlockSpec((tm, d), lambda i, j: (i, 0)),
        out_shape=jax.ShapeDtypeStruct((t, d), F32),
        scratch_shapes=[pltpu.VMEM((d, tm), F32), pltpu.VMEM((d, tm), BF16), pltpu.VMEM((te, tm), BF16)]
        + [pltpu.VMEM((ts, tm), BF16) for _ in range(n_sub)],
        compiler_params=_cparams(("arbitrary", "arbitrary")),
        name="peer_dense",
    )(h2, u16, v16, rank2, e2, cnt, e1, x1, gf, g_post)


def _dft_tables(n):
    k = np.arange(n, dtype=np.int64)
    ph = (np.outer(k, k) % n).astype(np.float64) * (2.0 * np.pi / n)
    return np.cos(ph), np.sin(ph)


def kernel(x, c, ctx, c_ctx, w_mod, b_mod, g_pre_mix, g_post_mix, g_pre_ffn, g_post_ffn, w_in, w_fmix, conv_w, conv_b, dt_bias_f, dt_bias_b, a_log_f, a_log_b, d_skip_f, d_skip_b, ssm_norm_w, w_out, w_query, sub_keys_1, sub_keys_2, expert_u, expert_v):
    bsz, seq, d = x.shape
    ctx_len = ctx.shape[1]
    layer = 0
    n_heads = dt_bias_f.shape[1]
    d_ssm = n_heads * SSM_HEAD_DIM
    d_f = d - d_ssm
    gc = d_f // N_FOURIER_GROUPS
    d_xbc = conv_w.shape[2]
    assert w_mod.shape[0] == 1, "single-layer kernel"

    cond = jnp.zeros((8, d), F32).at[:bsz].set(c).at[bsz].set(c_ctx)
    mod = _adaln(cond, w_mod[layer], b_mod[layer])
    mods = [mod[:, i * d:(i + 1) * d] for i in range(N_MOD)]
    sm_x, cm_x, gm_x, sf_x, cf_x, gf_x = [m[:bsz, None, :] for m in mods]
    sm_c = jnp.broadcast_to(mods[0][bsz][None, None, :], (bsz, 1, d))
    cm_c = jnp.broadcast_to(mods[1][bsz][None, None, :], (bsz, 1, d))

    cos_l, sin_l = _dft_tables(seq)
    cos_c, sin_c = _dft_tables(gc)
    scale = 1.0 / math.sqrt(seq * gc)
    wfc, wfs = _fold_fourier(w_in[layer], w_fmix[layer],
                             jnp.asarray(cos_c * scale, F32), jnp.asarray(sin_c * scale, F32))
    dt_pad = 512
    w_rest = w_in[layer][:, d_f:].astype(BF16)
    w_all = jnp.concatenate(
        [wfc, wfs, w_rest, jnp.zeros((d, dt_pad - 2 * n_heads), BF16)], axis=1)
    col_z = 2 * d_f
    col_xbc = col_z + d_ssm
    col_dt = col_xbc + d_xbc
    n_all = col_dt + dt_pad
    dft = jnp.asarray(np.concatenate([cos_l, -sin_l], axis=1), F32).astype(BF16)

    pad32 = lambda f, b_: jnp.zeros((1, LANE), F32).at[0, :n_heads].set(f).at[0, n_heads:2 * n_heads].set(b_)
    bias_row = pad32(dt_bias_f[layer], dt_bias_b[layer])
    a_row = pad32(-jnp.exp(a_log_f[layer]), -jnp.exp(a_log_b[layer]))

    proj_c = _inproj(ctx, g_pre_mix[layer], sm_c, cm_c, w_all, col_xbc, n_all - col_xbc, tm=ctx_len, tn=512)
    xbc_c = _conv_silu(proj_c, 0, conv_w[layer], conv_b[layer])
    zero_state = jnp.zeros((bsz, 2, n_heads // 2, SSM_STATE, 2 * SSM_HEAD_DIM), F32)
    (fin_c,) = _ssd(proj_c, d_xbc, xbc_c, bias_row, a_row, zero_state, emit_y=False)

    proj_x = _inproj(x, g_pre_mix[layer], sm_x, cm_x, w_all, 0, n_all, tm=min(1024, seq), tn=512)
    o_f = _seqdft(dft, proj_x, d_f)
    xbc_x = _conv_silu(proj_x, col_xbc, conv_w[layer], conv_b[layer])
    y_f, y_b, _ = _ssd(proj_x, col_dt, xbc_x, bias_row, a_row, fin_c, emit_y=True)

    dsk = jnp.repeat(d_skip_f[layer] + d_skip_b[layer], SSM_HEAD_DIM)[None, :]
    x1, h2, q = _outproj(o_f, y_f, y_b, xbc_x, proj_x, col_z, x, dsk, ssm_norm_w[layer][None, :],
                         w_out[layer].astype(BF16), g_post_mix[layer][None, :], gm_x,
                         g_pre_ffn[layer][None, :], cf_x, sf_x, w_query[layer].astype(BF16))

    t = bsz * seq
    rank2, cnt, e1, e2 = _peer_route(q.reshape(t, -1), sub_keys_1[layer], sub_keys_2[layer],
                                     tq=min(1024, t))
    tiles = lambda a: a.reshape(a.shape[0], a.shape[1] // BF16_ROWS, BF16_ROWS, t)
    u16 = expert_u[layer].astype(BF16)
    v16 = expert_v[layer].astype(BF16)
    out = _peer_dense(h2.reshape(t, d), u16, v16, tiles(rank2), cnt, e1, tiles(e2), x1.reshape(t, d), gf_x,
                      g_post_ffn[layer][None, :], seq)
    return out.reshape(bsz, seq, d)
```

```python
import functools
import math

import jax
import jax.numpy as jnp
import numpy as np
from jax import lax
from jax.experimental import pallas as pl
from jax.experimental.pallas import tpu as pltpu

F32 = jnp.float32
BF16 = jnp.bfloat16

EPS = 1e-6
N_MOD = 6
N_FOURIER_GROUPS = 4
SSM_HEAD_DIM = 64
SSM_GROUPS = 2
SSM_STATE = 128
SSM_CHUNK = 128
PEER_HEADS = 8
PEER_TOPK = 16
PEER_N_KEYS = 128
PEER_KEY_DIM = 128
PEER_HALF = PEER_KEY_DIM // 2

LANE = 128
VMEM_LIMIT = 56 * 1024 * 1024


def _cparams(sem):
    return pltpu.CompilerParams(dimension_semantics=sem, vmem_limit_bytes=VMEM_LIMIT)


def _bdot(a, b):
    return jnp.dot(a.astype(BF16), b.astype(BF16), preferred_element_type=F32)


def _dot_nt(a, b, precision=None):
    return lax.dot_general(a, b, (((1,), (1,)), ((), ())), precision=precision,
                           preferred_element_type=F32)


def _rms(u, g):
    return u * lax.rsqrt(jnp.mean(u * u, axis=-1, keepdims=True) + EPS) * g


def _silu(u):
    return u * (1.0 / (1.0 + jnp.exp(-u)))


def _adaln_kernel(c_ref, w_ref, b_ref, o_ref):
    a = _silu(c_ref[...])
    o_ref[...] = _bdot(a, w_ref[...]) + b_ref[...]


def _adaln(cond, w_mod, b_mod, tn=1024):
    rows, d = cond.shape
    n = w_mod.shape[1]
    return pl.pallas_call(
        _adaln_kernel,
        grid=(n // tn,),
        in_specs=[pl.BlockSpec((rows, d), lambda j: (0, 0)),
                  pl.BlockSpec((d, tn), lambda j: (0, j)),
                  pl.BlockSpec((1, tn), lambda j: (0, j))],
        out_specs=pl.BlockSpec((rows, tn), lambda j: (0, j)),
        out_shape=jax.ShapeDtypeStruct((rows, n), F32),
        compiler_params=_cparams(("arbitrary",)),
        name="adaln",
    )(cond, w_mod, b_mod.reshape(1, n))


def _fold_kernel(w_ref, fm_ref, cc_ref, sc_ref, oc_ref, os_ref):
    hp = lax.Precision.HIGHEST
    w = w_ref[...]
    fm = fm_ref[0]
    mc = jnp.dot(cc_ref[...], fm, precision=hp, preferred_element_type=F32)
    ms = jnp.dot(sc_ref[...], fm, precision=hp, preferred_element_type=F32)
    oc_ref[...] = jnp.dot(w, mc, precision=hp, preferred_element_type=F32).astype(BF16)
    os_ref[...] = jnp.dot(w, ms, precision=hp, preferred_element_type=F32).astype(BF16)


def _fold_fourier(w_in, w_fmix, cos_c, sin_c):
    d = w_in.shape[0]
    ng, gc, _ = w_fmix.shape
    out = jax.ShapeDtypeStruct((d, ng * gc), BF16)
    return pl.pallas_call(
        _fold_kernel,
        grid=(ng,),
        in_specs=[pl.BlockSpec((d, gc), lambda g: (0, g)),
                  pl.BlockSpec((1, gc, gc), lambda g: (g, 0, 0)),
                  pl.BlockSpec((gc, gc), lambda g: (0, 0)),
                  pl.BlockSpec((gc, gc), lambda g: (0, 0))],
        out_specs=[pl.BlockSpec((d, gc), lambda g: (0, g)),
                   pl.BlockSpec((d, gc), lambda g: (0, g))],
        out_shape=[out, out],
        compiler_params=_cparams(("arbitrary",)),
        name="fourier_fold",
    )(w_in, w_fmix, cos_c, sin_c)


def _inproj_kernel(x_ref, g_ref, sh_ref, sc_ref, w_ref, o_ref, h_ref):
    @pl.when(pl.program_id(2) == 0)
    def _():
        slab = min(LANE, h_ref.shape[0])

        def body(r, carry):
            rows = pl.ds(pl.multiple_of(r * slab, slab), slab)
            h = _rms(x_ref[0, rows, :], g_ref[...]) * (1.0 + sc_ref[0]) + sh_ref[0]
            h_ref[rows, :] = h.astype(BF16)
            return carry

        lax.fori_loop(0, h_ref.shape[0] // slab, body, 0)

    o_ref[0] = jnp.dot(h_ref[...], w_ref[...], preferred_element_type=F32)


def _inproj(xin, g, shift, scale, w_all, col0, ncols, tm, tn):
    b, l, d = xin.shape
    jb = col0 // tn
    return pl.pallas_call(
        _inproj_kernel,
        grid=(b, l // tm, ncols // tn),
        in_specs=[pl.BlockSpec((1, tm, d), lambda bi, i, j: (bi, i, 0)),
                  pl.BlockSpec((1, d), lambda bi, i, j: (0, 0)),
                  pl.BlockSpec((1, 1, d), lambda bi, i, j: (bi, 0, 0)),
                  pl.BlockSpec((1, 1, d), lambda bi, i, j: (bi, 0, 0)),
                  pl.BlockSpec((d, tn), lambda bi, i, j: (0, j + jb))],
        out_specs=pl.BlockSpec((1, tm, tn), lambda bi, i, j: (bi, i, j)),
        out_shape=jax.ShapeDtypeStruct((b, l, ncols), F32),
        scratch_shapes=[pltpu.VMEM((tm, d), BF16)],
        compiler_params=_cparams(("arbitrary", "arbitrary", "arbitrary")),
        name="inproj",
    )(xin, g.reshape(1, d), shift, scale, w_all)


def _seqdft_kernel(a_ref, x_ref, o_ref, acc_ref):
    k = pl.program_id(1)

    @pl.when(k == 0)
    def _():
        acc_ref[...] = jnp.zeros_like(acc_ref)

    acc_ref[...] += jnp.dot(a_ref[...], x_ref[0].astype(BF16), preferred_element_type=F32)

    @pl.when(k == pl.num_programs(1) - 1)
    def _():
        o_ref[0] = acc_ref[...].astype(BF16)


def _seqdft(dft, proj, n_f, tk=512):
    b, l, _ = proj.shape
    kl = l // tk
    return pl.pallas_call(
        _seqdft_kernel,
        grid=(b, 2 * kl),
        in_specs=[pl.BlockSpec((l, tk), lambda bi, k: (0, k)),
                  pl.BlockSpec((1, tk, n_f), lambda bi, k: (bi, k % kl, k // kl))],
        out_specs=pl.BlockSpec((1, l, n_f), lambda bi, k: (bi, 0, 0)),
        out_shape=jax.ShapeDtypeStruct((b, l, n_f), BF16),
        scratch_shapes=[pltpu.VMEM((l, n_f), F32)],
        compiler_params=_cparams(("arbitrary", "arbitrary")),
        name="seq_dft",
    )(dft, proj)


def _conv_kernel(u_ref, w_ref, b_ref, o_ref):
    u = u_ref[0]
    l = u.shape[0]
    row = lax.broadcasted_iota(jnp.int32, u.shape, 0)
    prev = jnp.where(row == 0, 0.0, pltpu.roll(u, 1, 0))
    nxt = jnp.where(row == l - 1, 0.0, pltpu.roll(u, l - 1, 0))
    w = w_ref[...]
    o_ref[0] = _silu(prev * w[0:1] + u * w[1:2] + nxt * w[2:3] + b_ref[...])


def _conv_silu(proj, col0, conv_w, conv_b, tc=256):
    b, l, _ = proj.shape
    n = conv_w.shape[1]
    jb = col0 // tc
    return pl.pallas_call(
        _conv_kernel,
        grid=(b, n // tc),
        in_specs=[pl.BlockSpec((1, l, tc), lambda bi, j: (bi, 0, j + jb)),
                  pl.BlockSpec((3, tc), lambda bi, j: (0, j)),
                  pl.BlockSpec((1, tc), lambda bi, j: (0, j))],
        out_specs=pl.BlockSpec((1, l, tc), lambda bi, j: (bi, 0, j)),
        out_shape=jax.ShapeDtypeStruct((b, l, n), F32),
        compiler_params=_cparams(("arbitrary", "arbitrary")),
        name="conv_silu",
    )(proj, conv_w, conv_b.reshape(1, n))


def _ssd_direction(reverse, dt_raw, bias, a_neg, xs_ref, bm_ref, cm_ref, st_ref, y_ref, n_heads):
    hp = lax.Precision.HIGHEST
    t = SSM_CHUNK
    p2 = 2 * SSM_HEAD_DIM
    col0 = n_heads if reverse else 0
    r = lax.broadcasted_iota(jnp.int32, (t, t), 0)
    c = lax.broadcasted_iota(jnp.int32, (t, t), 1)
    mask = (r <= c) if reverse else (r >= c)
    tri = mask.astype(F32)
    tri_t = ((r >= c) if reverse else (r <= c)).astype(F32)
    first_y = lax.broadcasted_iota(jnp.int32, (t, p2), 1) < SSM_HEAD_DIM
    first_s = lax.broadcasted_iota(jnp.int32, (SSM_STATE, p2), 1) < SSM_HEAD_DIM

    z = dt_raw + bias
    dtv = jnp.maximum(z, 0.0) + jnp.log(1.0 + jnp.exp(-jnp.abs(z)))
    a = dtv * a_neg
    a_t = a.T
    dtv_t = dtv.T
    cs = jnp.dot(tri, a, precision=hp, preferred_element_type=F32)
    cs_t = jnp.dot(a_t, tri_t, precision=hp, preferred_element_type=F32)
    w_t = dtv_t * jnp.exp(jnp.sum(a_t, axis=1, keepdims=True) - cs_t)
    etot = jnp.exp(jnp.sum(a, axis=0, keepdims=True))

    ppg = n_heads // SSM_GROUPS // 2
    for g in range(SSM_GROUPS):
        bg = bm_ref[0, :, g * SSM_STATE:(g + 1) * SSM_STATE]
        cg16 = cm_ref[0, :, g * SSM_STATE:(g + 1) * SSM_STATE].astype(BF16)
        gmat = _dot_nt(cg16, bg.astype(BF16))
        bgt = bg.T
        if y_ref is not None:
            st_g16 = jnp.concatenate([st_ref[g * ppg + k] for k in range(ppg)], axis=1).astype(BF16)
            yoff_g = jnp.dot(cg16, st_g16, preferred_element_type=F32)
        for k in range(ppg):
            pi = g * ppg + k
            xs16 = xs_ref[0, :, pi * p2:(pi + 1) * p2].astype(BF16)
            ys, ss, ecols, etots = [], [], [], []
            for e in range(2):
                j = col0 + 2 * pi + e
                col_b = jnp.broadcast_to(cs[:, j:j + 1], (t, t))
                lmat = jnp.exp(jnp.where(mask, col_b - cs_t[j:j + 1, :], -jnp.inf))
                m16 = (gmat * lmat * dtv_t[j:j + 1, :]).astype(BF16)
                if y_ref is not None:
                    ys.append(jnp.dot(m16, xs16, preferred_element_type=F32))
                    ecols.append(jnp.exp(col_b))
                bw16 = (bgt * w_t[j:j + 1, :]).astype(BF16)
                ss.append(jnp.dot(bw16, xs16, preferred_element_type=F32))
                etots.append(etot[:, j:j + 1])
            if y_ref is not None:
                y_off = yoff_g[:, k * p2:(k + 1) * p2] * jnp.where(first_y, ecols[0], ecols[1])
                y_ref[0, :, pi * p2:(pi + 1) * p2] = jnp.where(first_y, ys[0], ys[1]) + y_off
            st_ref[pi] = (jnp.where(first_s, etots[0], etots[1]) * st_ref[pi]
                          + jnp.where(first_s, ss[0], ss[1]))


def _ssd_kernel(n_heads, emit_y, *refs):
    (dtf_ref, dtb_ref, bias_ref, a_ref, xsf_ref, xsb_ref, bmf_ref, bmb_ref, cmf_ref, cmb_ref,
     init_ref) = refs[:11]
    if emit_y:
        yf_ref, yb_ref, fin_ref, stf_ref, stb_ref = refs[11:]
    else:
        yf_ref = yb_ref = None
        fin_ref, stf_ref, stb_ref = refs[11:]
    ci = pl.program_id(1)

    @pl.when(ci == 0)
    def _():
        stf_ref[...] = init_ref[0, 0]
        stb_ref[...] = init_ref[0, 1]

    bias = bias_ref[...]
    a_neg = a_ref[...]
    _ssd_direction(False, dtf_ref[0], bias, a_neg, xsf_ref, bmf_ref, cmf_ref, stf_ref, yf_ref, n_heads)
    _ssd_direction(True, dtb_ref[0], bias, a_neg, xsb_ref, bmb_ref, cmb_ref, stb_ref, yb_ref, n_heads)

    @pl.when(ci == pl.num_programs(1) - 1)
    def _():
        fin_ref[0, 0] = stf_ref[...]
        fin_ref[0, 1] = stb_ref[...]


def _ssd(proj, dt_col0, xbc, bias_row, a_row, init, emit_y):
    b, l, _ = proj.shape
    d_ssm = xbc.shape[2] - 2 * SSM_GROUPS * SSM_STATE
    d_bc = SSM_GROUPS * SSM_STATE
    n_heads = d_ssm // SSM_HEAD_DIM
    t = SSM_CHUNK
    nc = l // t
    dtb = dt_col0 // LANE
    fwd = lambda bi, c: (bi, c, 0)
    bwd = lambda bi, c: (bi, nc - 1 - c, 0)
    off = lambda f, o: (lambda bi, c: f(bi, c)[:2] + (o,))
    st_shape = (n_heads // 2, SSM_STATE, 2 * SSM_HEAD_DIM)
    in_specs = [
        pl.BlockSpec((1, t, LANE), off(fwd, dtb)), pl.BlockSpec((1, t, LANE), off(bwd, dtb)),
        pl.BlockSpec((1, LANE), lambda bi, c: (0, 0)), pl.BlockSpec((1, LANE), lambda bi, c: (0, 0)),
        pl.BlockSpec((1, t, d_ssm), fwd), pl.BlockSpec((1, t, d_ssm), bwd),
        pl.BlockSpec((1, t, d_bc), off(fwd, d_ssm // d_bc)), pl.BlockSpec((1, t, d_bc), off(bwd, d_ssm // d_bc)),
        pl.BlockSpec((1, t, d_bc), off(fwd, d_ssm // d_bc + 1)), pl.BlockSpec((1, t, d_bc), off(bwd, d_ssm // d_bc + 1)),
        pl.BlockSpec((1, 2) + st_shape, lambda bi, c: (bi, 0, 0, 0, 0)),
    ]
    fin_spec = pl.BlockSpec((1, 2) + st_shape, lambda bi, c: (bi, 0, 0, 0, 0))
    fin_shape = jax.ShapeDtypeStruct((b, 2) + st_shape, F32)
    if emit_y:
        y_shape = jax.ShapeDtypeStruct((b, l, d_ssm), F32)
        out_specs = [pl.BlockSpec((1, t, d_ssm), fwd), pl.BlockSpec((1, t, d_ssm), bwd), fin_spec]
        out_shape = [y_shape, y_shape, fin_shape]
    else:
        out_specs = [fin_spec]
        out_shape = [fin_shape]
    return pl.pallas_call(
        functools.partial(_ssd_kernel, n_heads, emit_y),
        grid=(b, nc),
        in_specs=in_specs,
        out_specs=out_specs,
        out_shape=out_shape,
        scratch_shapes=[pltpu.VMEM(st_shape, F32), pltpu.VMEM(st_shape, F32)],
        compiler_params=_cparams(("arbitrary", "arbitrary")),
        name="ssd_scan_y" if emit_y else "ssd_scan_state",
    )(proj, proj, bias_row, a_row, xbc, xbc, xbc, xbc, xbc, xbc, init)


def _outproj_kernel(of_ref, yf_ref, yb_ref, xs_ref, z_ref, x_ref, dsk_ref, nw_ref, wo_ref, gpost_ref,
                    gm_ref, gpre_ref, cf_ref, sf_ref, wq_ref, x1_ref, h2_ref, q_ref):
    d_f = of_ref.shape[2]
    y = yf_ref[0] + yb_ref[0] + dsk_ref[...] * xs_ref[0]
    y = y * _silu(z_ref[0])
    gw = y.shape[1] // SSM_GROUPS
    nw = nw_ref[...]
    parts = [_rms(y[:, g * gw:(g + 1) * gw], nw[:, g * gw:(g + 1) * gw]).astype(BF16) for g in range(SSM_GROUPS)]
    yx = jnp.dot(of_ref[0], wo_ref[:d_f, :], preferred_element_type=F32)
    for g in range(SSM_GROUPS):
        yx = yx + jnp.dot(parts[g], wo_ref[d_f + g * gw:d_f + (g + 1) * gw, :], preferred_element_type=F32)
    x1 = x_ref[0] + gm_ref[0] * _rms(yx, gpost_ref[...])
    x1_ref[0] = x1
    h2 = (_rms(x1, gpre_ref[...]) * (1.0 + cf_ref[0]) + sf_ref[0]).astype(BF16)
    h2_ref[0] = h2
    q_ref[0] = jnp.dot(h2, wq_ref[...], preferred_element_type=F32)


def _outproj(of, yf, yb, xbc, proj, z_col0, x, dsk, nw, w_out, g_post, gm, g_pre, cf, sf, w_query, tm=256):
    b, l, d = x.shape
    d_f = of.shape[2]
    d_s = yf.shape[2]
    nq = w_query.shape[1]
    zb = z_col0 // d_s
    row = lambda bi, i: (bi, i, 0)
    vec = lambda n: pl.BlockSpec((1, n), lambda bi, i: (0, 0))
    bvec = lambda n: pl.BlockSpec((1, 1, n), lambda bi, i: (bi, 0, 0))
    return pl.pallas_call(
        _outproj_kernel,
        grid=(b, l // tm),
        in_specs=[pl.BlockSpec((1, tm, d_f), row), pl.BlockSpec((1, tm, d_s), row), pl.BlockSpec((1, tm, d_s), row),
                  pl.BlockSpec((1, tm, d_s), row), pl.BlockSpec((1, tm, d_s), lambda bi, i: (bi, i, zb)),
                  pl.BlockSpec((1, tm, d), row), vec(d_s), vec(d_s),
                  pl.BlockSpec((d_f + d_s, d), lambda bi, i: (0, 0)), vec(d), bvec(d), vec(d), bvec(d), bvec(d),
                  pl.BlockSpec((d, nq), lambda bi, i: (0, 0))],
        out_specs=[pl.BlockSpec((1, tm, d), row), pl.BlockSpec((1, tm, d), row), pl.BlockSpec((1, tm, nq), row)],
        out_shape=[jax.ShapeDtypeStruct((b, l, d), F32), jax.ShapeDtypeStruct((b, l, d), BF16),
                   jax.ShapeDtypeStruct((b, l, nq), F32)],
        compiler_params=_cparams(("arbitrary", "arbitrary")),
        name="outproj_prenorm_query",
    )(of, yf, yb, xbc, proj, x, dsk, nw, w_out, g_post, gm, g_pre, cf, sf, w_query)


SUBLANES = 8


def _top16_ranked(s):
    n = s.shape[0]
    iota = lax.broadcasted_iota(jnp.int32, s.shape, 0).astype(F32)
    cur = s
    rank = jnp.full(s.shape, float(PEER_TOPK), F32)
    vals = []
    for k in range(PEER_TOPK):
        m = jnp.max(cur, axis=0, keepdims=True)
        idx = jnp.min(jnp.where(cur == m, iota, float(n)), axis=0, keepdims=True)
        hit = iota == idx
        rank = jnp.where(hit, float(k), rank)
        cur = jnp.where(hit, -jnp.inf, cur)
        vals.append(m)
    return vals, rank


def _candidate_pieces(v1, v2, sub):
    def stack8(vals):
        out = jnp.zeros(sub.shape, F32)
        for k, v in enumerate(vals):
            out = jnp.where(sub == k, v, out)
        return out

    v2_lo, v2_hi, v1_hi = stack8(v2[:8]), stack8(v2[8:]), stack8(v1[8:])
    return [v1[0] + v2_lo, v1[0] + v2_hi] + [v1[a] + v2_lo for a in range(1, 8)] + [v1_hi + v2[0]]


def _row_counts(sels, sub):
    n_rows = [jnp.sum(sels[0] + sels[1], axis=0, keepdims=True)]
    n_rows += [jnp.sum(sels[a + 1], axis=0, keepdims=True) for a in range(1, 8)]
    n_rows += [jnp.sum(jnp.where(sub == r, sels[9], 0.0), axis=0, keepdims=True) for r in range(8)]
    return n_rows


def _route_columns_exact(s1, s2):
    kk = PEER_TOPK
    v1, rank1 = _top16_ranked(s1)
    v2, rank2 = _top16_ranked(s2)
    sub = lax.broadcasted_iota(jnp.int32, (SUBLANES, s1.shape[1]), 0)
    subf = sub.astype(F32)
    cands = _candidate_pieces(v1, v2, sub)
    poses = [subf, subf + 8.0] + [a * kk + subf for a in range(1, 8)] + [(subf + 8.0) * kk]
    sels = [jnp.zeros(sub.shape, F32) for _ in cands]
    m0 = v1[0] + v2[0]
    zsum = jnp.zeros_like(m0)
    for _ in range(kk):
        m = jnp.max(functools.reduce(jnp.maximum, cands), axis=0, keepdims=True)
        firsts = [jnp.where(cd == m, ps, float(kk * kk)) for cd, ps in zip(cands, poses)]
        p = jnp.min(functools.reduce(jnp.minimum, firsts), axis=0, keepdims=True)
        hits = [ps == p for ps in poses]
        sels = [jnp.where(ht, 1.0, sl) for ht, sl in zip(hits, sels)]
        cands = [jnp.where(ht, -jnp.inf, cd) for ht, cd in zip(hits, cands)]
        zsum = zsum + jnp.exp(m - m0)
    n_rows = _row_counts(sels, sub)
    cnt = jnp.zeros(rank1.shape, F32)
    for a in range(kk):
        cnt = jnp.where(rank1 == float(a), n_rows[a], cnt)
    return rank2, cnt, zsum, v1[0], v2[0]


def _sorted_top16(blocks):
    v = list(blocks)
    n = len(v)

    def exchange(i, l, descending):
        hi, lo = jnp.maximum(v[i], v[l]), jnp.minimum(v[i], v[l])
        v[i], v[l] = (hi, lo) if descending else (lo, hi)

    k = 2
    while k <= n:
        j = k // 2
        while j >= 1:
            for i in range(n):
                if i ^ j > i:
                    exchange(i, i ^ j, (i & k) == 0)
            j //= 2
        k *= 2
    for shift in (4, 2, 1):
        w = [pltpu.roll(x, shift, 0) for x in v]
        v = [jnp.maximum(v[r], w[n - 1 - r]) for r in range(n)]
        j = n // 2
        while j >= 1:
            for i in range(n):
                if i ^ j > i:
                    exchange(i, i ^ j, True)
            j //= 2
    return v


def _route_columns_sorted(s1, s2):
    kk = PEER_TOPK
    nb = s1.shape[0] // SUBLANES
    tcols = s1.shape[1]
    blocks = lambda s: [s[SUBLANES * r:SUBLANES * (r + 1), :] for r in range(nb)]
    b1, b2 = blocks(s1), blocks(s2)
    v1, v2 = _sorted_top16(b1), _sorted_top16(b2)
    sub = lax.broadcasted_iota(jnp.int32, (SUBLANES, tcols), 0)
    cands = _candidate_pieces(v1, v2, sub)
    neg = jnp.full((SUBLANES, tcols), -jnp.inf, F32)
    top = _sorted_top16(cands + [neg] * (nb - len(cands)))
    tau = top[kk - 1]
    m0 = v1[0] + v2[0]
    picked = [cd >= tau for cd in cands]
    sels = [jnp.where(pk, 1.0, 0.0) for pk in picked]
    zparts = [jnp.where(pk, jnp.exp(cd - m0), 0.0) for pk, cd in zip(picked, cands)]
    zsum = jnp.sum(functools.reduce(jnp.add, zparts), axis=0, keepdims=True)
    n_rows = _row_counts(sels, sub)

    rank2_blocks, cnt_blocks = [], []
    for blk1, blk2 in zip(b1, b2):
        rk = functools.reduce(jnp.add, [jnp.where(v2[k] > blk2, 1.0, 0.0) for k in range(kk)])
        ct = jnp.zeros(blk1.shape, F32)
        for a in range(kk):
            ct = jnp.where(blk1 == v1[a], n_rows[a], ct)
        rank2_blocks.append(rk)
        cnt_blocks.append(ct)
    rank2 = jnp.concatenate(rank2_blocks, axis=0)
    cnt = jnp.concatenate(cnt_blocks, axis=0)

    count = lambda flags: jnp.sum(functools.reduce(jnp.add, flags), axis=0, keepdims=True)
    strict = lambda v: functools.reduce(jnp.logical_and, [v[k] > v[k + 1] for k in range(kk - 1)])[0:1]
    ok = (strict(v1) & strict(v2) & strict(top)
          & (count([jnp.where(blk >= v1[kk - 1], 1.0, 0.0) for blk in b1]) == float(kk))
          & (count([jnp.where(blk >= v2[kk - 1], 1.0, 0.0) for blk in b2]) == float(kk))
          & (functools.reduce(jnp.add, n_rows) == float(kk)))
    return rank2, cnt, zsum, v1[0][0:1], v2[0][0:1], ok


def _peer_route_kernel(q_ref, k1_ref, k2_ref, rank2_ref, cnt_ref, e1_ref, e2_ref, s1_ref, s2_ref):
    hp = lax.Precision.HIGHEST
    qh = q_ref[...]
    s1_ref[...] = _dot_nt(k1_ref[0], qh[:, :PEER_HALF], precision=hp)
    s2_ref[...] = _dot_nt(k2_ref[0], qh[:, PEER_HALF:], precision=hp)

    def column(ci, carry):
        cols = pl.ds(pl.multiple_of(ci * LANE, LANE), LANE)
        s1 = s1_ref[:, cols]
        s2 = s2_ref[:, cols]

        def emit(rank2, cnt, zsum, m1, m2):
            rank2_ref[0, :, cols] = rank2.astype(BF16)
            cnt_ref[0, :, cols] = cnt
            e1_ref[0, :, cols] = jnp.exp(s1 - m1)
            e2_ref[0, :, cols] = (jnp.exp(s2 - m2) / zsum).astype(BF16)

        *fast, ok = _route_columns_sorted(s1, s2)
        emit(*fast)
        n_bad = jnp.sum(jnp.where(ok, 0.0, 1.0))

        @pl.when(n_bad > 0.0)
        def _():
            emit(*_route_columns_exact(s1, s2))

        return carry

    lax.fori_loop(0, s1_ref.shape[1] // LANE, column, 0)


def _peer_route(q, k1, k2, tq=1024):
    t, _ = q.shape
    nh, nk, hd = k1.shape
    out = lambda dt: jax.ShapeDtypeStruct((nh, nk, t), dt)
    spec = pl.BlockSpec((1, nk, tq), lambda i, h: (h, 0, i))
    return pl.pallas_call(
        _peer_route_kernel,
        grid=(t // tq, nh),
        in_specs=[pl.BlockSpec((tq, PEER_KEY_DIM), lambda i, h: (i, h)),
                  pl.BlockSpec((1, nk, hd), lambda i, h: (h, 0, 0)),
                  pl.BlockSpec((1, nk, hd), lambda i, h: (h, 0, 0))],
        out_specs=[spec, spec, spec, spec],
        out_shape=[out(BF16), out(F32), out(F32), out(BF16)],
        scratch_shapes=[pltpu.VMEM((nk, tq), F32), pltpu.VMEM((nk, tq), F32)],
        compiler_params=_cparams(("arbitrary", "arbitrary")),
        name="peer_route",
    )(q, k1, k2)


BF16_ROWS = 16


def _gate_weights(w_ref, rank2_ref, e2_ref, cnt_ref, e1_ref):
    nk = PEER_N_KEYS
    tm = w_ref.shape[1]
    for i_loc in range(w_ref.shape[0] // nk):
        w = None
        for h in range(PEER_HEADS):
            cnt_b = jnp.broadcast_to(cnt_ref[h, i_loc:i_loc + 1, :], (BF16_ROWS, tm)).astype(BF16)
            e1_b = jnp.broadcast_to(e1_ref[h, i_loc:i_loc + 1, :], (BF16_ROWS, tm)).astype(BF16)
            wh = jnp.where(rank2_ref[h] < cnt_b[None], e2_ref[h], jnp.zeros((), BF16)) * e1_b[None]
            w = wh if w is None else w + wh
        w_ref[i_loc * nk:(i_loc + 1) * nk, :] = w.reshape(nk, tm)


def _peer_dense_kernel(n_sub, h2_ref, u_ref, vt_ref, rank2_ref, e2_ref, cnt_ref, e1_ref,
                       x1_ref, gf_ref, gpost_ref, o_ref, acc_ref, h2t_ref, w_ref, *g_refs):
    j = pl.program_id(1)

    @pl.when(j == 0)
    def _():
        acc_ref[...] = jnp.zeros_like(acc_ref)
        h2t_ref[...] = h2_ref[...].T

    nk = PEER_N_KEYS
    ts = u_ref.shape[0] // n_sub
    ats = [jnp.dot(u_ref[s * ts:(s + 1) * ts, :], h2t_ref[...], preferred_element_type=F32)
           for s in range(n_sub)]
    _gate_weights(w_ref, rank2_ref, e2_ref, cnt_ref, e1_ref)
    for s in range(n_sub):
        at = ats[s]
        for il in range(ts // nk):
            rows = slice(s * ts + il * nk, s * ts + (il + 1) * nk)
            a = at[il * nk:(il + 1) * nk, :]
            gelu = 0.5 * a * (1.0 + lax.erf(a * (1.0 / math.sqrt(2.0))))
            g_refs[s][il * nk:(il + 1) * nk, :] = gelu.astype(BF16) * w_ref[rows, :]
        acc_ref[...] += jnp.dot(vt_ref[:, s * ts:(s + 1) * ts], g_refs[s][...], preferred_element_type=F32)

    @pl.when(j == pl.num_programs(1) - 1)
    def _():
        y = acc_ref[...].T
        o_ref[...] = x1_ref[...] + gf_ref[0] * _rms(y, gpost_ref[...])


def _peer_dense(h2, u16, vt16, rank2, cnt, e1, e2, x1, gf, g_post, seq_len, tm=512, ts=512, n_sub=2):
    t, d = h2.shape
    ne = u16.shape[0]
    nh, nk, _ = cnt.shape
    te = ts * n_sub
    i_per_step = te // nk
    n_steps = ne // te
    aux_j = pl.BlockSpec((nh, nk // BF16_ROWS, BF16_ROWS, tm), lambda i, j: (0, 0, 0, i))
    aux_i = pl.BlockSpec((nh, i_per_step, tm), lambda i, j: (0, j, i))
    blocks_per_seq = seq_len // tm
    return pl.pallas_call(
        functools.partial(_peer_dense_kernel, n_sub),
        grid=(t // tm, n_steps),
        in_specs=[pl.BlockSpec((tm, d), lambda i, j: (i, 0)),
                  pl.BlockSpec((te, d), lambda i, j: (j, 0)),
                  pl.BlockSpec((d, te), lambda i, j: (0, j)),
                  aux_j, aux_j, aux_i, aux_i,
                  pl.BlockSpec((tm, d), lambda i, j: (i, 0), pipeline_mode=pl.Buffered(1)),
                  pl.BlockSpec((1, 1, d), lambda i, j: (i // blocks_per_seq, 0, 0)),
                  pl.BlockSpec((1, d), lambda i, j: (0, 0))],
        out_specs=pl.BlockSpec((tm, d), lambda i, j: (i, 0)),
        out_shape=jax.ShapeDtypeStruct((t, d), F32),
        scratch_shapes=[pltpu.VMEM((d, tm), F32), pltpu.VMEM((d, tm), BF16), pltpu.VMEM((te, tm), BF16)]
        + [pltpu.VMEM((ts, tm), BF16) for _ in range(n_sub)],
        compiler_params=_cparams(("arbitrary", "arbitrary")),
        name="peer_dense",
    )(h2, u16, vt16, rank2, e2, cnt, e1, x1, gf, g_post)


def _dft_tables(n):
    k = np.arange(n, dtype=np.int64)
    ph = (np.outer(k, k) % n).astype(np.float64) * (2.0 * np.pi / n)
    return np.cos(ph), np.sin(ph)


def kernel(x, c, ctx, c_ctx, w_mod, b_mod, g_pre_mix, g_post_mix, g_pre_ffn, g_post_ffn, w_in, w_fmix, conv_w, conv_b, dt_bias_f, dt_bias_b, a_log_f, a_log_b, d_skip_f, d_skip_b, ssm_norm_w, w_out, w_query, sub_keys_1, sub_keys_2, expert_u, expert_v):
    bsz, seq, d = x.shape
    ctx_len = ctx.shape[1]
    layer = 0
    n_heads = dt_bias_f.shape[1]
    d_ssm = n_heads * SSM_HEAD_DIM
    d_f = d - d_ssm
    gc = d_f // N_FOURIER_GROUPS
    d_xbc = conv_w.shape[2]
    assert w_mod.shape[0] == 1, "single-layer kernel"

    cond = jnp.zeros((8, d), F32).at[:bsz].set(c).at[bsz].set(c_ctx)
    mod = _adaln(cond, w_mod[layer], b_mod[layer])
    mods = [mod[:, i * d:(i + 1) * d] for i in range(N_MOD)]
    sm_x, cm_x, gm_x, sf_x, cf_x, gf_x = [m[:bsz, None, :] for m in mods]
    sm_c = jnp.broadcast_to(mods[0][bsz][None, None, :], (bsz, 1, d))
    cm_c = jnp.broadcast_to(mods[1][bsz][None, None, :], (bsz, 1, d))

    cos_l, sin_l = _dft_tables(seq)
    cos_c, sin_c = _dft_tables(gc)
    scale = 1.0 / math.sqrt(seq * gc)
    wfc, wfs = _fold_fourier(w_in[layer], w_fmix[layer],
                             jnp.asarray(cos_c * scale, F32), jnp.asarray(sin_c * scale, F32))
    dt_pad = 512
    w_rest = w_in[layer][:, d_f:].astype(BF16)
    w_all = jnp.concatenate(
        [wfc, wfs, w_rest, jnp.zeros((d, dt_pad - 2 * n_heads), BF16)], axis=1)
    col_z = 2 * d_f
    col_xbc = col_z + d_ssm
    col_dt = col_xbc + d_xbc
    n_all = col_dt + dt_pad
    dft = jnp.asarray(np.concatenate([cos_l, -sin_l], axis=1), F32).astype(BF16)

    pad32 = lambda f, b_: jnp.zeros((1, LANE), F32).at[0, :n_heads].set(f).at[0, n_heads:2 * n_heads].set(b_)
    bias_row = pad32(dt_bias_f[layer], dt_bias_b[layer])
    a_row = pad32(-jnp.exp(a_log_f[layer]), -jnp.exp(a_log_b[layer]))

    proj_c = _inproj(ctx, g_pre_mix[layer], sm_c, cm_c, w_all, col_xbc, n_all - col_xbc, tm=ctx_len, tn=512)
    xbc_c = _conv_silu(proj_c, 0, conv_w[layer], conv_b[layer])
    zero_state = jnp.zeros((bsz, 2, n_heads // 2, SSM_STATE, 2 * SSM_HEAD_DIM), F32)
    (fin_c,) = _ssd(proj_c, d_xbc, xbc_c, bias_row, a_row, zero_state, emit_y=False)

    proj_x = _inproj(x, g_pre_mix[layer], sm_x, cm_x, w_all, 0, n_all, tm=min(1024, seq), tn=512)
    o_f = _seqdft(dft, proj_x, d_f)
    xbc_x = _conv_silu(proj_x, col_xbc, conv_w[layer], conv_b[layer])
    y_f, y_b, _ = _ssd(proj_x, col_dt, xbc_x, bias_row, a_row, fin_c, emit_y=True)

    dsk = jnp.repeat(d_skip_f[layer] + d_skip_b[layer], SSM_HEAD_DIM)[None, :]
    x1, h2, q = _outproj(o_f, y_f, y_b, xbc_x, proj_x, col_z, x, dsk, ssm_norm_w[layer][None, :],
                         w_out[layer].astype(BF16), g_post_mix[layer][None, :], gm_x,
                         g_pre_ffn[layer][None, :], cf_x, sf_x, w_query[layer].astype(BF16))

    t = bsz * seq
    rank2, cnt, e1, e2 = _peer_route(q.reshape(t, -1), sub_keys_1[layer], sub_keys_2[layer],
                                     tq=min(1024, t))
    tiles = lambda a: a.reshape(a.shape[0], a.shape[1] // BF16_ROWS, BF16_ROWS, t)
    u16 = expert_u[layer].astype(BF16)
    vt16 = expert_v[layer].T.astype(BF16)
    out = _peer_dense(h2.reshape(t, d), u16, vt16, tiles(rank2), cnt, e1, tiles(e2), x1.reshape(t, d), gf_x,
                      g_post_ffn[layer][None, :], seq)
    return out.reshape(bsz, seq, d)
```

```python
import functools
import math

import jax
import jax.numpy as jnp
import numpy as np
from jax import lax
from jax.experimental import pallas as pl
from jax.experimental.pallas import tpu as pltpu

F32 = jnp.float32
BF16 = jnp.bfloat16

EPS = 1e-6
N_MOD = 6
N_FOURIER_GROUPS = 4
SSM_HEAD_DIM = 64
SSM_GROUPS = 2
SSM_STATE = 128
SSM_CHUNK = 128
PEER_HEADS = 8
PEER_TOPK = 16
PEER_N_KEYS = 128
PEER_KEY_DIM = 128
PEER_HALF = PEER_KEY_DIM // 2

LANE = 128
VMEM_LIMIT = 56 * 1024 * 1024


def _cparams(sem):
    return pltpu.CompilerParams(dimension_semantics=sem, vmem_limit_bytes=VMEM_LIMIT)


def _bdot(a, b):
    return jnp.dot(a.astype(BF16), b.astype(BF16), preferred_element_type=F32)


def _dot_nt(a, b, precision=None):
    return lax.dot_general(a, b, (((1,), (1,)), ((), ())), precision=precision,
                           preferred_element_type=F32)


def _rms(u, g):
    return u * lax.rsqrt(jnp.mean(u * u, axis=-1, keepdims=True) + EPS) * g


def _silu(u):
    return u * (1.0 / (1.0 + jnp.exp(-u)))


def _adaln_kernel(c_ref, w_ref, b_ref, o_ref):
    a = _silu(c_ref[...])
    o_ref[...] = _bdot(a, w_ref[...]) + b_ref[...]


def _adaln(cond, w_mod, b_mod, tn=1024):
    rows, d = cond.shape
    n = w_mod.shape[1]
    return pl.pallas_call(
        _adaln_kernel,
        grid=(n // tn,),
        in_specs=[pl.BlockSpec((rows, d), lambda j: (0, 0)),
                  pl.BlockSpec((d, tn), lambda j: (0, j)),
                  pl.BlockSpec((1, tn), lambda j: (0, j))],
        out_specs=pl.BlockSpec((rows, tn), lambda j: (0, j)),
        out_shape=jax.ShapeDtypeStruct((rows, n), F32),
        compiler_params=_cparams(("arbitrary",)),
        name="adaln",
    )(cond, w_mod, b_mod.reshape(1, n))


def _weights_kernel(ng, n_copy, w_ref, tail_ref, fm_ref, cc_ref, sc_ref, o_ref):
    hp = lax.Precision.HIGHEST
    g = pl.program_id(0)

    def fold(tbl_ref):
        m = jnp.dot(tbl_ref[...], fm_ref[0], precision=hp, preferred_element_type=F32)
        o_ref[...] = jnp.dot(w_ref[...], m, precision=hp, preferred_element_type=F32).astype(BF16)

    pl.when(g < ng)(lambda: fold(cc_ref))
    pl.when((g >= ng) & (g < 2 * ng))(lambda: fold(sc_ref))

    @pl.when((g >= 2 * ng) & (g < 2 * ng + n_copy))
    def _():
        o_ref[...] = w_ref[...].astype(BF16)

    @pl.when(g >= 2 * ng + n_copy)
    def _():
        o_ref[...] = tail_ref[...].astype(BF16)


def _build_weights(w_in, w_tail, w_fmix, cos_c, sin_c):
    d = w_in.shape[0]
    ng, gc, _ = w_fmix.shape
    n_copy = (w_in.shape[1] - ng * gc) // gc
    n_tail = w_tail.shape[1] // gc
    n_tiles = 2 * ng + n_copy + n_tail
    return pl.pallas_call(
        functools.partial(_weights_kernel, ng, n_copy),
        grid=(n_tiles,),
        in_specs=[pl.BlockSpec((d, gc), lambda g: (0, jnp.where(g < 2 * ng, g % ng, jnp.minimum(g - ng, ng + n_copy - 1)))),
                  pl.BlockSpec((d, gc), lambda g: (0, jnp.clip(g - (2 * ng + n_copy), 0, n_tail - 1))),
                  pl.BlockSpec((1, gc, gc), lambda g: (jnp.where(g < 2 * ng, g % ng, 0), 0, 0)),
                  pl.BlockSpec((gc, gc), lambda g: (0, 0)),
                  pl.BlockSpec((gc, gc), lambda g: (0, 0))],
        out_specs=pl.BlockSpec((d, gc), lambda g: (0, g)),
        out_shape=jax.ShapeDtypeStruct((d, n_tiles * gc), BF16),
        compiler_params=_cparams(("arbitrary",)),
        name="inproj_weights",
    )(w_in, w_tail, w_fmix, cos_c, sin_c)


def _inproj_kernel(x_ref, g_ref, sh_ref, sc_ref, w_ref, o_ref, h_ref):
    @pl.when(pl.program_id(2) == 0)
    def _():
        slab = min(LANE, h_ref.shape[0])

        def body(r, carry):
            rows = pl.ds(pl.multiple_of(r * slab, slab), slab)
            h = _rms(x_ref[0, rows, :], g_ref[...]) * (1.0 + sc_ref[0]) + sh_ref[0]
            h_ref[rows, :] = h.astype(BF16)
            return carry

        lax.fori_loop(0, h_ref.shape[0] // slab, body, 0)

    o_ref[0] = jnp.dot(h_ref[...], w_ref[...], preferred_element_type=F32)


def _inproj(xin, g, shift, scale, w_all, col0, ncols, tm, tn):
    b, l, d = xin.shape
    jb = col0 // tn
    return pl.pallas_call(
        _inproj_kernel,
        grid=(b, l // tm, ncols // tn),
        in_specs=[pl.BlockSpec((1, tm, d), lambda bi, i, j: (bi, i, 0)),
                  pl.BlockSpec((1, d), lambda bi, i, j: (0, 0)),
                  pl.BlockSpec((1, 1, d), lambda bi, i, j: (bi, 0, 0)),
                  pl.BlockSpec((1, 1, d), lambda bi, i, j: (bi, 0, 0)),
                  pl.BlockSpec((d, tn), lambda bi, i, j: (0, j + jb))],
        out_specs=pl.BlockSpec((1, tm, tn), lambda bi, i, j: (bi, i, j)),
        out_shape=jax.ShapeDtypeStruct((b, l, ncols), F32),
        scratch_shapes=[pltpu.VMEM((tm, d), BF16)],
        compiler_params=_cparams(("arbitrary", "arbitrary", "arbitrary")),
        name="inproj",
    )(xin, g.reshape(1, d), shift, scale, w_all)


def _seqdft_kernel(a_ref, x_ref, o_ref, acc_ref):
    k = pl.program_id(1)

    @pl.when(k == 0)
    def _():
        acc_ref[...] = jnp.zeros_like(acc_ref)

    acc_ref[...] += jnp.dot(a_ref[...], x_ref[0].astype(BF16), preferred_element_type=F32)

    @pl.when(k == pl.num_programs(1) - 1)
    def _():
        o_ref[0] = acc_ref[...].astype(BF16)


def _seqdft(dft, proj, n_f, tk=512):
    b, l, _ = proj.shape
    kl = l // tk
    return pl.pallas_call(
        _seqdft_kernel,
        grid=(b, 2 * kl),
        in_specs=[pl.BlockSpec((l, tk), lambda bi, k: (0, k)),
                  pl.BlockSpec((1, tk, n_f), lambda bi, k: (bi, k % kl, k // kl))],
        out_specs=pl.BlockSpec((1, l, n_f), lambda bi, k: (bi, 0, 0)),
        out_shape=jax.ShapeDtypeStruct((b, l, n_f), BF16),
        scratch_shapes=[pltpu.VMEM((l, n_f), F32)],
        compiler_params=_cparams(("arbitrary", "arbitrary")),
        name="seq_dft",
    )(dft, proj)


def _conv_kernel(u_ref, w_ref, b_ref, o_ref):
    u = u_ref[0]
    l = u.shape[0]
    row = lax.broadcasted_iota(jnp.int32, u.shape, 0)
    prev = jnp.where(row == 0, 0.0, pltpu.roll(u, 1, 0))
    nxt = jnp.where(row == l - 1, 0.0, pltpu.roll(u, l - 1, 0))
    w = w_ref[...]
    o_ref[0] = _silu(prev * w[0:1] + u * w[1:2] + nxt * w[2:3] + b_ref[...])


def _conv_silu(proj, col0, conv_w, conv_b, tc=256):
    b, l, _ = proj.shape
    n = conv_w.shape[1]
    jb = col0 // tc
    return pl.pallas_call(
        _conv_kernel,
        grid=(b, n // tc),
        in_specs=[pl.BlockSpec((1, l, tc), lambda bi, j: (bi, 0, j + jb)),
                  pl.BlockSpec((3, tc), lambda bi, j: (0, j)),
                  pl.BlockSpec((1, tc), lambda bi, j: (0, j))],
        out_specs=pl.BlockSpec((1, l, tc), lambda bi, j: (bi, 0, j)),
        out_shape=jax.ShapeDtypeStruct((b, l, n), F32),
        compiler_params=_cparams(("arbitrary", "arbitrary")),
        name="conv_silu",
    )(proj, conv_w, conv_b.reshape(1, n))


def _ssd_direction(reverse, dt_raw, bias, a_neg, xs_ref, bm_ref, cm_ref, st_ref, y_ref, n_heads):
    hp = lax.Precision.HIGHEST
    t = SSM_CHUNK
    p2 = 2 * SSM_HEAD_DIM
    col0 = n_heads if reverse else 0
    r = lax.broadcasted_iota(jnp.int32, (t, t), 0)
    c = lax.broadcasted_iota(jnp.int32, (t, t), 1)
    mask = (r <= c) if reverse else (r >= c)
    tri = mask.astype(F32)
    tri_t = ((r >= c) if reverse else (r <= c)).astype(F32)
    first_y = lax.broadcasted_iota(jnp.int32, (t, p2), 1) < SSM_HEAD_DIM
    first_s = lax.broadcasted_iota(jnp.int32, (SSM_STATE, p2), 1) < SSM_HEAD_DIM

    z = dt_raw + bias
    dtv = jnp.maximum(z, 0.0) + jnp.log(1.0 + jnp.exp(-jnp.abs(z)))
    a = dtv * a_neg
    a_t = a.T
    dtv_t = dtv.T
    cs = jnp.dot(tri, a, precision=hp, preferred_element_type=F32)
    cs_t = jnp.dot(a_t, tri_t, precision=hp, preferred_element_type=F32)
    w_t = dtv_t * jnp.exp(jnp.sum(a_t, axis=1, keepdims=True) - cs_t)
    etot = jnp.exp(jnp.sum(a, axis=0, keepdims=True))

    ppg = n_heads // SSM_GROUPS // 2
    for g in range(SSM_GROUPS):
        bg = bm_ref[0, :, g * SSM_STATE:(g + 1) * SSM_STATE]
        cg16 = cm_ref[0, :, g * SSM_STATE:(g + 1) * SSM_STATE].astype(BF16)
        gmat = _dot_nt(cg16, bg.astype(BF16))
        bgt = bg.T
        if y_ref is not None:
            st_g16 = jnp.concatenate([st_ref[g * ppg + k] for k in range(ppg)], axis=1).astype(BF16)
            yoff_g = jnp.dot(cg16, st_g16, preferred_element_type=F32)
        for k in range(ppg):
            pi = g * ppg + k
            xs16 = xs_ref[0, :, pi * p2:(pi + 1) * p2].astype(BF16)
            ys, ss, ecols, etots = [], [], [], []
            for e in range(2):
                j = col0 + 2 * pi + e
                col_b = jnp.broadcast_to(cs[:, j:j + 1], (t, t))
                lmat = jnp.exp(jnp.where(mask, col_b - cs_t[j:j + 1, :], -jnp.inf))
                m16 = (gmat * lmat * dtv_t[j:j + 1, :]).astype(BF16)
                if y_ref is not None:
                    ys.append(jnp.dot(m16, xs16, preferred_element_type=F32))
                    ecols.append(jnp.exp(col_b))
                bw16 = (bgt * w_t[j:j + 1, :]).astype(BF16)
                ss.append(jnp.dot(bw16, xs16, preferred_element_type=F32))
                etots.append(etot[:, j:j + 1])
            if y_ref is not None:
                y_off = yoff_g[:, k * p2:(k + 1) * p2] * jnp.where(first_y, ecols[0], ecols[1])
                y_ref[0, :, pi * p2:(pi + 1) * p2] = jnp.where(first_y, ys[0], ys[1]) + y_off
            st_ref[pi] = (jnp.where(first_s, etots[0], etots[1]) * st_ref[pi]
                          + jnp.where(first_s, ss[0], ss[1]))


def _ssd_kernel(n_heads, emit_y, *refs):
    (dtf_ref, dtb_ref, bias_ref, a_ref, xsf_ref, xsb_ref, bmf_ref, bmb_ref, cmf_ref, cmb_ref,
     init_ref) = refs[:11]
    if emit_y:
        yf_ref, yb_ref, fin_ref, stf_ref, stb_ref = refs[11:]
    else:
        yf_ref = yb_ref = None
        fin_ref, stf_ref, stb_ref = refs[11:]
    ci = pl.program_id(1)

    @pl.when(ci == 0)
    def _():
        stf_ref[...] = init_ref[0, 0]
        stb_ref[...] = init_ref[0, 1]

    bias = bias_ref[...]
    a_neg = a_ref[...]
    _ssd_direction(False, dtf_ref[0], bias, a_neg, xsf_ref, bmf_ref, cmf_ref, stf_ref, yf_ref, n_heads)
    _ssd_direction(True, dtb_ref[0], bias, a_neg, xsb_ref, bmb_ref, cmb_ref, stb_ref, yb_ref, n_heads)

    @pl.when(ci == pl.num_programs(1) - 1)
    def _():
        fin_ref[0, 0] = stf_ref[...]
        fin_ref[0, 1] = stb_ref[...]


def _ssd(proj, dt_col0, xbc, bias_row, a_row, init, emit_y):
    b, l, _ = proj.shape
    d_ssm = xbc.shape[2] - 2 * SSM_GROUPS * SSM_STATE
    d_bc = SSM_GROUPS * SSM_STATE
    n_heads = d_ssm // SSM_HEAD_DIM
    t = SSM_CHUNK
    nc = l // t
    dtb = dt_col0 // LANE
    fwd = lambda bi, c: (bi, c, 0)
    bwd = lambda bi, c: (bi, nc - 1 - c, 0)
    off = lambda f, o: (lambda bi, c: f(bi, c)[:2] + (o,))
    st_shape = (n_heads // 2, SSM_STATE, 2 * SSM_HEAD_DIM)
    in_specs = [
        pl.BlockSpec((1, t, LANE), off(fwd, dtb)), pl.BlockSpec((1, t, LANE), off(bwd, dtb)),
        pl.BlockSpec((1, LANE), lambda bi, c: (0, 0)), pl.BlockSpec((1, LANE), lambda bi, c: (0, 0)),
        pl.BlockSpec((1, t, d_ssm), fwd), pl.BlockSpec((1, t, d_ssm), bwd),
        pl.BlockSpec((1, t, d_bc), off(fwd, d_ssm // d_bc)), pl.BlockSpec((1, t, d_bc), off(bwd, d_ssm // d_bc)),
        pl.BlockSpec((1, t, d_bc), off(fwd, d_ssm // d_bc + 1)), pl.BlockSpec((1, t, d_bc), off(bwd, d_ssm // d_bc + 1)),
        pl.BlockSpec((1, 2) + st_shape, lambda bi, c: (bi, 0, 0, 0, 0)),
    ]
    fin_spec = pl.BlockSpec((1, 2) + st_shape, lambda bi, c: (bi, 0, 0, 0, 0))
    fin_shape = jax.ShapeDtypeStruct((b, 2) + st_shape, F32)
    if emit_y:
        y_shape = jax.ShapeDtypeStruct((b, l, d_ssm), F32)
        out_specs = [pl.BlockSpec((1, t, d_ssm), fwd), pl.BlockSpec((1, t, d_ssm), bwd), fin_spec]
        out_shape = [y_shape, y_shape, fin_shape]
    else:
        out_specs = [fin_spec]
        out_shape = [fin_shape]
    return pl.pallas_call(
        functools.partial(_ssd_kernel, n_heads, emit_y),
        grid=(b, nc),
        in_specs=in_specs,
        out_specs=out_specs,
        out_shape=out_shape,
        scratch_shapes=[pltpu.VMEM(st_shape, F32), pltpu.VMEM(st_shape, F32)],
        compiler_params=_cparams(("arbitrary", "arbitrary")),
        name="ssd_scan_y" if emit_y else "ssd_scan_state",
    )(proj, proj, bias_row, a_row, xbc, xbc, xbc, xbc, xbc, xbc, init)


def _outproj_kernel(of_ref, yf_ref, yb_ref, xs_ref, z_ref, x_ref, dsk_ref, nw_ref, wo_ref, gpost_ref,
                    gm_ref, gpre_ref, cf_ref, sf_ref, wq_ref, x1_ref, h2_ref, q_ref):
    d_f = of_ref.shape[2]
    y = yf_ref[0] + yb_ref[0] + dsk_ref[...] * xs_ref[0]
    y = y * _silu(z_ref[0])
    gw = y.shape[1] // SSM_GROUPS
    nw = nw_ref[...]
    parts = [_rms(y[:, g * gw:(g + 1) * gw], nw[:, g * gw:(g + 1) * gw]).astype(BF16) for g in range(SSM_GROUPS)]
    yx = jnp.dot(of_ref[0], wo_ref[:d_f, :], preferred_element_type=F32)
    for g in range(SSM_GROUPS):
        yx = yx + jnp.dot(parts[g], wo_ref[d_f + g * gw:d_f + (g + 1) * gw, :], preferred_element_type=F32)
    x1 = x_ref[0] + gm_ref[0] * _rms(yx, gpost_ref[...])
    x1_ref[0] = x1
    h2 = (_rms(x1, gpre_ref[...]) * (1.0 + cf_ref[0]) + sf_ref[0]).astype(BF16)
    h2_ref[0] = h2
    q_ref[0] = jnp.dot(h2, wq_ref[...], preferred_element_type=F32)


def _outproj(of, yf, yb, xbc, proj, z_col0, x, dsk, nw, w_out, g_post, gm, g_pre, cf, sf, w_query, tm=256):
    b, l, d = x.shape
    d_f = of.shape[2]
    d_s = yf.shape[2]
    nq = w_query.shape[1]
    zb = z_col0 // d_s
    row = lambda bi, i: (bi, i, 0)
    vec = lambda n: pl.BlockSpec((1, n), lambda bi, i: (0, 0))
    bvec = lambda n: pl.BlockSpec((1, 1, n), lambda bi, i: (bi, 0, 0))
    return pl.pallas_call(
        _outproj_kernel,
        grid=(b, l // tm),
        in_specs=[pl.BlockSpec((1, tm, d_f), row), pl.BlockSpec((1, tm, d_s), row), pl.BlockSpec((1, tm, d_s), row),
                  pl.BlockSpec((1, tm, d_s), row), pl.BlockSpec((1, tm, d_s), lambda bi, i: (bi, i, zb)),
                  pl.BlockSpec((1, tm, d), row), vec(d_s), vec(d_s),
                  pl.BlockSpec((d_f + d_s, d), lambda bi, i: (0, 0)), vec(d), bvec(d), vec(d), bvec(d), bvec(d),
                  pl.BlockSpec((d, nq), lambda bi, i: (0, 0))],
        out_specs=[pl.BlockSpec((1, tm, d), row), pl.BlockSpec((1, tm, d), row), pl.BlockSpec((1, tm, nq), row)],
        out_shape=[jax.ShapeDtypeStruct((b, l, d), F32), jax.ShapeDtypeStruct((b, l, d), BF16),
                   jax.ShapeDtypeStruct((b, l, nq), F32)],
        compiler_params=_cparams(("arbitrary", "arbitrary")),
        name="outproj_prenorm_query",
    )(of, yf, yb, xbc, proj, x, dsk, nw, w_out, g_post, gm, g_pre, cf, sf, w_query)


SUBLANES = 8


def _top16_ranked(s):
    n = s.shape[0]
    iota = lax.broadcasted_iota(jnp.int32, s.shape, 0).astype(F32)
    cur = s
    rank = jnp.full(s.shape, float(PEER_TOPK), F32)
    vals = []
    for k in range(PEER_TOPK):
        m = jnp.max(cur, axis=0, keepdims=True)
        idx = jnp.min(jnp.where(cur == m, iota, float(n)), axis=0, keepdims=True)
        hit = iota == idx
        rank = jnp.where(hit, float(k), rank)
        cur = jnp.where(hit, -jnp.inf, cur)
        vals.append(m)
    return vals, rank


def _candidate_pieces(v1, v2, sub):
    def stack8(vals):
        out = jnp.zeros(sub.shape, F32)
        for k, v in enumerate(vals):
            out = jnp.where(sub == k, v, out)
        return out

    v2_lo, v2_hi, v1_hi = stack8(v2[:8]), stack8(v2[8:]), stack8(v1[8:])
    return [v1[0] + v2_lo, v1[0] + v2_hi] + [v1[a] + v2_lo for a in range(1, 8)] + [v1_hi + v2[0]]


def _row_counts(sels, sub):
    n_rows = [jnp.sum(sels[0] + sels[1], axis=0, keepdims=True)]
    n_rows += [jnp.sum(sels[a + 1], axis=0, keepdims=True) for a in range(1, 8)]
    n_rows += [jnp.sum(jnp.where(sub == r, sels[9], 0.0), axis=0, keepdims=True) for r in range(8)]
    return n_rows


def _route_columns_exact(s1, s2):
    kk = PEER_TOPK
    v1, rank1 = _top16_ranked(s1)
    v2, rank2 = _top16_ranked(s2)
    sub = lax.broadcasted_iota(jnp.int32, (SUBLANES, s1.shape[1]), 0)
    subf = sub.astype(F32)
    cands = _candidate_pieces(v1, v2, sub)
    poses = [subf, subf + 8.0] + [a * kk + subf for a in range(1, 8)] + [(subf + 8.0) * kk]
    sels = [jnp.zeros(sub.shape, F32) for _ in cands]
    m0 = v1[0] + v2[0]
    zsum = jnp.zeros_like(m0)
    for _ in range(kk):
        m = jnp.max(functools.reduce(jnp.maximum, cands), axis=0, keepdims=True)
        firsts = [jnp.where(cd == m, ps, float(kk * kk)) for cd, ps in zip(cands, poses)]
        p = jnp.min(functools.reduce(jnp.minimum, firsts), axis=0, keepdims=True)
        hits = [ps == p for ps in poses]
        sels = [jnp.where(ht, 1.0, sl) for ht, sl in zip(hits, sels)]
        cands = [jnp.where(ht, -jnp.inf, cd) for ht, cd in zip(hits, cands)]
        zsum = zsum + jnp.exp(m - m0)
    n_rows = _row_counts(sels, sub)
    cnt = jnp.zeros(rank1.shape, F32)
    for a in range(kk):
        cnt = jnp.where(rank1 == float(a), n_rows[a], cnt)
    return rank2, cnt, zsum, v1[0], v2[0]


def _sorted_top16(blocks):
    v = list(blocks)
    n = len(v)

    def exchange(i, l, descending):
        hi, lo = jnp.maximum(v[i], v[l]), jnp.minimum(v[i], v[l])
        v[i], v[l] = (hi, lo) if descending else (lo, hi)

    k = 2
    while k <= n:
        j = k // 2
        while j >= 1:
            for i in range(n):
                if i ^ j > i:
                    exchange(i, i ^ j, (i & k) == 0)
            j //= 2
        k *= 2
    for shift in (4, 2, 1):
        w = [pltpu.roll(x, shift, 0) for x in v]
        v = [jnp.maximum(v[r], w[n - 1 - r]) for r in range(n)]
        j = n // 2
        while j >= 1:
            for i in range(n):
                if i ^ j > i:
                    exchange(i, i ^ j, True)
            j //= 2
    return v


def _route_columns_sorted(s1, s2):
    kk = PEER_TOPK
    nb = s1.shape[0] // SUBLANES
    tcols = s1.shape[1]
    blocks = lambda s: [s[SUBLANES * r:SUBLANES * (r + 1), :] for r in range(nb)]
    b1, b2 = blocks(s1), blocks(s2)
    v1, v2 = _sorted_top16(b1), _sorted_top16(b2)
    sub = lax.broadcasted_iota(jnp.int32, (SUBLANES, tcols), 0)
    cands = _candidate_pieces(v1, v2, sub)
    neg = jnp.full((SUBLANES, tcols), -jnp.inf, F32)
    top = _sorted_top16(cands + [neg] * (nb - len(cands)))
    tau = top[kk - 1]
    m0 = v1[0] + v2[0]
    picked = [cd >= tau for cd in cands]
    sels = [jnp.where(pk, 1.0, 0.0) for pk in picked]
    zparts = [jnp.where(pk, jnp.exp(cd - m0), 0.0) for pk, cd in zip(picked, cands)]
    zsum = jnp.sum(functools.reduce(jnp.add, zparts), axis=0, keepdims=True)
    n_rows = _row_counts(sels, sub)

    rank2_blocks, cnt_blocks = [], []
    for blk1, blk2 in zip(b1, b2):
        rk = functools.reduce(jnp.add, [jnp.where(v2[k] > blk2, 1.0, 0.0) for k in range(kk)])
        ct = jnp.zeros(blk1.shape, F32)
        for a in range(kk):
            ct = jnp.where(blk1 == v1[a], n_rows[a], ct)
        rank2_blocks.append(rk)
        cnt_blocks.append(ct)
    rank2 = jnp.concatenate(rank2_blocks, axis=0)
    cnt = jnp.concatenate(cnt_blocks, axis=0)

    count = lambda flags: jnp.sum(functools.reduce(jnp.add, flags), axis=0, keepdims=True)
    strict = lambda v: functools.reduce(jnp.logical_and, [v[k] > v[k + 1] for k in range(kk - 1)])[0:1]
    ok = (strict(v1) & strict(v2) & strict(top)
          & (count([jnp.where(blk >= v1[kk - 1], 1.0, 0.0) for blk in b1]) == float(kk))
          & (count([jnp.where(blk >= v2[kk - 1], 1.0, 0.0) for blk in b2]) == float(kk))
          & (functools.reduce(jnp.add, n_rows) == float(kk)))
    return rank2, cnt, zsum, v1[0][0:1], v2[0][0:1], ok


def _peer_route_kernel(q_ref, k1_ref, k2_ref, rank2_ref, cnt_ref, e1_ref, e2_ref, s1_ref, s2_ref):
    def split(v):
        hi = v.astype(BF16)
        return hi, (v - hi.astype(F32)).astype(BF16)

    def scores(keys, qs):
        k_hi, k_lo = split(keys)
        q_hi, q_lo = split(qs)
        return _dot_nt(k_hi, q_hi) + (_dot_nt(k_hi, q_lo) + _dot_nt(k_lo, q_hi))

    qh = q_ref[...]
    s1_ref[...] = scores(k1_ref[0], qh[:, :PEER_HALF])
    s2_ref[...] = scores(k2_ref[0], qh[:, PEER_HALF:])

    def column(ci, carry):
        cols = pl.ds(pl.multiple_of(ci * LANE, LANE), LANE)
        s1 = s1_ref[:, cols]
        s2 = s2_ref[:, cols]

        def emit(rank2, cnt, zsum, m1, m2):
            rank2_ref[0, :, cols] = rank2.astype(BF16)
            cnt_ref[0, :, cols] = cnt
            e1_ref[0, :, cols] = jnp.exp(s1 - m1)
            e2_ref[0, :, cols] = (jnp.exp(s2 - m2) / zsum).astype(BF16)

        *fast, ok = _route_columns_sorted(s1, s2)
        emit(*fast)
        n_bad = jnp.sum(jnp.where(ok, 0.0, 1.0))

        @pl.when(n_bad > 0.0)
        def _():
            emit(*_route_columns_exact(s1, s2))

        return carry

    lax.fori_loop(0, s1_ref.shape[1] // LANE, column, 0)


def _peer_route(q, k1, k2, tq=1024):
    t, _ = q.shape
    nh, nk, hd = k1.shape
    out = lambda dt: jax.ShapeDtypeStruct((nh, nk, t), dt)
    spec = pl.BlockSpec((1, nk, tq), lambda i, h: (h, 0, i))
    return pl.pallas_call(
        _peer_route_kernel,
        grid=(t // tq, nh),
        in_specs=[pl.BlockSpec((tq, PEER_KEY_DIM), lambda i, h: (i, h)),
                  pl.BlockSpec((1, nk, hd), lambda i, h: (h, 0, 0)),
                  pl.BlockSpec((1, nk, hd), lambda i, h: (h, 0, 0))],
        out_specs=[spec, spec, spec, spec],
        out_shape=[out(BF16), out(F32), out(F32), out(BF16)],
        scratch_shapes=[pltpu.VMEM((nk, tq), F32), pltpu.VMEM((nk, tq), F32)],
        compiler_params=_cparams(("arbitrary", "arbitrary")),
        name="peer_route",
    )(q, k1, k2)


BF16_ROWS = 16


def _gate_weights(w_ref, rank2_ref, e2_ref, cnt_ref, e1_ref):
    nk = PEER_N_KEYS
    tm = w_ref.shape[1]
    for i_loc in range(w_ref.shape[0] // nk):
        w = None
        for h in range(PEER_HEADS):
            cnt_b = jnp.broadcast_to(cnt_ref[h, i_loc:i_loc + 1, :], (BF16_ROWS, tm)).astype(BF16)
            e1_b = jnp.broadcast_to(e1_ref[h, i_loc:i_loc + 1, :], (BF16_ROWS, tm)).astype(BF16)
            wh = jnp.where(rank2_ref[h] < cnt_b[None], e2_ref[h], jnp.zeros((), BF16)) * e1_b[None]
            w = wh if w is None else w + wh
        w_ref[i_loc * nk:(i_loc + 1) * nk, :] = w.reshape(nk, tm)


def _peer_dense_kernel(n_sub, h2_ref, u_ref, vt_ref, rank2_ref, e2_ref, cnt_ref, e1_ref,
                       x1_ref, gf_ref, gpost_ref, o_ref, acc_ref, h2t_ref, w_ref, *g_refs):
    j = pl.program_id(1)

    @pl.when(j == 0)
    def _():
        acc_ref[...] = jnp.zeros_like(acc_ref)
        h2t_ref[...] = h2_ref[...].T

    nk = PEER_N_KEYS
    ts = u_ref.shape[0] // n_sub
    ats = [jnp.dot(u_ref[s * ts:(s + 1) * ts, :], h2t_ref[...], preferred_element_type=F32)
           for s in range(n_sub)]
    _gate_weights(w_ref, rank2_ref, e2_ref, cnt_ref, e1_ref)
    for s in range(n_sub):
        at = ats[s]
        for il in range(ts // nk):
            rows = slice(s * ts + il * nk, s * ts + (il + 1) * nk)
            a = at[il * nk:(il + 1) * nk, :]
            gelu = 0.5 * a * (1.0 + lax.erf(a * (1.0 / math.sqrt(2.0))))
            g_refs[s][il * nk:(il + 1) * nk, :] = gelu.astype(BF16) * w_ref[rows, :]
        acc_ref[...] += jnp.dot(vt_ref[:, s * ts:(s + 1) * ts], g_refs[s][...], preferred_element_type=F32)

    @pl.when(j == pl.num_programs(1) - 1)
    def _():
        y = acc_ref[...].T
        o_ref[...] = x1_ref[...] + gf_ref[0] * _rms(y, gpost_ref[...])


def _peer_dense(h2, u16, vt16, rank2, cnt, e1, e2, x1, gf, g_post, seq_len, tm=512, ts=512, n_sub=2):
    t, d = h2.shape
    ne = u16.shape[0]
    nh, nk, _ = cnt.shape
    te = ts * n_sub
    i_per_step = te // nk
    n_steps = ne // te
    aux_j = pl.BlockSpec((nh, nk // BF16_ROWS, BF16_ROWS, tm), lambda i, j: (0, 0, 0, i))
    aux_i = pl.BlockSpec((nh, i_per_step, tm), lambda i, j: (0, j, i))
    blocks_per_seq = seq_len // tm
    return pl.pallas_call(
        functools.partial(_peer_dense_kernel, n_sub),
        grid=(t // tm, n_steps),
        in_specs=[pl.BlockSpec((tm, d), lambda i, j: (i, 0)),
                  pl.BlockSpec((te, d), lambda i, j: (j, 0)),
                  pl.BlockSpec((d, te), lambda i, j: (0, j)),
                  aux_j, aux_j, aux_i, aux_i,
                  pl.BlockSpec((tm, d), lambda i, j: (i, 0), pipeline_mode=pl.Buffered(1)),
                  pl.BlockSpec((1, 1, d), lambda i, j: (i // blocks_per_seq, 0, 0)),
                  pl.BlockSpec((1, d), lambda i, j: (0, 0))],
        out_specs=pl.BlockSpec((tm, d), lambda i, j: (i, 0)),
        out_shape=jax.ShapeDtypeStruct((t, d), F32),
        scratch_shapes=[pltpu.VMEM((d, tm), F32), pltpu.VMEM((d, tm), BF16), pltpu.VMEM((te, tm), BF16)]
        + [pltpu.VMEM((ts, tm), BF16) for _ in range(n_sub)],
        compiler_params=_cparams(("arbitrary", "arbitrary")),
        name="peer_dense",
    )(h2, u16, vt16, rank2, e2, cnt, e1, x1, gf, g_post)


def _dft_tables(n):
    k = np.arange(n, dtype=np.int64)
    ph = (np.outer(k, k) % n).astype(np.float64) * (2.0 * np.pi / n)
    return np.cos(ph), np.sin(ph)


def kernel(x, c, ctx, c_ctx, w_mod, b_mod, g_pre_mix, g_post_mix, g_pre_ffn, g_post_ffn, w_in, w_fmix, conv_w, conv_b, dt_bias_f, dt_bias_b, a_log_f, a_log_b, d_skip_f, d_skip_b, ssm_norm_w, w_out, w_query, sub_keys_1, sub_keys_2, expert_u, expert_v):
    bsz, seq, d = x.shape
    ctx_len = ctx.shape[1]
    layer = 0
    n_heads = dt_bias_f.shape[1]
    d_ssm = n_heads * SSM_HEAD_DIM
    d_f = d - d_ssm
    gc = d_f // N_FOURIER_GROUPS
    d_xbc = conv_w.shape[2]
    assert w_mod.shape[0] == 1, "single-layer kernel"

    cond = jnp.zeros((8, d), F32).at[:bsz].set(c).at[bsz].set(c_ctx)
    mod = _adaln(cond, w_mod[layer], b_mod[layer])
    mods = [mod[:, i * d:(i + 1) * d] for i in range(N_MOD)]
    sm_x, cm_x, gm_x, sf_x, cf_x, gf_x = [m[:bsz, None, :] for m in mods]
    sm_c = jnp.broadcast_to(mods[0][bsz][None, None, :], (bsz, 1, d))
    cm_c = jnp.broadcast_to(mods[1][bsz][None, None, :], (bsz, 1, d))

    cos_l, sin_l = _dft_tables(seq)
    cos_c, sin_c = _dft_tables(gc)
    scale = 1.0 / math.sqrt(seq * gc)
    dt_pad = 512
    col_z = 2 * d_f
    col_xbc = col_z + d_ssm
    col_dt = col_xbc + d_xbc
    n_all = col_dt + dt_pad
    assert (d_ssm + d_xbc) % gc == 0 and dt_pad % gc == 0, "z/xBC columns must fill whole weight tiles"
    w_tail = jnp.pad(w_in[layer][:, d_f + d_ssm + d_xbc:], ((0, 0), (0, dt_pad - 2 * n_heads)))
    w_all = _build_weights(w_in[layer], w_tail, w_fmix[layer],
                           jnp.asarray(cos_c * scale, F32), jnp.asarray(sin_c * scale, F32))
    assert w_all.shape[1] == n_all
    dft = jnp.asarray(np.concatenate([cos_l, -sin_l], axis=1), F32).astype(BF16)

    pad32 = lambda f, b_: jnp.zeros((1, LANE), F32).at[0, :n_heads].set(f).at[0, n_heads:2 * n_heads].set(b_)
    bias_row = pad32(dt_bias_f[layer], dt_bias_b[layer])
    a_row = pad32(-jnp.exp(a_log_f[layer]), -jnp.exp(a_log_b[layer]))

    proj_c = _inproj(ctx, g_pre_mix[layer], sm_c, cm_c, w_all, col_xbc, n_all - col_xbc, tm=ctx_len, tn=512)
    xbc_c = _conv_silu(proj_c, 0, conv_w[layer], conv_b[layer])
    zero_state = jnp.zeros((bsz, 2, n_heads // 2, SSM_STATE, 2 * SSM_HEAD_DIM), F32)
    (fin_c,) = _ssd(proj_c, d_xbc, xbc_c, bias_row, a_row, zero_state, emit_y=False)

    proj_x = _inproj(x, g_pre_mix[layer], sm_x, cm_x, w_all, 0, n_all, tm=min(1024, seq), tn=1024)
    o_f = _seqdft(dft, proj_x, d_f)
    xbc_x = _conv_silu(proj_x, col_xbc, conv_w[layer], conv_b[layer])
    y_f, y_b, _ = _ssd(proj_x, col_dt, xbc_x, bias_row, a_row, fin_c, emit_y=True)

    dsk = jnp.repeat(d_skip_f[layer] + d_skip_b[layer], SSM_HEAD_DIM)[None, :]
    x1, h2, q = _outproj(o_f, y_f, y_b, xbc_x, proj_x, col_z, x, dsk, ssm_norm_w[layer][None, :],
                         w_out[layer].astype(BF16), g_post_mix[layer][None, :], gm_x,
                         g_pre_ffn[layer][None, :], cf_x, sf_x, w_query[layer].astype(BF16))

    t = bsz * seq
    rank2, cnt, e1, e2 = _peer_route(q.reshape(t, -1), sub_keys_1[layer], sub_keys_2[layer],
                                     tq=min(1024, t))
    tiles = lambda a: a.reshape(a.shape[0], a.shape[1] // BF16_ROWS, BF16_ROWS, t)
    u16 = expert_u[layer].astype(BF16)
    vt16 = expert_v[layer].T.astype(BF16)
    out = _peer_dense(h2.reshape(t, d), u16, vt16, tiles(rank2), cnt, e1, tiles(e2), x1.reshape(t, d), gf_x,
                      g_post_ffn[layer][None, :], seq)
    return out.reshape(bsz, seq, d)
```

```python
import functools
import math

import jax
import jax.numpy as jnp
import numpy as np
from jax import lax
from jax.experimental import pallas as pl
from jax.experimental.pallas import tpu as pltpu

F32 = jnp.float32
BF16 = jnp.bfloat16

EPS = 1e-6
N_MOD = 6
N_FOURIER_GROUPS = 4
SSM_HEAD_DIM = 64
SSM_GROUPS = 2
SSM_STATE = 128
SSM_CHUNK = 128
PEER_HEADS = 8
PEER_TOPK = 16
PEER_N_KEYS = 128
PEER_KEY_DIM = 128
PEER_HALF = PEER_KEY_DIM // 2

LANE = 128
VMEM_LIMIT = 56 * 1024 * 1024


def _cparams(sem):
    return pltpu.CompilerParams(dimension_semantics=sem, vmem_limit_bytes=VMEM_LIMIT)


def _bdot(a, b):
    return jnp.dot(a.astype(BF16), b.astype(BF16), preferred_element_type=F32)


def _dot_nt(a, b, precision=None):
    return lax.dot_general(a, b, (((1,), (1,)), ((), ())), precision=precision,
                           preferred_element_type=F32)


def _rms(u, g):
    return u * lax.rsqrt(jnp.mean(u * u, axis=-1, keepdims=True) + EPS) * g


def _silu(u):
    return u * (1.0 / (1.0 + jnp.exp(-u)))


def _adaln_kernel(c_ref, w_ref, b_ref, o_ref):
    a = _silu(c_ref[...])
    o_ref[...] = _bdot(a, w_ref[...]) + b_ref[...]


def _adaln(cond, w_mod, b_mod, tn=1024):
    rows, d = cond.shape
    n = w_mod.shape[1]
    return pl.pallas_call(
        _adaln_kernel,
        grid=(n // tn,),
        in_specs=[pl.BlockSpec((rows, d), lambda j: (0, 0)),
                  pl.BlockSpec((d, tn), lambda j: (0, j)),
                  pl.BlockSpec((1, tn), lambda j: (0, j))],
        out_specs=pl.BlockSpec((rows, tn), lambda j: (0, j)),
        out_shape=jax.ShapeDtypeStruct((rows, n), F32),
        compiler_params=_cparams(("arbitrary",)),
        name="adaln",
    )(cond, w_mod, b_mod.reshape(1, n))


def _weights_kernel(ng, n_copy, wt_ref, tail_ref, fm_ref, cc_ref, sc_ref, o_ref):
    hp = lax.Precision.HIGHEST
    g = pl.program_id(0)

    def fold(tbl_ref):
        m = jnp.dot(tbl_ref[...], fm_ref[0], precision=hp, preferred_element_type=F32)
        o_ref[...] = jnp.dot(wt_ref[...].T, m, precision=hp, preferred_element_type=F32).astype(BF16)

    pl.when(g < ng)(lambda: fold(cc_ref))
    pl.when((g >= ng) & (g < 2 * ng))(lambda: fold(sc_ref))

    @pl.when((g >= 2 * ng) & (g < 2 * ng + n_copy))
    def _():
        o_ref[...] = wt_ref[...].T.astype(BF16)

    @pl.when(g == 2 * ng + n_copy)
    def _():
        tail = tail_ref[...]
        fill = jnp.zeros((o_ref.shape[1] - tail.shape[0], tail.shape[1]), F32)
        o_ref[...] = jnp.concatenate([tail, fill], axis=0).T.astype(BF16)

    @pl.when(g > 2 * ng + n_copy)
    def _():
        o_ref[...] = jnp.zeros_like(o_ref)


def _build_weights(w_in_t, w_fmix, cos_c, sin_c, n_all):
    n_in, d = w_in_t.shape
    ng, gc, _ = w_fmix.shape
    n_copy = (n_in - ng * gc) // gc
    tail_rows = n_in - (ng + n_copy) * gc
    assert 0 < tail_rows < gc and tail_rows % SUBLANES == 0 and ((ng + n_copy) * gc) % tail_rows == 0
    n_tiles = n_all // gc
    return pl.pallas_call(
        functools.partial(_weights_kernel, ng, n_copy),
        grid=(n_tiles,),
        in_specs=[pl.BlockSpec((gc, d), lambda g: (jnp.where(g < 2 * ng, g % ng, jnp.minimum(g - ng, ng + n_copy - 1)), 0)),
                  pl.BlockSpec((tail_rows, d), lambda g: ((ng + n_copy) * gc // tail_rows, 0)),
                  pl.BlockSpec((1, gc, gc), lambda g: (jnp.where(g < 2 * ng, g % ng, 0), 0, 0)),
                  pl.BlockSpec((gc, gc), lambda g: (0, 0)),
                  pl.BlockSpec((gc, gc), lambda g: (0, 0))],
        out_specs=pl.BlockSpec((d, gc), lambda g: (0, g)),
        out_shape=jax.ShapeDtypeStruct((d, n_tiles * gc), BF16),
        compiler_params=_cparams(("arbitrary",)),
        name="inproj_weights",
    )(w_in_t, w_in_t, w_fmix, cos_c, sin_c)


def _inproj_kernel(x_ref, g_ref, sh_ref, sc_ref, w_ref, o_ref, h_ref):
    @pl.when(pl.program_id(2) == 0)
    def _():
        slab = min(LANE, h_ref.shape[0])

        def body(r, carry):
            rows = pl.ds(pl.multiple_of(r * slab, slab), slab)
            h = _rms(x_ref[0, rows, :], g_ref[...]) * (1.0 + sc_ref[0]) + sh_ref[0]
            h_ref[rows, :] = h.astype(BF16)
            return carry

        lax.fori_loop(0, h_ref.shape[0] // slab, body, 0)

    o_ref[0] = jnp.dot(h_ref[...], w_ref[...], preferred_element_type=F32)


def _inproj(xin, g, shift, scale, w_all, col0, ncols, tm, tn):
    b, l, d = xin.shape
    jb = col0 // tn
    return pl.pallas_call(
        _inproj_kernel,
        grid=(b, l // tm, ncols // tn),
        in_specs=[pl.BlockSpec((1, tm, d), lambda bi, i, j: (bi, i, 0)),
                  pl.BlockSpec((1, d), lambda bi, i, j: (0, 0)),
                  pl.BlockSpec((1, 1, d), lambda bi, i, j: (bi, 0, 0)),
                  pl.BlockSpec((1, 1, d), lambda bi, i, j: (bi, 0, 0)),
                  pl.BlockSpec((d, tn), lambda bi, i, j: (0, j + jb))],
        out_specs=pl.BlockSpec((1, tm, tn), lambda bi, i, j: (bi, i, j)),
        out_shape=jax.ShapeDtypeStruct((b, l, ncols), F32),
        scratch_shapes=[pltpu.VMEM((tm, d), BF16)],
        compiler_params=_cparams(("arbitrary", "arbitrary", "arbitrary")),
        name="inproj",
    )(xin, g.reshape(1, d), shift, scale, w_all)


def _seqdft_kernel(a_ref, x_ref, o_ref, acc_ref):
    k = pl.program_id(1)

    @pl.when(k == 0)
    def _():
        acc_ref[...] = jnp.zeros_like(acc_ref)

    acc_ref[...] += jnp.dot(a_ref[...], x_ref[0].astype(BF16), preferred_element_type=F32)

    @pl.when(k == pl.num_programs(1) - 1)
    def _():
        o_ref[0] = acc_ref[...].astype(BF16)


def _seqdft(dft, proj, n_f, tk=1024):
    b, l, _ = proj.shape
    tk = min(tk, l)
    kl = l // tk
    return pl.pallas_call(
        _seqdft_kernel,
        grid=(b, 2 * kl),
        in_specs=[pl.BlockSpec((l, tk), lambda bi, k: (0, k)),
                  pl.BlockSpec((1, tk, n_f), lambda bi, k: (bi, k % kl, k // kl))],
        out_specs=pl.BlockSpec((1, l, n_f), lambda bi, k: (bi, 0, 0)),
        out_shape=jax.ShapeDtypeStruct((b, l, n_f), BF16),
        scratch_shapes=[pltpu.VMEM((l, n_f), F32)],
        compiler_params=_cparams(("arbitrary", "arbitrary")),
        name="seq_dft",
    )(dft, proj)


def _conv_kernel(u_ref, w_ref, b_ref, o_ref):
    u = u_ref[0]
    l = u.shape[0]
    row = lax.broadcasted_iota(jnp.int32, u.shape, 0)
    prev = jnp.where(row == 0, 0.0, pltpu.roll(u, 1, 0))
    nxt = jnp.where(row == l - 1, 0.0, pltpu.roll(u, l - 1, 0))
    w = w_ref[...]
    o_ref[0] = _silu(prev * w[0:1] + u * w[1:2] + nxt * w[2:3] + b_ref[...])


def _conv_silu(proj, col0, conv_w, conv_b, tc=512):
    b, l, _ = proj.shape
    n = conv_w.shape[1]
    jb = col0 // tc
    return pl.pallas_call(
        _conv_kernel,
        grid=(b, n // tc),
        in_specs=[pl.BlockSpec((1, l, tc), lambda bi, j: (bi, 0, j + jb)),
                  pl.BlockSpec((3, tc), lambda bi, j: (0, j)),
                  pl.BlockSpec((1, tc), lambda bi, j: (0, j))],
        out_specs=pl.BlockSpec((1, l, tc), lambda bi, j: (bi, 0, j)),
        out_shape=jax.ShapeDtypeStruct((b, l, n), F32),
        compiler_params=_cparams(("arbitrary", "arbitrary")),
        name="conv_silu",
    )(proj, conv_w, conv_b.reshape(1, n))


def _ssd_direction(reverse, dt_raw, bias, a_neg, xs_ref, bm_ref, cm_ref, st_ref, y_ref, n_heads):
    hp = lax.Precision.HIGHEST
    t = SSM_CHUNK
    p2 = 2 * SSM_HEAD_DIM
    col0 = n_heads if reverse else 0
    r = lax.broadcasted_iota(jnp.int32, (t, t), 0)
    c = lax.broadcasted_iota(jnp.int32, (t, t), 1)
    mask = (r <= c) if reverse else (r >= c)
    tri = mask.astype(F32)
    tri_t = ((r >= c) if reverse else (r <= c)).astype(F32)
    first_y = lax.broadcasted_iota(jnp.int32, (t, p2), 1) < SSM_HEAD_DIM
    first_s = lax.broadcasted_iota(jnp.int32, (SSM_STATE, p2), 1) < SSM_HEAD_DIM

    z = dt_raw + bias
    dtv = jnp.maximum(z, 0.0) + jnp.log(1.0 + jnp.exp(-jnp.abs(z)))
    a = dtv * a_neg
    a_t = a.T
    dtv_t = dtv.T
    cs = jnp.dot(tri, a, precision=hp, preferred_element_type=F32)
    cs_t = jnp.dot(a_t, tri_t, precision=hp, preferred_element_type=F32)
    w_t = dtv_t * jnp.exp(jnp.sum(a_t, axis=1, keepdims=True) - cs_t)
    etot = jnp.exp(jnp.sum(a, axis=0, keepdims=True))

    ppg = n_heads // SSM_GROUPS // 2
    for g in range(SSM_GROUPS):
        bg = bm_ref[0, :, g * SSM_STATE:(g + 1) * SSM_STATE]
        cg16 = cm_ref[0, :, g * SSM_STATE:(g + 1) * SSM_STATE].astype(BF16)
        gmat = _dot_nt(cg16, bg.astype(BF16))
        bgt = bg.T
        if y_ref is not None:
            st_g16 = jnp.concatenate([st_ref[g * ppg + k] for k in range(ppg)], axis=1).astype(BF16)
            yoff_g = jnp.dot(cg16, st_g16, preferred_element_type=F32)
        for k in range(ppg):
            pi = g * ppg + k
            xs16 = xs_ref[0, :, pi * p2:(pi + 1) * p2].astype(BF16)
            ys, ss, ecols, etots = [], [], [], []
            for e in range(2):
                j = col0 + 2 * pi + e
                col_b = jnp.broadcast_to(cs[:, j:j + 1], (t, t))
                lmat = jnp.exp(jnp.where(mask, col_b - cs_t[j:j + 1, :], -jnp.inf))
                m16 = (gmat * lmat * dtv_t[j:j + 1, :]).astype(BF16)
                if y_ref is not None:
                    ys.append(jnp.dot(m16, xs16, preferred_element_type=F32))
                    ecols.append(jnp.exp(col_b))
                bw16 = (bgt * w_t[j:j + 1, :]).astype(BF16)
                ss.append(jnp.dot(bw16, xs16, preferred_element_type=F32))
                etots.append(etot[:, j:j + 1])
            if y_ref is not None:
                y_off = yoff_g[:, k * p2:(k + 1) * p2] * jnp.where(first_y, ecols[0], ecols[1])
                y_ref[0, :, pi * p2:(pi + 1) * p2] = jnp.where(first_y, ys[0], ys[1]) + y_off
            st_ref[pi] = (jnp.where(first_s, etots[0], etots[1]) * st_ref[pi]
                          + jnp.where(first_s, ss[0], ss[1]))


def _ssd_kernel(n_heads, emit_y, *refs):
    (dtf_ref, dtb_ref, bias_ref, a_ref, xsf_ref, xsb_ref, bmf_ref, bmb_ref, cmf_ref, cmb_ref,
     init_ref) = refs[:11]
    if emit_y:
        yf_ref, yb_ref, fin_ref, stf_ref, stb_ref = refs[11:]
    else:
        yf_ref = yb_ref = None
        fin_ref, stf_ref, stb_ref = refs[11:]
    ci = pl.program_id(1)

    @pl.when(ci == 0)
    def _():
        stf_ref[...] = init_ref[0, 0]
        stb_ref[...] = init_ref[0, 1]

    bias = bias_ref[...]
    a_neg = a_ref[...]
    _ssd_direction(False, dtf_ref[0], bias, a_neg, xsf_ref, bmf_ref, cmf_ref, stf_ref, yf_ref, n_heads)
    _ssd_direction(True, dtb_ref[0], bias, a_neg, xsb_ref, bmb_ref, cmb_ref, stb_ref, yb_ref, n_heads)

    @pl.when(ci == pl.num_programs(1) - 1)
    def _():
        fin_ref[0, 0] = stf_ref[...]
        fin_ref[0, 1] = stb_ref[...]


def _ssd(proj, dt_col0, xbc, bias_row, a_row, init, emit_y):
    b, l, _ = proj.shape
    d_ssm = xbc.shape[2] - 2 * SSM_GROUPS * SSM_STATE
    d_bc = SSM_GROUPS * SSM_STATE
    n_heads = d_ssm // SSM_HEAD_DIM
    t = SSM_CHUNK
    nc = l // t
    dtb = dt_col0 // LANE
    fwd = lambda bi, c: (bi, c, 0)
    bwd = lambda bi, c: (bi, nc - 1 - c, 0)
    off = lambda f, o: (lambda bi, c: f(bi, c)[:2] + (o,))
    st_shape = (n_heads // 2, SSM_STATE, 2 * SSM_HEAD_DIM)
    in_specs = [
        pl.BlockSpec((1, t, LANE), off(fwd, dtb)), pl.BlockSpec((1, t, LANE), off(bwd, dtb)),
        pl.BlockSpec((1, LANE), lambda bi, c: (0, 0)), pl.BlockSpec((1, LANE), lambda bi, c: (0, 0)),
        pl.BlockSpec((1, t, d_ssm), fwd), pl.BlockSpec((1, t, d_ssm), bwd),
        pl.BlockSpec((1, t, d_bc), off(fwd, d_ssm // d_bc)), pl.BlockSpec((1, t, d_bc), off(bwd, d_ssm // d_bc)),
        pl.BlockSpec((1, t, d_bc), off(fwd, d_ssm // d_bc + 1)), pl.BlockSpec((1, t, d_bc), off(bwd, d_ssm // d_bc + 1)),
        pl.BlockSpec((1, 2) + st_shape, lambda bi, c: (bi, 0, 0, 0, 0)),
    ]
    fin_spec = pl.BlockSpec((1, 2) + st_shape, lambda bi, c: (bi, 0, 0, 0, 0))
    fin_shape = jax.ShapeDtypeStruct((b, 2) + st_shape, F32)
    if emit_y:
        y_shape = jax.ShapeDtypeStruct((b, l, d_ssm), F32)
        out_specs = [pl.BlockSpec((1, t, d_ssm), fwd), pl.BlockSpec((1, t, d_ssm), bwd), fin_spec]
        out_shape = [y_shape, y_shape, fin_shape]
    else:
        out_specs = [fin_spec]
        out_shape = [fin_shape]
    return pl.pallas_call(
        functools.partial(_ssd_kernel, n_heads, emit_y),
        grid=(b, nc),
        in_specs=in_specs,
        out_specs=out_specs,
        out_shape=out_shape,
        scratch_shapes=[pltpu.VMEM(st_shape, F32), pltpu.VMEM(st_shape, F32)],
        compiler_params=_cparams(("arbitrary", "arbitrary")),
        name="ssd_scan_y" if emit_y else "ssd_scan_state",
    )(proj, proj, bias_row, a_row, xbc, xbc, xbc, xbc, xbc, xbc, init)


def _outproj_kernel(of_ref, yf_ref, yb_ref, xs_ref, z_ref, x_ref, dsk_ref, nw_ref, wo_ref, gpost_ref,
                    gm_ref, gpre_ref, cf_ref, sf_ref, wq_ref, x1_ref, h2_ref, q_ref):
    d_f = of_ref.shape[2]
    y = yf_ref[0] + yb_ref[0] + dsk_ref[...] * xs_ref[0]
    y = y * _silu(z_ref[0])
    gw = y.shape[1] // SSM_GROUPS
    nw = nw_ref[...]
    parts = [_rms(y[:, g * gw:(g + 1) * gw], nw[:, g * gw:(g + 1) * gw]).astype(BF16) for g in range(SSM_GROUPS)]
    yx = jnp.dot(of_ref[0], wo_ref[:d_f, :], preferred_element_type=F32)
    for g in range(SSM_GROUPS):
        yx = yx + jnp.dot(parts[g], wo_ref[d_f + g * gw:d_f + (g + 1) * gw, :], preferred_element_type=F32)
    x1 = x_ref[0] + gm_ref[0] * _rms(yx, gpost_ref[...])
    x1_ref[0] = x1
    h2 = (_rms(x1, gpre_ref[...]) * (1.0 + cf_ref[0]) + sf_ref[0]).astype(BF16)
    h2_ref[0] = h2
    q_ref[0] = jnp.dot(h2, wq_ref[...], preferred_element_type=F32)


def _outproj(of, yf, yb, xbc, proj, z_col0, x, dsk, nw, w_out, g_post, gm, g_pre, cf, sf, w_query, tm=256):
    b, l, d = x.shape
    d_f = of.shape[2]
    d_s = yf.shape[2]
    nq = w_query.shape[1]
    zb = z_col0 // d_s
    row = lambda bi, i: (bi, i, 0)
    vec = lambda n: pl.BlockSpec((1, n), lambda bi, i: (0, 0))
    bvec = lambda n: pl.BlockSpec((1, 1, n), lambda bi, i: (bi, 0, 0))
    return pl.pallas_call(
        _outproj_kernel,
        grid=(b, l // tm),
        in_specs=[pl.BlockSpec((1, tm, d_f), row), pl.BlockSpec((1, tm, d_s), row), pl.BlockSpec((1, tm, d_s), row),
                  pl.BlockSpec((1, tm, d_s), row), pl.BlockSpec((1, tm, d_s), lambda bi, i: (bi, i, zb)),
                  pl.BlockSpec((1, tm, d), row), vec(d_s), vec(d_s),
                  pl.BlockSpec((d_f + d_s, d), lambda bi, i: (0, 0)), vec(d), bvec(d), vec(d), bvec(d), bvec(d),
                  pl.BlockSpec((d, nq), lambda bi, i: (0, 0))],
        out_specs=[pl.BlockSpec((1, tm, d), row), pl.BlockSpec((1, tm, d), row), pl.BlockSpec((1, tm, nq), row)],
        out_shape=[jax.ShapeDtypeStruct((b, l, d), F32), jax.ShapeDtypeStruct((b, l, d), BF16),
                   jax.ShapeDtypeStruct((b, l, nq), F32)],
        compiler_params=_cparams(("arbitrary", "arbitrary")),
        name="outproj_prenorm_query",
    )(of, yf, yb, xbc, proj, x, dsk, nw, w_out, g_post, gm, g_pre, cf, sf, w_query)


SUBLANES = 8


def _top16_ranked(s):
    n = s.shape[0]
    iota = lax.broadcasted_iota(jnp.int32, s.shape, 0).astype(F32)
    cur = s
    rank = jnp.full(s.shape, float(PEER_TOPK), F32)
    vals = []
    for k in range(PEER_TOPK):
        m = jnp.max(cur, axis=0, keepdims=True)
        idx = jnp.min(jnp.where(cur == m, iota, float(n)), axis=0, keepdims=True)
        hit = iota == idx
        rank = jnp.where(hit, float(k), rank)
        cur = jnp.where(hit, -jnp.inf, cur)
        vals.append(m)
    return vals, rank


def _candidate_pieces(v1, v2, sub):
    def stack8(vals):
        out = jnp.zeros(sub.shape, F32)
        for k, v in enumerate(vals):
            out = jnp.where(sub == k, v, out)
        return out

    v2_lo, v2_hi, v1_hi = stack8(v2[:8]), stack8(v2[8:]), stack8(v1[8:])
    return [v1[0] + v2_lo, v1[0] + v2_hi] + [v1[a] + v2_lo for a in range(1, 8)] + [v1_hi + v2[0]]


def _row_counts(sels, sub):
    n_rows = [jnp.sum(sels[0] + sels[1], axis=0, keepdims=True)]
    n_rows += [jnp.sum(sels[a + 1], axis=0, keepdims=True) for a in range(1, 8)]
    n_rows += [jnp.sum(jnp.where(sub == r, sels[9], 0.0), axis=0, keepdims=True) for r in range(8)]
    return n_rows


def _route_columns_exact(s1, s2):
    kk = PEER_TOPK
    v1, rank1 = _top16_ranked(s1)
    v2, rank2 = _top16_ranked(s2)
    sub = lax.broadcasted_iota(jnp.int32, (SUBLANES, s1.shape[1]), 0)
    subf = sub.astype(F32)
    cands = _candidate_pieces(v1, v2, sub)
    poses = [subf, subf + 8.0] + [a * kk + subf for a in range(1, 8)] + [(subf + 8.0) * kk]
    sels = [jnp.zeros(sub.shape, F32) for _ in cands]
    m0 = v1[0] + v2[0]
    zsum = jnp.zeros_like(m0)
    for _ in range(kk):
        m = jnp.max(functools.reduce(jnp.maximum, cands), axis=0, keepdims=True)
        firsts = [jnp.where(cd == m, ps, float(kk * kk)) for cd, ps in zip(cands, poses)]
        p = jnp.min(functools.reduce(jnp.minimum, firsts), axis=0, keepdims=True)
        hits = [ps == p for ps in poses]
        sels = [jnp.where(ht, 1.0, sl) for ht, sl in zip(hits, sels)]
        cands = [jnp.where(ht, -jnp.inf, cd) for ht, cd in zip(hits, cands)]
        zsum = zsum + jnp.exp(m - m0)
    n_rows = _row_counts(sels, sub)
    cnt = jnp.zeros(rank1.shape, F32)
    for a in range(kk):
        cnt = jnp.where(rank1 == float(a), n_rows[a], cnt)
    return rank2, cnt, zsum, v1[0], v2[0]


def _sorted_top16(blocks):
    v = list(blocks)
    n = len(v)

    def exchange(i, l, descending):
        hi, lo = jnp.maximum(v[i], v[l]), jnp.minimum(v[i], v[l])
        v[i], v[l] = (hi, lo) if descending else (lo, hi)

    k = 2
    while k <= n:
        j = k // 2
        while j >= 1:
            for i in range(n):
                if i ^ j > i:
                    exchange(i, i ^ j, (i & k) == 0)
            j //= 2
        k *= 2
    for shift in (4, 2, 1):
        w = [pltpu.roll(x, shift, 0) for x in v]
        v = [jnp.maximum(v[r], w[n - 1 - r]) for r in range(n)]
        j = n // 2
        while j >= 1:
            for i in range(n):
                if i ^ j > i:
                    exchange(i, i ^ j, True)
            j //= 2
    return v


def _route_columns_sorted(s1, s2):
    kk = PEER_TOPK
    nb = s1.shape[0] // SUBLANES
    tcols = s1.shape[1]
    blocks = lambda s: [s[SUBLANES * r:SUBLANES * (r + 1), :] for r in range(nb)]
    b1, b2 = blocks(s1), blocks(s2)
    v1, v2 = _sorted_top16(b1), _sorted_top16(b2)
    sub = lax.broadcasted_iota(jnp.int32, (SUBLANES, tcols), 0)
    cands = _candidate_pieces(v1, v2, sub)
    neg = jnp.full((SUBLANES, tcols), -jnp.inf, F32)
    top = _sorted_top16(cands + [neg] * (nb - len(cands)))
    tau = top[kk - 1]
    m0 = v1[0] + v2[0]
    picked = [cd >= tau for cd in cands]
    sels = [jnp.where(pk, 1.0, 0.0) for pk in picked]
    zparts = [jnp.where(pk, jnp.exp(cd - m0), 0.0) for pk, cd in zip(picked, cands)]
    zsum = jnp.sum(functools.reduce(jnp.add, zparts), axis=0, keepdims=True)
    n_rows = _row_counts(sels, sub)

    def rank_in(v, blk):
        gt = lambda pivot: pivot > blk
        c1 = gt(v[7])
        c2 = gt(jnp.where(c1, v[11], v[3]))
        c3 = gt(jnp.where(c1, jnp.where(c2, v[13], v[9]), jnp.where(c2, v[5], v[1])))
        even = [jnp.where(c3, v[4 * q + 2], v[4 * q]) for q in range(4)]
        c4 = gt(jnp.where(c1, jnp.where(c2, even[3], even[2]), jnp.where(c2, even[1], even[0])))
        c5 = gt(v[15])
        bit = lambda c, val: jnp.where(c, val, 0.0)
        return (bit(c1, 8.0) + bit(c2, 4.0)) + (bit(c3, 2.0) + bit(c4, 1.0)) + bit(c5, 1.0)

    rank2_blocks, cnt_blocks = [], []
    for blk1, blk2 in zip(b1, b2):
        rk = rank_in(v2, blk2)
        ct = jnp.zeros(blk1.shape, F32)
        for a in range(kk):
            ct = jnp.where(blk1 == v1[a], n_rows[a], ct)
        rank2_blocks.append(rk)
        cnt_blocks.append(ct)
    rank2 = jnp.concatenate(rank2_blocks, axis=0)
    cnt = jnp.concatenate(cnt_blocks, axis=0)

    count = lambda flags: jnp.sum(functools.reduce(jnp.add, flags), axis=0, keepdims=True)
    strict = lambda v: functools.reduce(jnp.logical_and, [v[k] > v[k + 1] for k in range(kk - 1)])[0:1]
    ok = (strict(v1) & strict(v2) & strict(top)
          & (count([jnp.where(blk >= v1[kk - 1], 1.0, 0.0) for blk in b1]) == float(kk))
          & (count([jnp.where(blk >= v2[kk - 1], 1.0, 0.0) for blk in b2]) == float(kk))
          & (functools.reduce(jnp.add, n_rows) == float(kk)))
    return rank2, cnt, zsum, v1[0][0:1], v2[0][0:1], ok


def _peer_route_kernel(q_ref, k1_ref, k2_ref, rank2_ref, cnt_ref, e1_ref, e2_ref, s1_ref, s2_ref):
    def split(v):
        hi = v.astype(BF16)
        return hi, (v - hi.astype(F32)).astype(BF16)

    def scores(keys, qs):
        k_hi, k_lo = split(keys)
        q_hi, q_lo = split(qs)
        return _dot_nt(k_hi, q_hi) + (_dot_nt(k_hi, q_lo) + _dot_nt(k_lo, q_hi))

    qh = q_ref[...]
    s1_ref[...] = scores(k1_ref[0], qh[:, :PEER_HALF])
    s2_ref[...] = scores(k2_ref[0], qh[:, PEER_HALF:])

    def column(ci, carry):
        cols = pl.ds(pl.multiple_of(ci * LANE, LANE), LANE)
        s1 = s1_ref[:, cols]
        s2 = s2_ref[:, cols]

        def emit(rank2, cnt, zsum, m1, m2):
            rank2_ref[0, :, cols] = rank2.astype(BF16)
            cnt_ref[0, :, cols] = cnt
            e1_ref[0, :, cols] = jnp.exp(s1 - m1)
            e2_ref[0, :, cols] = (jnp.exp(s2 - m2) / zsum).astype(BF16)

        *fast, ok = _route_columns_sorted(s1, s2)
        emit(*fast)
        n_bad = jnp.sum(jnp.where(ok, 0.0, 1.0))

        @pl.when(n_bad > 0.0)
        def _():
            emit(*_route_columns_exact(s1, s2))

        return carry

    lax.fori_loop(0, s1_ref.shape[1] // LANE, column, 0)


def _peer_route(q, k1, k2, tq=1024):
    t, _ = q.shape
    nh, nk, hd = k1.shape
    out = lambda dt: jax.ShapeDtypeStruct((nh, nk, t), dt)
    spec = pl.BlockSpec((1, nk, tq), lambda i, h: (h, 0, i))
    return pl.pallas_call(
        _peer_route_kernel,
        grid=(t // tq, nh),
        in_specs=[pl.BlockSpec((tq, PEER_KEY_DIM), lambda i, h: (i, h)),
                  pl.BlockSpec((1, nk, hd), lambda i, h: (h, 0, 0)),
                  pl.BlockSpec((1, nk, hd), lambda i, h: (h, 0, 0))],
        out_specs=[spec, spec, spec, spec],
        out_shape=[out(BF16), out(F32), out(F32), out(BF16)],
        scratch_shapes=[pltpu.VMEM((nk, tq), F32), pltpu.VMEM((nk, tq), F32)],
        compiler_params=_cparams(("arbitrary", "arbitrary")),
        name="peer_route",
    )(q, k1, k2)


BF16_ROWS = 16


def _gate_weights(w_ref, rank2_ref, e2_ref, cnt_ref, e1_ref):
    nk = PEER_N_KEYS
    tm = w_ref.shape[1]
    for i_loc in range(w_ref.shape[0] // nk):
        w = None
        for h in range(PEER_HEADS):
            cnt_b = jnp.broadcast_to(cnt_ref[h, i_loc:i_loc + 1, :], (BF16_ROWS, tm)).astype(BF16)
            e1_b = jnp.broadcast_to(e1_ref[h, i_loc:i_loc + 1, :], (BF16_ROWS, tm)).astype(BF16)
            wh = jnp.where(rank2_ref[h] < cnt_b[None], e2_ref[h], jnp.zeros((), BF16)) * e1_b[None]
            w = wh if w is None else w + wh
        w_ref[i_loc * nk:(i_loc + 1) * nk, :] = w.reshape(nk, tm)


def _peer_dense_kernel(n_sub, h2_ref, u_ref, vt_ref, rank2_ref, e2_ref, cnt_ref, e1_ref,
                       x1_ref, gf_ref, gpost_ref, o_ref, acc_ref, h2t_ref, w_ref, *g_refs):
    j = pl.program_id(1)

    @pl.when(j == 0)
    def _():
        acc_ref[...] = jnp.zeros_like(acc_ref)
        h2t_ref[...] = h2_ref[...].T

    nk = PEER_N_KEYS
    ts = u_ref.shape[0] // n_sub
    ats = [jnp.dot(u_ref[s * ts:(s + 1) * ts, :], h2t_ref[...], preferred_element_type=F32)
           for s in range(n_sub)]
    _gate_weights(w_ref, rank2_ref, e2_ref, cnt_ref, e1_ref)
    for s in range(n_sub):
        at = ats[s]
        for il in range(ts // nk):
            rows = slice(s * ts + il * nk, s * ts + (il + 1) * nk)
            a = at[il * nk:(il + 1) * nk, :]
            gelu = 0.5 * a * (1.0 + lax.erf(a * (1.0 / math.sqrt(2.0))))
            g_refs[s][il * nk:(il + 1) * nk, :] = gelu.astype(BF16) * w_ref[rows, :]
        acc_ref[...] += jnp.dot(vt_ref[:, s * ts:(s + 1) * ts], g_refs[s][...], preferred_element_type=F32)

    @pl.when(j == pl.num_programs(1) - 1)
    def _():
        y = acc_ref[...].T
        o_ref[...] = x1_ref[...] + gf_ref[0] * _rms(y, gpost_ref[...])


def _peer_dense(h2, u16, vt16, rank2, cnt, e1, e2, x1, gf, g_post, seq_len, tm=512, ts=512, n_sub=2):
    t, d = h2.shape
    ne = u16.shape[0]
    nh, nk, _ = cnt.shape
    te = ts * n_sub
    i_per_step = te // nk
    n_steps = ne // te
    aux_j = pl.BlockSpec((nh, nk // BF16_ROWS, BF16_ROWS, tm), lambda i, j: (0, 0, 0, i))
    aux_i = pl.BlockSpec((nh, i_per_step, tm), lambda i, j: (0, j, i))
    blocks_per_seq = seq_len // tm
    return pl.pallas_call(
        functools.partial(_peer_dense_kernel, n_sub),
        grid=(t // tm, n_steps),
        in_specs=[pl.BlockSpec((tm, d), lambda i, j: (i, 0)),
                  pl.BlockSpec((te, d), lambda i, j: (j, 0)),
                  pl.BlockSpec((d, te), lambda i, j: (0, j)),
                  aux_j, aux_j, aux_i, aux_i,
                  pl.BlockSpec((tm, d), lambda i, j: (i, 0), pipeline_mode=pl.Buffered(1)),
                  pl.BlockSpec((1, 1, d), lambda i, j: (i // blocks_per_seq, 0, 0)),
                  pl.BlockSpec((1, d), lambda i, j: (0, 0))],
        out_specs=pl.BlockSpec((tm, d), lambda i, j: (i, 0)),
        out_shape=jax.ShapeDtypeStruct((t, d), F32),
        scratch_shapes=[pltpu.VMEM((d, tm), F32), pltpu.VMEM((d, tm), BF16), pltpu.VMEM((te, tm), BF16)]
        + [pltpu.VMEM((ts, tm), BF16) for _ in range(n_sub)],
        compiler_params=_cparams(("arbitrary", "arbitrary")),
        name="peer_dense",
    )(h2, u16, vt16, rank2, e2, cnt, e1, x1, gf, g_post)


def _dft_tables(n):
    k = np.arange(n, dtype=np.int64)
    ph = (np.outer(k, k) % n).astype(np.float64) * (2.0 * np.pi / n)
    return np.cos(ph), np.sin(ph)


def kernel(x, c, ctx, c_ctx, w_mod, b_mod, g_pre_mix, g_post_mix, g_pre_ffn, g_post_ffn, w_in, w_fmix, conv_w, conv_b, dt_bias_f, dt_bias_b, a_log_f, a_log_b, d_skip_f, d_skip_b, ssm_norm_w, w_out, w_query, sub_keys_1, sub_keys_2, expert_u, expert_v):
    bsz, seq, d = x.shape
    ctx_len = ctx.shape[1]
    layer = 0
    n_heads = dt_bias_f.shape[1]
    d_ssm = n_heads * SSM_HEAD_DIM
    d_f = d - d_ssm
    gc = d_f // N_FOURIER_GROUPS
    d_xbc = conv_w.shape[2]
    assert w_mod.shape[0] == 1, "single-layer kernel"

    cond = jnp.zeros((8, d), F32).at[:bsz].set(c).at[bsz].set(c_ctx)
    mod = _adaln(cond, w_mod[layer], b_mod[layer])
    mods = [mod[:, i * d:(i + 1) * d] for i in range(N_MOD)]
    sm_x, cm_x, gm_x, sf_x, cf_x, gf_x = [m[:bsz, None, :] for m in mods]
    sm_c = jnp.broadcast_to(mods[0][bsz][None, None, :], (bsz, 1, d))
    cm_c = jnp.broadcast_to(mods[1][bsz][None, None, :], (bsz, 1, d))

    cos_l, sin_l = _dft_tables(seq)
    cos_c, sin_c = _dft_tables(gc)
    scale = 1.0 / math.sqrt(seq * gc)
    dt_pad = 512
    col_z = 2 * d_f
    col_xbc = col_z + d_ssm
    col_dt = col_xbc + d_xbc
    n_all = col_dt + dt_pad
    assert (d_ssm + d_xbc) % gc == 0 and dt_pad % gc == 0, "z/xBC columns must fill whole weight tiles"
    w_all = _build_weights(w_in[layer].T, w_fmix[layer],
                           jnp.asarray(cos_c * scale, F32), jnp.asarray(sin_c * scale, F32), n_all)
    dft = jnp.asarray(np.concatenate([cos_l, -sin_l], axis=1), F32).astype(BF16)

    pad32 = lambda f, b_: jnp.zeros((1, LANE), F32).at[0, :n_heads].set(f).at[0, n_heads:2 * n_heads].set(b_)
    bias_row = pad32(dt_bias_f[layer], dt_bias_b[layer])
    a_row = pad32(-jnp.exp(a_log_f[layer]), -jnp.exp(a_log_b[layer]))

    proj_c = _inproj(ctx, g_pre_mix[layer], sm_c, cm_c, w_all, col_xbc, n_all - col_xbc, tm=ctx_len, tn=512)
    xbc_c = _conv_silu(proj_c, 0, conv_w[layer], conv_b[layer])
    zero_state = jnp.zeros((bsz, 2, n_heads // 2, SSM_STATE, 2 * SSM_HEAD_DIM), F32)
    (fin_c,) = _ssd(proj_c, d_xbc, xbc_c, bias_row, a_row, zero_state, emit_y=False)

    proj_x = _inproj(x, g_pre_mix[layer], sm_x, cm_x, w_all, 0, n_all, tm=min(1024, seq), tn=1024)
    o_f = _seqdft(dft, proj_x, d_f)
    xbc_x = _conv_silu(proj_x, col_xbc, conv_w[layer], conv_b[layer])
    y_f, y_b, _ = _ssd(proj_x, col_dt, xbc_x, bias_row, a_row, fin_c, emit_y=True)

    dsk = jnp.repeat(d_skip_f[layer] + d_skip_b[layer], SSM_HEAD_DIM)[None, :]
    x1, h2, q = _outproj(o_f, y_f, y_b, xbc_x, proj_x, col_z, x, dsk, ssm_norm_w[layer][None, :],
                         w_out[layer].astype(BF16), g_post_mix[layer][None, :], gm_x,
                         g_pre_ffn[layer][None, :], cf_x, sf_x, w_query[layer].astype(BF16))

    t = bsz * seq
    rank2, cnt, e1, e2 = _peer_route(q.reshape(t, -1), sub_keys_1[layer], sub_keys_2[layer],
                                     tq=min(2048, t))
    tiles = lambda a: a.reshape(a.shape[0], a.shape[1] // BF16_ROWS, BF16_ROWS, t)
    u16 = expert_u[layer].astype(BF16)
    vt16 = expert_v[layer].T.astype(BF16)
    out = _peer_dense(h2.reshape(t, d), u16, vt16, tiles(rank2), cnt, e1, tiles(e2), x1.reshape(t, d), gf_x,
                      g_post_ffn[layer][None, :], seq)
    return out.reshape(bsz, seq, d)
```

```python
import functools
import math

import jax
import jax.numpy as jnp
import numpy as np
from jax import lax
from jax.experimental import pallas as pl
from jax.experimental.pallas import tpu as pltpu

F32 = jnp.float32
BF16 = jnp.bfloat16

EPS = 1e-6
N_MOD = 6
N_FOURIER_GROUPS = 4
SSM_HEAD_DIM = 64
SSM_GROUPS = 2
SSM_STATE = 128
SSM_CHUNK = 128
PEER_HEADS = 8
PEER_TOPK = 16
PEER_N_KEYS = 128
PEER_KEY_DIM = 128
PEER_HALF = PEER_KEY_DIM // 2

LANE = 128
VMEM_LIMIT = 56 * 1024 * 1024


def _cparams(sem):
    return pltpu.CompilerParams(dimension_semantics=sem, vmem_limit_bytes=VMEM_LIMIT)


def _bdot(a, b):
    return jnp.dot(a.astype(BF16), b.astype(BF16), preferred_element_type=F32)


def _dot_nt(a, b, precision=None):
    return lax.dot_general(a, b, (((1,), (1,)), ((), ())), precision=precision,
                           preferred_element_type=F32)


def _rms(u, g):
    return u * lax.rsqrt(jnp.mean(u * u, axis=-1, keepdims=True) + EPS) * g


def _silu(u):
    return u * (1.0 / (1.0 + jnp.exp(-u)))


def _adaln_kernel(c_ref, w_ref, b_ref, o_ref):
    a = _silu(c_ref[...])
    o_ref[...] = _bdot(a, w_ref[...]) + b_ref[...]


def _adaln(cond, w_mod, b_mod, tn=1024):
    rows, d = cond.shape
    n = w_mod.shape[1]
    return pl.pallas_call(
        _adaln_kernel,
        grid=(n // tn,),
        in_specs=[pl.BlockSpec((rows, d), lambda j: (0, 0)),
                  pl.BlockSpec((d, tn), lambda j: (0, j)),
                  pl.BlockSpec((1, tn), lambda j: (0, j))],
        out_specs=pl.BlockSpec((rows, tn), lambda j: (0, j)),
        out_shape=jax.ShapeDtypeStruct((rows, n), F32),
        compiler_params=_cparams(("arbitrary",)),
        name="adaln",
    )(cond, w_mod, b_mod.reshape(1, n))


def _weights_kernel(ng, n_copy, wt_ref, tail_ref, fm_ref, cc_ref, sc_ref, o_ref):
    hp = lax.Precision.HIGHEST
    g = pl.program_id(0)

    def fold(tbl_ref):
        m = jnp.dot(tbl_ref[...], fm_ref[0], precision=hp, preferred_element_type=F32)
        o_ref[...] = jnp.dot(wt_ref[...].T, m, precision=hp, preferred_element_type=F32).astype(BF16)

    pl.when(g < ng)(lambda: fold(cc_ref))
    pl.when((g >= ng) & (g < 2 * ng))(lambda: fold(sc_ref))

    @pl.when((g >= 2 * ng) & (g < 2 * ng + n_copy))
    def _():
        o_ref[...] = wt_ref[...].T.astype(BF16)

    @pl.when(g == 2 * ng + n_copy)
    def _():
        tail = tail_ref[...]
        fill = jnp.zeros((o_ref.shape[1] - tail.shape[0], tail.shape[1]), F32)
        o_ref[...] = jnp.concatenate([tail, fill], axis=0).T.astype(BF16)

    @pl.when(g > 2 * ng + n_copy)
    def _():
        o_ref[...] = jnp.zeros_like(o_ref)


def _build_weights(w_in_t, w_fmix, cos_c, sin_c, n_all):
    n_in, d = w_in_t.shape
    ng, gc, _ = w_fmix.shape
    n_copy = (n_in - ng * gc) // gc
    tail_rows = n_in - (ng + n_copy) * gc
    assert 0 < tail_rows < gc and tail_rows % SUBLANES == 0 and ((ng + n_copy) * gc) % tail_rows == 0
    n_tiles = n_all // gc
    return pl.pallas_call(
        functools.partial(_weights_kernel, ng, n_copy),
        grid=(n_tiles,),
        in_specs=[pl.BlockSpec((gc, d), lambda g: (jnp.where(g < 2 * ng, g % ng, jnp.minimum(g - ng, ng + n_copy - 1)), 0)),
                  pl.BlockSpec((tail_rows, d), lambda g: ((ng + n_copy) * gc // tail_rows, 0)),
                  pl.BlockSpec((1, gc, gc), lambda g: (jnp.where(g < 2 * ng, g % ng, 0), 0, 0)),
                  pl.BlockSpec((gc, gc), lambda g: (0, 0)),
                  pl.BlockSpec((gc, gc), lambda g: (0, 0))],
        out_specs=pl.BlockSpec((d, gc), lambda g: (0, g)),
        out_shape=jax.ShapeDtypeStruct((d, n_tiles * gc), BF16),
        compiler_params=_cparams(("arbitrary",)),
        name="inproj_weights",
    )(w_in_t, w_in_t, w_fmix, cos_c, sin_c)


def _inproj_kernel(x_ref, g_ref, sh_ref, sc_ref, w_ref, o_ref, h_ref=None):
    modulated = lambda rows: (_rms(x_ref[0, rows, :], g_ref[...]) * (1.0 + sc_ref[0]) + sh_ref[0]).astype(BF16)
    if h_ref is None:
        o_ref[0] = jnp.dot(modulated(slice(None)), w_ref[...], preferred_element_type=F32)
        return

    @pl.when(pl.program_id(2) == 0)
    def _():
        slab = min(LANE, h_ref.shape[0])

        def body(r, carry):
            rows = pl.ds(pl.multiple_of(r * slab, slab), slab)
            h_ref[rows, :] = modulated(rows)
            return carry

        lax.fori_loop(0, h_ref.shape[0] // slab, body, 0)

    o_ref[0] = jnp.dot(h_ref[...], w_ref[...], preferred_element_type=F32)


def _inproj(xin, g, shift, scale, w_all, col0, ncols, tm, tn):
    b, l, d = xin.shape
    jb = col0 // tn
    w_mode = dict(pipeline_mode=pl.Buffered(1)) if ncols == tn else {}
    return pl.pallas_call(
        _inproj_kernel,
        grid=(b, l // tm, ncols // tn),
        in_specs=[pl.BlockSpec((1, tm, d), lambda bi, i, j: (bi, i, 0)),
                  pl.BlockSpec((1, d), lambda bi, i, j: (0, 0)),
                  pl.BlockSpec((1, 1, d), lambda bi, i, j: (bi, 0, 0)),
                  pl.BlockSpec((1, 1, d), lambda bi, i, j: (bi, 0, 0)),
                  pl.BlockSpec((d, tn), lambda bi, i, j: (0, j + jb), **w_mode)],
        out_specs=pl.BlockSpec((1, tm, tn), lambda bi, i, j: (bi, i, j)),
        out_shape=jax.ShapeDtypeStruct((b, l, ncols), F32),
        scratch_shapes=[] if ncols == tn else [pltpu.VMEM((tm, d), BF16)],
        compiler_params=_cparams(("arbitrary", "arbitrary", "arbitrary")),
        name="inproj",
    )(xin, g.reshape(1, d), shift, scale, w_all)


def _seqdft_kernel(a_ref, x_ref, o_ref, acc_ref):
    k = pl.program_id(1)

    @pl.when(k == 0)
    def _():
        acc_ref[...] = jnp.zeros_like(acc_ref)

    acc_ref[...] += jnp.dot(a_ref[...], x_ref[0].astype(BF16), preferred_element_type=F32)

    @pl.when(k == pl.num_programs(1) - 1)
    def _():
        o_ref[0] = acc_ref[...].astype(BF16)


def _seqdft(dft, proj, n_f, tk=1024):
    b, l, _ = proj.shape
    tk = min(tk, l)
    kl = l // tk
    return pl.pallas_call(
        _seqdft_kernel,
        grid=(b, 2 * kl),
        in_specs=[pl.BlockSpec((l, tk), lambda bi, k: (0, k)),
                  pl.BlockSpec((1, tk, n_f), lambda bi, k: (bi, k % kl, k // kl))],
        out_specs=pl.BlockSpec((1, l, n_f), lambda bi, k: (bi, 0, 0)),
        out_shape=jax.ShapeDtypeStruct((b, l, n_f), BF16),
        scratch_shapes=[pltpu.VMEM((l, n_f), F32)],
        compiler_params=_cparams(("arbitrary", "arbitrary")),
        name="seq_dft",
    )(dft, proj)


def _conv_kernel(u_ref, w_ref, b_ref, o_ref):
    u = u_ref[0]
    l = u.shape[0]
    row = lax.broadcasted_iota(jnp.int32, u.shape, 0)
    prev = jnp.where(row == 0, 0.0, pltpu.roll(u, 1, 0))
    nxt = jnp.where(row == l - 1, 0.0, pltpu.roll(u, l - 1, 0))
    w = w_ref[...]
    o_ref[0] = _silu(prev * w[0:1] + u * w[1:2] + nxt * w[2:3] + b_ref[...])


def _conv_silu(proj, col0, conv_w, conv_b, tc=512):
    b, l, _ = proj.shape
    n = conv_w.shape[1]
    jb = col0 // tc
    return pl.pallas_call(
        _conv_kernel,
        grid=(b, n // tc),
        in_specs=[pl.BlockSpec((1, l, tc), lambda bi, j: (bi, 0, j + jb)),
                  pl.BlockSpec((3, tc), lambda bi, j: (0, j)),
                  pl.BlockSpec((1, tc), lambda bi, j: (0, j))],
        out_specs=pl.BlockSpec((1, l, tc), lambda bi, j: (bi, 0, j)),
        out_shape=jax.ShapeDtypeStruct((b, l, n), F32),
        compiler_params=_cparams(("arbitrary", "arbitrary")),
        name="conv_silu",
    )(proj, conv_w, conv_b.reshape(1, n))


def _ssd_direction(reverse, dt_raw, bias, a_neg, xs_ref, bm_ref, cm_ref, st_ref, y_ref, n_heads):
    hp = lax.Precision.HIGHEST
    t = SSM_CHUNK
    p2 = 2 * SSM_HEAD_DIM
    col0 = n_heads if reverse else 0
    r = lax.broadcasted_iota(jnp.int32, (t, t), 0)
    c = lax.broadcasted_iota(jnp.int32, (t, t), 1)
    mask = (r <= c) if reverse else (r >= c)
    tri = mask.astype(F32)
    tri_t = ((r >= c) if reverse else (r <= c)).astype(F32)
    first_y = lax.broadcasted_iota(jnp.int32, (t, p2), 1) < SSM_HEAD_DIM
    first_s = lax.broadcasted_iota(jnp.int32, (SSM_STATE, p2), 1) < SSM_HEAD_DIM

    z = dt_raw + bias
    dtv = jnp.maximum(z, 0.0) + jnp.log(1.0 + jnp.exp(-jnp.abs(z)))
    a = dtv * a_neg
    a_t = a.T
    dtv_t = dtv.T
    cs = jnp.dot(tri, a, precision=hp, preferred_element_type=F32)
    cs_t = jnp.dot(a_t, tri_t, precision=hp, preferred_element_type=F32)
    w_t = dtv_t * jnp.exp(jnp.sum(a_t, axis=1, keepdims=True) - cs_t)
    etot = jnp.exp(jnp.sum(a, axis=0, keepdims=True))

    ppg = n_heads // SSM_GROUPS // 2
    for g in range(SSM_GROUPS):
        bg = bm_ref[0, :, g * SSM_STATE:(g + 1) * SSM_STATE]
        cg16 = cm_ref[0, :, g * SSM_STATE:(g + 1) * SSM_STATE].astype(BF16)
        gmat = _dot_nt(cg16, bg.astype(BF16))
        bgt = bg.T
        if y_ref is not None:
            st_g16 = jnp.concatenate([st_ref[g * ppg + k] for k in range(ppg)], axis=1).astype(BF16)
            yoff_g = jnp.dot(cg16, st_g16, preferred_element_type=F32)
        for k in range(ppg):
            pi = g * ppg + k
            xs16 = xs_ref[0, :, pi * p2:(pi + 1) * p2].astype(BF16)
            ys, ss, ecols, etots = [], [], [], []
            for e in range(2):
                j = col0 + 2 * pi + e
                col_b = jnp.broadcast_to(cs[:, j:j + 1], (t, t))
                lmat = jnp.exp(jnp.where(mask, col_b - cs_t[j:j + 1, :], -jnp.inf))
                m16 = (gmat * lmat * dtv_t[j:j + 1, :]).astype(BF16)
                if y_ref is not None:
                    ys.append(jnp.dot(m16, xs16, preferred_element_type=F32))
                    ecols.append(jnp.exp(col_b))
                bw16 = (bgt * w_t[j:j + 1, :]).astype(BF16)
                ss.append(jnp.dot(bw16, xs16, preferred_element_type=F32))
                etots.append(etot[:, j:j + 1])
            if y_ref is not None:
                y_off = yoff_g[:, k * p2:(k + 1) * p2] * jnp.where(first_y, ecols[0], ecols[1])
                y_ref[0, :, pi * p2:(pi + 1) * p2] = jnp.where(first_y, ys[0], ys[1]) + y_off
            st_ref[pi] = (jnp.where(first_s, etots[0], etots[1]) * st_ref[pi]
                          + jnp.where(first_s, ss[0], ss[1]))


def _ssd_kernel(n_heads, emit_y, *refs):
    (dtf_ref, dtb_ref, bias_ref, a_ref, xsf_ref, xsb_ref, bmf_ref, bmb_ref, cmf_ref, cmb_ref,
     init_ref) = refs[:11]
    if emit_y:
        yf_ref, yb_ref, fin_ref, stf_ref, stb_ref = refs[11:]
    else:
        yf_ref = yb_ref = None
        fin_ref, stf_ref, stb_ref = refs[11:]
    ci = pl.program_id(1)

    @pl.when(ci == 0)
    def _():
        stf_ref[...] = init_ref[0, 0]
        stb_ref[...] = init_ref[0, 1]

    bias = bias_ref[...]
    a_neg = a_ref[...]
    _ssd_direction(False, dtf_ref[0], bias, a_neg, xsf_ref, bmf_ref, cmf_ref, stf_ref, yf_ref, n_heads)
    _ssd_direction(True, dtb_ref[0], bias, a_neg, xsb_ref, bmb_ref, cmb_ref, stb_ref, yb_ref, n_heads)

    @pl.when(ci == pl.num_programs(1) - 1)
    def _():
        fin_ref[0, 0] = stf_ref[...]
        fin_ref[0, 1] = stb_ref[...]


def _ssd(proj, dt_col0, xbc, bias_row, a_row, init, emit_y):
    b, l, _ = proj.shape
    d_ssm = xbc.shape[2] - 2 * SSM_GROUPS * SSM_STATE
    d_bc = SSM_GROUPS * SSM_STATE
    n_heads = d_ssm // SSM_HEAD_DIM
    t = SSM_CHUNK
    nc = l // t
    dtb = dt_col0 // LANE
    fwd = lambda bi, c: (bi, c, 0)
    bwd = lambda bi, c: (bi, nc - 1 - c, 0)
    off = lambda f, o: (lambda bi, c: f(bi, c)[:2] + (o,))
    st_shape = (n_heads // 2, SSM_STATE, 2 * SSM_HEAD_DIM)
    in_specs = [
        pl.BlockSpec((1, t, LANE), off(fwd, dtb)), pl.BlockSpec((1, t, LANE), off(bwd, dtb)),
        pl.BlockSpec((1, LANE), lambda bi, c: (0, 0)), pl.BlockSpec((1, LANE), lambda bi, c: (0, 0)),
        pl.BlockSpec((1, t, d_ssm), fwd), pl.BlockSpec((1, t, d_ssm), bwd),
        pl.BlockSpec((1, t, d_bc), off(fwd, d_ssm // d_bc)), pl.BlockSpec((1, t, d_bc), off(bwd, d_ssm // d_bc)),
        pl.BlockSpec((1, t, d_bc), off(fwd, d_ssm // d_bc + 1)), pl.BlockSpec((1, t, d_bc), off(bwd, d_ssm // d_bc + 1)),
        pl.BlockSpec((1, 2) + st_shape, lambda bi, c: (bi, 0, 0, 0, 0)),
    ]
    fin_spec = pl.BlockSpec((1, 2) + st_shape, lambda bi, c: (bi, 0, 0, 0, 0))
    fin_shape = jax.ShapeDtypeStruct((b, 2) + st_shape, F32)
    if emit_y:
        y_shape = jax.ShapeDtypeStruct((b, l, d_ssm), F32)
        out_specs = [pl.BlockSpec((1, t, d_ssm), fwd), pl.BlockSpec((1, t, d_ssm), bwd), fin_spec]
        out_shape = [y_shape, y_shape, fin_shape]
    else:
        out_specs = [fin_spec]
        out_shape = [fin_shape]
    return pl.pallas_call(
        functools.partial(_ssd_kernel, n_heads, emit_y),
        grid=(b, nc),
        in_specs=in_specs,
        out_specs=out_specs,
        out_shape=out_shape,
        scratch_shapes=[pltpu.VMEM(st_shape, F32), pltpu.VMEM(st_shape, F32)],
        compiler_params=_cparams(("arbitrary", "arbitrary")),
        name="ssd_scan_y" if emit_y else "ssd_scan_state",
    )(proj, proj, bias_row, a_row, xbc, xbc, xbc, xbc, xbc, xbc, init)


def _outproj_kernel(of_ref, yf_ref, yb_ref, xs_ref, z_ref, x_ref, dsk_ref, nw_ref, wo_ref, gpost_ref,
                    gm_ref, gpre_ref, cf_ref, sf_ref, wq_ref, x1_ref, h2_ref, q_ref):
    d_f = of_ref.shape[2]
    y = yf_ref[0] + yb_ref[0] + dsk_ref[...] * xs_ref[0]
    y = y * _silu(z_ref[0])
    gw = y.shape[1] // SSM_GROUPS
    nw = nw_ref[...]
    parts = [_rms(y[:, g * gw:(g + 1) * gw], nw[:, g * gw:(g + 1) * gw]).astype(BF16) for g in range(SSM_GROUPS)]
    yx = jnp.dot(of_ref[0], wo_ref[:d_f, :], preferred_element_type=F32)
    for g in range(SSM_GROUPS):
        yx = yx + jnp.dot(parts[g], wo_ref[d_f + g * gw:d_f + (g + 1) * gw, :], preferred_element_type=F32)
    x1 = x_ref[0] + gm_ref[0] * _rms(yx, gpost_ref[...])
    x1_ref[0] = x1
    h2 = (_rms(x1, gpre_ref[...]) * (1.0 + cf_ref[0]) + sf_ref[0]).astype(BF16)
    h2_ref[0] = h2
    q_ref[0] = jnp.dot(h2, wq_ref[...], preferred_element_type=F32)


def _outproj(of, yf, yb, xbc, proj, z_col0, x, dsk, nw, w_out, g_post, gm, g_pre, cf, sf, w_query, tm=256):
    b, l, d = x.shape
    d_f = of.shape[2]
    d_s = yf.shape[2]
    nq = w_query.shape[1]
    zb = z_col0 // d_s
    row = lambda bi, i: (bi, i, 0)
    vec = lambda n: pl.BlockSpec((1, n), lambda bi, i: (0, 0))
    bvec = lambda n: pl.BlockSpec((1, 1, n), lambda bi, i: (bi, 0, 0))
    return pl.pallas_call(
        _outproj_kernel,
        grid=(b, l // tm),
        in_specs=[pl.BlockSpec((1, tm, d_f), row), pl.BlockSpec((1, tm, d_s), row), pl.BlockSpec((1, tm, d_s), row),
                  pl.BlockSpec((1, tm, d_s), row), pl.BlockSpec((1, tm, d_s), lambda bi, i: (bi, i, zb)),
                  pl.BlockSpec((1, tm, d), row), vec(d_s), vec(d_s),
                  pl.BlockSpec((d_f + d_s, d), lambda bi, i: (0, 0)), vec(d), bvec(d), vec(d), bvec(d), bvec(d),
                  pl.BlockSpec((d, nq), lambda bi, i: (0, 0))],
        out_specs=[pl.BlockSpec((1, tm, d), row), pl.BlockSpec((1, tm, d), row), pl.BlockSpec((1, tm, nq), row)],
        out_shape=[jax.ShapeDtypeStruct((b, l, d), F32), jax.ShapeDtypeStruct((b, l, d), BF16),
                   jax.ShapeDtypeStruct((b, l, nq), F32)],
        compiler_params=_cparams(("arbitrary", "arbitrary")),
        name="outproj_prenorm_query",
    )(of, yf, yb, xbc, proj, x, dsk, nw, w_out, g_post, gm, g_pre, cf, sf, w_query)


SUBLANES = 8


def _top16_ranked(s):
    n = s.shape[0]
    iota = lax.broadcasted_iota(jnp.int32, s.shape, 0).astype(F32)
    cur = s
    rank = jnp.full(s.shape, float(PEER_TOPK), F32)
    vals = []
    for k in range(PEER_TOPK):
        m = jnp.max(cur, axis=0, keepdims=True)
        idx = jnp.min(jnp.where(cur == m, iota, float(n)), axis=0, keepdims=True)
        hit = iota == idx
        rank = jnp.where(hit, float(k), rank)
        cur = jnp.where(hit, -jnp.inf, cur)
        vals.append(m)
    return vals, rank


def _candidate_pieces(v1, v2, sub):
    def stack8(vals):
        out = jnp.zeros(sub.shape, F32)
        for k, v in enumerate(vals):
            out = jnp.where(sub == k, v, out)
        return out

    v2_lo, v2_hi, v1_hi = stack8(v2[:8]), stack8(v2[8:]), stack8(v1[8:])
    return [v1[0] + v2_lo, v1[0] + v2_hi] + [v1[a] + v2_lo for a in range(1, 8)] + [v1_hi + v2[0]]


def _row_counts(sels, sub):
    n_rows = [jnp.sum(sels[0] + sels[1], axis=0, keepdims=True)]
    n_rows += [jnp.sum(sels[a + 1], axis=0, keepdims=True) for a in range(1, 8)]
    n_rows += [jnp.sum(jnp.where(sub == r, sels[9], 0.0), axis=0, keepdims=True) for r in range(8)]
    return n_rows


def _route_columns_exact(s1, s2):
    kk = PEER_TOPK
    v1, rank1 = _top16_ranked(s1)
    v2, rank2 = _top16_ranked(s2)
    sub = lax.broadcasted_iota(jnp.int32, (SUBLANES, s1.shape[1]), 0)
    subf = sub.astype(F32)
    cands = _candidate_pieces(v1, v2, sub)
    poses = [subf, subf + 8.0] + [a * kk + subf for a in range(1, 8)] + [(subf + 8.0) * kk]
    sels = [jnp.zeros(sub.shape, F32) for _ in cands]
    m0 = v1[0] + v2[0]
    zsum = jnp.zeros_like(m0)
    for _ in range(kk):
        m = jnp.max(functools.reduce(jnp.maximum, cands), axis=0, keepdims=True)
        firsts = [jnp.where(cd == m, ps, float(kk * kk)) for cd, ps in zip(cands, poses)]
        p = jnp.min(functools.reduce(jnp.minimum, firsts), axis=0, keepdims=True)
        hits = [ps == p for ps in poses]
        sels = [jnp.where(ht, 1.0, sl) for ht, sl in zip(hits, sels)]
        cands = [jnp.where(ht, -jnp.inf, cd) for ht, cd in zip(hits, cands)]
        zsum = zsum + jnp.exp(m - m0)
    n_rows = _row_counts(sels, sub)
    cnt = jnp.zeros(rank1.shape, F32)
    for a in range(kk):
        cnt = jnp.where(rank1 == float(a), n_rows[a], cnt)
    return rank2, cnt, zsum, v1[0], v2[0]


def _sorted_top16(blocks):
    v = list(blocks)
    n = len(v)

    def exchange(i, l, descending):
        hi, lo = jnp.maximum(v[i], v[l]), jnp.minimum(v[i], v[l])
        v[i], v[l] = (hi, lo) if descending else (lo, hi)

    k = 2
    while k <= n:
        j = k // 2
        while j >= 1:
            for i in range(n):
                if i ^ j > i:
                    exchange(i, i ^ j, (i & k) == 0)
            j //= 2
        k *= 2
    for shift in (4, 2, 1):
        w = [pltpu.roll(x, shift, 0) for x in v]
        v = [jnp.maximum(v[r], w[n - 1 - r]) for r in range(n)]
        j = n // 2
        while j >= 1:
            for i in range(n):
                if i ^ j > i:
                    exchange(i, i ^ j, True)
            j //= 2
    return v


def _route_columns_sorted(s1, s2):
    kk = PEER_TOPK
    nb = s1.shape[0] // SUBLANES
    tcols = s1.shape[1]
    blocks = lambda s: [s[SUBLANES * r:SUBLANES * (r + 1), :] for r in range(nb)]
    b1, b2 = blocks(s1), blocks(s2)
    v1, v2 = _sorted_top16(b1), _sorted_top16(b2)
    sub = lax.broadcasted_iota(jnp.int32, (SUBLANES, tcols), 0)
    cands = _candidate_pieces(v1, v2, sub)
    neg = jnp.full((SUBLANES, tcols), -jnp.inf, F32)
    top = _sorted_top16(cands + [neg] * (nb - len(cands)))
    tau = top[kk - 1]
    m0 = v1[0] + v2[0]
    picked = [cd >= tau for cd in cands]
    sels = [jnp.where(pk, 1.0, 0.0) for pk in picked]
    zparts = [jnp.where(pk, jnp.exp(cd - m0), 0.0) for pk, cd in zip(picked, cands)]
    zsum = jnp.sum(functools.reduce(jnp.add, zparts), axis=0, keepdims=True)
    n_rows = _row_counts(sels, sub)

    def rank_in(v, blk):
        gt = lambda pivot: pivot > blk
        c1 = gt(v[7])
        c2 = gt(jnp.where(c1, v[11], v[3]))
        c3 = gt(jnp.where(c1, jnp.where(c2, v[13], v[9]), jnp.where(c2, v[5], v[1])))
        even = [jnp.where(c3, v[4 * q + 2], v[4 * q]) for q in range(4)]
        c4 = gt(jnp.where(c1, jnp.where(c2, even[3], even[2]), jnp.where(c2, even[1], even[0])))
        c5 = gt(v[15])
        bit = lambda c, val: jnp.where(c, val, 0.0)
        return (bit(c1, 8.0) + bit(c2, 4.0)) + (bit(c3, 2.0) + bit(c4, 1.0)) + bit(c5, 1.0)

    rank2_blocks, cnt_blocks = [], []
    for blk1, blk2 in zip(b1, b2):
        rk = rank_in(v2, blk2)
        ct = jnp.zeros(blk1.shape, F32)
        for a in range(kk):
            ct = jnp.where(blk1 == v1[a], n_rows[a], ct)
        rank2_blocks.append(rk)
        cnt_blocks.append(ct)
    rank2 = jnp.concatenate(rank2_blocks, axis=0)
    cnt = jnp.concatenate(cnt_blocks, axis=0)

    count = lambda flags: jnp.sum(functools.reduce(jnp.add, flags), axis=0, keepdims=True)
    strict = lambda v: functools.reduce(jnp.logical_and, [v[k] > v[k + 1] for k in range(kk - 1)])[0:1]
    ok = (strict(v1) & strict(v2) & strict(top)
          & (count([jnp.where(blk >= v1[kk - 1], 1.0, 0.0) for blk in b1]) == float(kk))
          & (count([jnp.where(blk >= v2[kk - 1], 1.0, 0.0) for blk in b2]) == float(kk))
          & (functools.reduce(jnp.add, n_rows) == float(kk)))
    return rank2, cnt, zsum, v1[0][0:1], v2[0][0:1], ok


def _peer_route_kernel(q_ref, k1_ref, k2_ref, rank2_ref, cnt_ref, e1_ref, e2_ref, s1_ref, s2_ref):
    def split(v):
        hi = v.astype(BF16)
        return hi, (v - hi.astype(F32)).astype(BF16)

    def scores(keys, qs):
        k_hi, k_lo = split(keys)
        q_hi, q_lo = split(qs)
        return _dot_nt(k_hi, q_hi) + (_dot_nt(k_hi, q_lo) + _dot_nt(k_lo, q_hi))

    qh = q_ref[...]
    s1_ref[...] = scores(k1_ref[0], qh[:, :PEER_HALF])
    s2_ref[...] = scores(k2_ref[0], qh[:, PEER_HALF:])

    width = min(2 * LANE, s1_ref.shape[1])

    def column(ci, carry):
        cols = pl.ds(pl.multiple_of(ci * width, width), width)
        s1 = s1_ref[:, cols]
        s2 = s2_ref[:, cols]

        def emit(rank2, cnt, zsum, m1, m2):
            rank2_ref[0, :, cols] = rank2.astype(BF16)
            cnt_ref[0, :, cols] = cnt
            e1_ref[0, :, cols] = jnp.exp(s1 - m1)
            e2_ref[0, :, cols] = (jnp.exp(s2 - m2) / zsum).astype(BF16)

        *fast, ok = _route_columns_sorted(s1, s2)
        emit(*fast)
        n_bad = jnp.sum(jnp.where(ok, 0.0, 1.0))

        @pl.when(n_bad > 0.0)
        def _():
            emit(*_route_columns_exact(s1, s2))

        return carry

    lax.fori_loop(0, s1_ref.shape[1] // width, column, 0)


def _peer_route(q, k1, k2, tq=1024):
    t, _ = q.shape
    nh, nk, hd = k1.shape
    out = lambda dt: jax.ShapeDtypeStruct((nh, nk, t), dt)
    spec = pl.BlockSpec((1, nk, tq), lambda i, h: (h, 0, i))
    return pl.pallas_call(
        _peer_route_kernel,
        grid=(t // tq, nh),
        in_specs=[pl.BlockSpec((tq, PEER_KEY_DIM), lambda i, h: (i, h)),
                  pl.BlockSpec((1, nk, hd), lambda i, h: (h, 0, 0)),
                  pl.BlockSpec((1, nk, hd), lambda i, h: (h, 0, 0))],
        out_specs=[spec, spec, spec, spec],
        out_shape=[out(BF16), out(F32), out(F32), out(BF16)],
        scratch_shapes=[pltpu.VMEM((nk, tq), F32), pltpu.VMEM((nk, tq), F32)],
        compiler_params=_cparams(("arbitrary", "arbitrary")),
        name="peer_route",
    )(q, k1, k2)


BF16_ROWS = 16


def _gate_weights(w_ref, rank2_ref, e2_ref, cnt_ref, e1_ref):
    nk = PEER_N_KEYS
    tm = w_ref.shape[1]
    for i_loc in range(w_ref.shape[0] // nk):
        w = None
        for h in range(PEER_HEADS):
            cnt_b = jnp.broadcast_to(cnt_ref[h, i_loc:i_loc + 1, :], (BF16_ROWS, tm)).astype(BF16)
            e1_b = jnp.broadcast_to(e1_ref[h, i_loc:i_loc + 1, :], (BF16_ROWS, tm)).astype(BF16)
            wh = jnp.where(rank2_ref[h] < cnt_b[None], e2_ref[h], jnp.zeros((), BF16)) * e1_b[None]
            w = wh if w is None else w + wh
        w_ref[i_loc * nk:(i_loc + 1) * nk, :] = w.reshape(nk, tm)


def _peer_dense_kernel(n_sub, h2_ref, u_ref, vt_ref, rank2_ref, e2_ref, cnt_ref, e1_ref,
                       x1_ref, gf_ref, gpost_ref, o_ref, acc_ref, h2t_ref, w_ref, *g_refs):
    j = pl.program_id(1)

    @pl.when(j == 0)
    def _():
        acc_ref[...] = jnp.zeros_like(acc_ref)
        h2t_ref[...] = h2_ref[...].T

    nk = PEER_N_KEYS
    ts = u_ref.shape[0] // n_sub
    ats = [jnp.dot(u_ref[s * ts:(s + 1) * ts, :], h2t_ref[...], preferred_element_type=F32)
           for s in range(n_sub)]
    _gate_weights(w_ref, rank2_ref, e2_ref, cnt_ref, e1_ref)
    for s in range(n_sub):
        at = ats[s]
        for il in range(ts // nk):
            rows = slice(s * ts + il * nk, s * ts + (il + 1) * nk)
            a = at[il * nk:(il + 1) * nk, :]
            gelu = 0.5 * a * (1.0 + lax.erf(a * (1.0 / math.sqrt(2.0))))
            g_refs[s][il * nk:(il + 1) * nk, :] = gelu.astype(BF16) * w_ref[rows, :]
        acc_ref[...] += jnp.dot(vt_ref[:, s * ts:(s + 1) * ts], g_refs[s][...], preferred_element_type=F32)

    @pl.when(j == pl.num_programs(1) - 1)
    def _():
        y = acc_ref[...].T
        o_ref[...] = x1_ref[...] + gf_ref[0] * _rms(y, gpost_ref[...])


def _peer_dense(h2, u16, vt16, rank2, cnt, e1, e2, x1, gf, g_post, seq_len, tm=512, ts=512, n_sub=2):
    t, d = h2.shape
    ne = u16.shape[0]
    nh, nk, _ = cnt.shape
    te = ts * n_sub
    i_per_step = te // nk
    n_steps = ne // te
    aux_j = pl.BlockSpec((nh, nk // BF16_ROWS, BF16_ROWS, tm), lambda i, j: (0, 0, 0, i))
    aux_i = pl.BlockSpec((nh, i_per_step, tm), lambda i, j: (0, j, i))
    blocks_per_seq = seq_len // tm
    return pl.pallas_call(
        functools.partial(_peer_dense_kernel, n_sub),
        grid=(t // tm, n_steps),
        in_specs=[pl.BlockSpec((tm, d), lambda i, j: (i, 0)),
                  pl.BlockSpec((te, d), lambda i, j: (j, 0)),
                  pl.BlockSpec((d, te), lambda i, j: (0, j)),
                  aux_j, aux_j, aux_i, aux_i,
                  pl.BlockSpec((tm, d), lambda i, j: (i, 0), pipeline_mode=pl.Buffered(1)),
                  pl.BlockSpec((1, 1, d), lambda i, j: (i // blocks_per_seq, 0, 0)),
                  pl.BlockSpec((1, d), lambda i, j: (0, 0))],
        out_specs=pl.BlockSpec((tm, d), lambda i, j: (i, 0)),
        out_shape=jax.ShapeDtypeStruct((t, d), F32),
        scratch_shapes=[pltpu.VMEM((d, tm), F32), pltpu.VMEM((d, tm), BF16), pltpu.VMEM((te, tm), BF16)]
        + [pltpu.VMEM((ts, tm), BF16) for _ in range(n_sub)],
        compiler_params=_cparams(("arbitrary", "arbitrary")),
        name="peer_dense",
    )(h2, u16, vt16, rank2, e2, cnt, e1, x1, gf, g_post)


def _dft_tables(n):
    k = np.arange(n, dtype=np.int64)
    ph = (np.outer(k, k) % n).astype(np.float64) * (2.0 * np.pi / n)
    return np.cos(ph), np.sin(ph)


def kernel(x, c, ctx, c_ctx, w_mod, b_mod, g_pre_mix, g_post_mix, g_pre_ffn, g_post_ffn, w_in, w_fmix, conv_w, conv_b, dt_bias_f, dt_bias_b, a_log_f, a_log_b, d_skip_f, d_skip_b, ssm_norm_w, w_out, w_query, sub_keys_1, sub_keys_2, expert_u, expert_v):
    bsz, seq, d = x.shape
    ctx_len = ctx.shape[1]
    layer = 0
    n_heads = dt_bias_f.shape[1]
    d_ssm = n_heads * SSM_HEAD_DIM
    d_f = d - d_ssm
    gc = d_f // N_FOURIER_GROUPS
    d_xbc = conv_w.shape[2]
    assert w_mod.shape[0] == 1, "single-layer kernel"

    cond = jnp.zeros((8, d), F32).at[:bsz].set(c).at[bsz].set(c_ctx)
    mod = _adaln(cond, w_mod[layer], b_mod[layer])
    mods = [mod[:, i * d:(i + 1) * d] for i in range(N_MOD)]
    sm_x, cm_x, gm_x, sf_x, cf_x, gf_x = [m[:bsz, None, :] for m in mods]
    sm_c = jnp.broadcast_to(mods[0][bsz][None, None, :], (bsz, 1, d))
    cm_c = jnp.broadcast_to(mods[1][bsz][None, None, :], (bsz, 1, d))

    cos_l, sin_l = _dft_tables(seq)
    cos_c, sin_c = _dft_tables(gc)
    scale = 1.0 / math.sqrt(seq * gc)
    dt_pad = LANE
    col_z = 2 * d_f
    col_xbc = col_z + d_ssm
    col_dt = col_xbc + d_xbc
    n_all = col_dt + dt_pad
    assert (d_ssm + d_xbc) % gc == 0, "z/xBC columns must fill whole weight tiles"
    w_all = _build_weights(w_in[layer].T, w_fmix[layer], jnp.asarray(cos_c * scale, F32),
                           jnp.asarray(sin_c * scale, F32), pl.cdiv(n_all, gc) * gc)
    dft = jnp.asarray(np.concatenate([cos_l, -sin_l], axis=1), F32).astype(BF16)

    pad32 = lambda f, b_: jnp.zeros((1, LANE), F32).at[0, :n_heads].set(f).at[0, n_heads:2 * n_heads].set(b_)
    bias_row = pad32(dt_bias_f[layer], dt_bias_b[layer])
    a_row = pad32(-jnp.exp(a_log_f[layer]), -jnp.exp(a_log_b[layer]))

    proj_c = _inproj(ctx, g_pre_mix[layer], sm_c, cm_c, w_all, col_xbc, n_all - col_xbc, tm=ctx_len, tn=LANE)
    xbc_c = _conv_silu(proj_c, 0, conv_w[layer], conv_b[layer])
    zero_state = jnp.zeros((bsz, 2, n_heads // 2, SSM_STATE, 2 * SSM_HEAD_DIM), F32)
    (fin_c,) = _ssd(proj_c, d_xbc, xbc_c, bias_row, a_row, zero_state, emit_y=False)

    proj_x = _inproj(x, g_pre_mix[layer], sm_x, cm_x, w_all, 0, n_all, tm=min(256, seq), tn=n_all)
    o_f = _seqdft(dft, proj_x, d_f)
    xbc_x = _conv_silu(proj_x, col_xbc, conv_w[layer], conv_b[layer])
    y_f, y_b, _ = _ssd(proj_x, col_dt, xbc_x, bias_row, a_row, fin_c, emit_y=True)

    dsk = jnp.repeat(d_skip_f[layer] + d_skip_b[layer], SSM_HEAD_DIM)[None, :]
    x1, h2, q = _outproj(o_f, y_f, y_b, xbc_x, proj_x, col_z, x, dsk, ssm_norm_w[layer][None, :],
                         w_out[layer].astype(BF16), g_post_mix[layer][None, :], gm_x,
                         g_pre_ffn[layer][None, :], cf_x, sf_x, w_query[layer].astype(BF16))

    t = bsz * seq
    rank2, cnt, e1, e2 = _peer_route(q.reshape(t, -1), sub_keys_1[layer], sub_keys_2[layer],
                                     tq=min(2048, t))
    tiles = lambda a: a.reshape(a.shape[0], a.shape[1] // BF16_ROWS, BF16_ROWS, t)
    u16 = expert_u[layer].astype(BF16)
    vt16 = expert_v[layer].T.astype(BF16)
    out = _peer_dense(h2.reshape(t, d), u16, vt16, tiles(rank2), cnt, e1, tiles(e2), x1.reshape(t, d), gf_x,
                      g_post_ffn[layer][None, :], seq)
    return out.reshape(bsz, seq, d)
```

```python
import functools
import math

import jax
import jax.numpy as jnp
import numpy as np
from jax import lax
from jax.experimental import pallas as pl
from jax.experimental.pallas import tpu as pltpu

F32 = jnp.float32
BF16 = jnp.bfloat16

EPS = 1e-6
N_MOD = 6
N_FOURIER_GROUPS = 4
SSM_HEAD_DIM = 64
SSM_GROUPS = 2
SSM_STATE = 128
SSM_CHUNK = 128
PEER_HEADS = 8
PEER_TOPK = 16
PEER_N_KEYS = 128
PEER_KEY_DIM = 128
PEER_HALF = PEER_KEY_DIM // 2

LANE = 128
VMEM_LIMIT = 56 * 1024 * 1024


def _cparams(sem):
    return pltpu.CompilerParams(dimension_semantics=sem, vmem_limit_bytes=VMEM_LIMIT)


def _bdot(a, b):
    return jnp.dot(a.astype(BF16), b.astype(BF16), preferred_element_type=F32)


def _dot_nt(a, b, precision=None):
    return lax.dot_general(a, b, (((1,), (1,)), ((), ())), precision=precision,
                           preferred_element_type=F32)


def _rms(u, g):
    return u * lax.rsqrt(jnp.mean(u * u, axis=-1, keepdims=True) + EPS) * g


def _silu(u):
    return u * (1.0 / (1.0 + jnp.exp(-u)))


def _adaln_kernel(c_ref, w_ref, b_ref, o_ref):
    a = _silu(c_ref[...])
    o_ref[...] = _bdot(a, w_ref[...]) + b_ref[...]


def _adaln(cond, w_mod, b_mod, tn=1024):
    rows, d = cond.shape
    n = w_mod.shape[1]
    return pl.pallas_call(
        _adaln_kernel,
        grid=(n // tn,),
        in_specs=[pl.BlockSpec((rows, d), lambda j: (0, 0)),
                  pl.BlockSpec((d, tn), lambda j: (0, j)),
                  pl.BlockSpec((1, tn), lambda j: (0, j))],
        out_specs=pl.BlockSpec((rows, tn), lambda j: (0, j)),
        out_shape=jax.ShapeDtypeStruct((rows, n), F32),
        compiler_params=_cparams(("arbitrary",)),
        name="adaln",
    )(cond, w_mod, b_mod.reshape(1, n))


def _weights_kernel(ng, n_copy, wt_ref, tail_ref, fm_ref, cc_ref, sc_ref, o_ref):
    hp = lax.Precision.HIGHEST
    g = pl.program_id(0)

    def fold(tbl_ref):
        m = jnp.dot(tbl_ref[...], fm_ref[0], precision=hp, preferred_element_type=F32)
        o_ref[...] = jnp.dot(wt_ref[...].T, m, precision=hp, preferred_element_type=F32).astype(BF16)

    pl.when(g < ng)(lambda: fold(cc_ref))
    pl.when((g >= ng) & (g < 2 * ng))(lambda: fold(sc_ref))

    @pl.when((g >= 2 * ng) & (g < 2 * ng + n_copy))
    def _():
        o_ref[...] = wt_ref[...].T.astype(BF16)

    @pl.when(g == 2 * ng + n_copy)
    def _():
        tail = tail_ref[...]
        fill = jnp.zeros((o_ref.shape[1] - tail.shape[0], tail.shape[1]), F32)
        o_ref[...] = jnp.concatenate([tail, fill], axis=0).T.astype(BF16)

    @pl.when(g > 2 * ng + n_copy)
    def _():
        o_ref[...] = jnp.zeros_like(o_ref)


def _build_weights(w_in_t, w_fmix, cos_c, sin_c, n_all):
    n_in, d = w_in_t.shape
    ng, gc, _ = w_fmix.shape
    n_copy = (n_in - ng * gc) // gc
    tail_rows = n_in - (ng + n_copy) * gc
    assert 0 < tail_rows < gc and tail_rows % SUBLANES == 0 and ((ng + n_copy) * gc) % tail_rows == 0
    n_tiles = n_all // gc
    return pl.pallas_call(
        functools.partial(_weights_kernel, ng, n_copy),
        grid=(n_tiles,),
        in_specs=[pl.BlockSpec((gc, d), lambda g: (jnp.where(g < 2 * ng, g % ng, jnp.minimum(g - ng, ng + n_copy - 1)), 0)),
                  pl.BlockSpec((tail_rows, d), lambda g: ((ng + n_copy) * gc // tail_rows, 0)),
                  pl.BlockSpec((1, gc, gc), lambda g: (jnp.where(g < 2 * ng, g % ng, 0), 0, 0)),
                  pl.BlockSpec((gc, gc), lambda g: (0, 0)),
                  pl.BlockSpec((gc, gc), lambda g: (0, 0))],
        out_specs=pl.BlockSpec((d, gc), lambda g: (0, g)),
        out_shape=jax.ShapeDtypeStruct((d, n_tiles * gc), BF16),
        compiler_params=_cparams(("arbitrary",)),
        name="inproj_weights",
    )(w_in_t, w_in_t, w_fmix, cos_c, sin_c)


def _inproj_kernel(x_ref, g_ref, sh_ref, sc_ref, w_ref, o_ref, h_ref=None):
    modulated = lambda rows: (_rms(x_ref[0, rows, :], g_ref[...]) * (1.0 + sc_ref[0]) + sh_ref[0]).astype(BF16)
    if h_ref is None:
        o_ref[0] = jnp.dot(modulated(slice(None)), w_ref[...], preferred_element_type=F32)
        return

    @pl.when(pl.program_id(2) == 0)
    def _():
        slab = min(LANE, h_ref.shape[0])

        def body(r, carry):
            rows = pl.ds(pl.multiple_of(r * slab, slab), slab)
            h_ref[rows, :] = modulated(rows)
            return carry

        lax.fori_loop(0, h_ref.shape[0] // slab, body, 0)

    o_ref[0] = jnp.dot(h_ref[...], w_ref[...], preferred_element_type=F32)


def _inproj(xin, g, shift, scale, w_all, col0, ncols, tm, tn):
    b, l, d = xin.shape
    jb = col0 // tn
    w_mode = dict(pipeline_mode=pl.Buffered(1)) if ncols == tn else {}
    return pl.pallas_call(
        _inproj_kernel,
        grid=(b, l // tm, ncols // tn),
        in_specs=[pl.BlockSpec((1, tm, d), lambda bi, i, j: (bi, i, 0)),
                  pl.BlockSpec((1, d), lambda bi, i, j: (0, 0)),
                  pl.BlockSpec((1, 1, d), lambda bi, i, j: (bi, 0, 0)),
                  pl.BlockSpec((1, 1, d), lambda bi, i, j: (bi, 0, 0)),
                  pl.BlockSpec((d, tn), lambda bi, i, j: (0, j + jb), **w_mode)],
        out_specs=pl.BlockSpec((1, tm, tn), lambda bi, i, j: (bi, i, j)),
        out_shape=jax.ShapeDtypeStruct((b, l, ncols), F32),
        scratch_shapes=[] if ncols == tn else [pltpu.VMEM((tm, d), BF16)],
        compiler_params=_cparams(("arbitrary", "arbitrary", "arbitrary")),
        name="inproj",
    )(xin, g.reshape(1, d), shift, scale, w_all)


def _seqdft_kernel(a_ref, x_ref, o_ref, acc_ref):
    k = pl.program_id(1)

    @pl.when(k == 0)
    def _():
        acc_ref[...] = jnp.zeros_like(acc_ref)

    acc_ref[...] += jnp.dot(a_ref[...], x_ref[0].astype(BF16), preferred_element_type=F32)

    @pl.when(k == pl.num_programs(1) - 1)
    def _():
        o_ref[0] = acc_ref[...].astype(BF16)


def _seqdft(dft, proj, n_f, tk=1024):
    b, l, _ = proj.shape
    tk = min(tk, l)
    kl = l // tk
    return pl.pallas_call(
        _seqdft_kernel,
        grid=(b, 2 * kl),
        in_specs=[pl.BlockSpec((l, tk), lambda bi, k: (0, k)),
                  pl.BlockSpec((1, tk, n_f), lambda bi, k: (bi, k % kl, k // kl))],
        out_specs=pl.BlockSpec((1, l, n_f), lambda bi, k: (bi, 0, 0)),
        out_shape=jax.ShapeDtypeStruct((b, l, n_f), BF16),
        scratch_shapes=[pltpu.VMEM((l, n_f), F32)],
        compiler_params=_cparams(("arbitrary", "arbitrary")),
        name="seq_dft",
    )(dft, proj)


def _conv_kernel(u_ref, w_ref, b_ref, o_ref):
    u = u_ref[0]
    l = u.shape[0]
    row = lax.broadcasted_iota(jnp.int32, u.shape, 0)
    prev = jnp.where(row == 0, 0.0, pltpu.roll(u, 1, 0))
    nxt = jnp.where(row == l - 1, 0.0, pltpu.roll(u, l - 1, 0))
    w = w_ref[...]
    o_ref[0] = _silu(prev * w[0:1] + u * w[1:2] + nxt * w[2:3] + b_ref[...])


def _conv_silu(proj, col0, conv_w, conv_b, tc=512):
    b, l, _ = proj.shape
    n = conv_w.shape[1]
    jb = col0 // tc
    return pl.pallas_call(
        _conv_kernel,
        grid=(b, n // tc),
        in_specs=[pl.BlockSpec((1, l, tc), lambda bi, j: (bi, 0, j + jb)),
                  pl.BlockSpec((3, tc), lambda bi, j: (0, j)),
                  pl.BlockSpec((1, tc), lambda bi, j: (0, j))],
        out_specs=pl.BlockSpec((1, l, tc), lambda bi, j: (bi, 0, j)),
        out_shape=jax.ShapeDtypeStruct((b, l, n), F32),
        compiler_params=_cparams(("arbitrary", "arbitrary")),
        name="conv_silu",
    )(proj, conv_w, conv_b.reshape(1, n))


def _ssd_direction(reverse, dt_raw, bias, a_neg, xs_ref, bm_ref, cm_ref, st_ref, y_ref, n_heads):
    hp = lax.Precision.HIGHEST
    t = SSM_CHUNK
    p2 = 2 * SSM_HEAD_DIM
    col0 = n_heads if reverse else 0
    r = lax.broadcasted_iota(jnp.int32, (t, t), 0)
    c = lax.broadcasted_iota(jnp.int32, (t, t), 1)
    mask = (r <= c) if reverse else (r >= c)
    tri = mask.astype(F32)
    tri_t = ((r >= c) if reverse else (r <= c)).astype(F32)
    first_y = lax.broadcasted_iota(jnp.int32, (t, p2), 1) < SSM_HEAD_DIM
    first_s = lax.broadcasted_iota(jnp.int32, (SSM_STATE, p2), 1) < SSM_HEAD_DIM

    z = dt_raw + bias
    dtv = jnp.maximum(z, 0.0) + jnp.log(1.0 + jnp.exp(-jnp.abs(z)))
    a = dtv * a_neg
    a_t = a.T
    dtv_t = dtv.T
    cs = jnp.dot(tri, a, precision=hp, preferred_element_type=F32)
    cs_t = jnp.dot(a_t, tri_t, precision=hp, preferred_element_type=F32)
    w_t = dtv_t * jnp.exp(jnp.sum(a_t, axis=1, keepdims=True) - cs_t)
    etot = jnp.exp(jnp.sum(a, axis=0, keepdims=True))

    ppg = n_heads // SSM_GROUPS // 2
    for g in range(SSM_GROUPS):
        bg = bm_ref[0, :, g * SSM_STATE:(g + 1) * SSM_STATE]
        cg16 = cm_ref[0, :, g * SSM_STATE:(g + 1) * SSM_STATE].astype(BF16)
        gmat = _dot_nt(cg16, bg.astype(BF16))
        bgt = bg.T
        if y_ref is not None:
            st_g16 = jnp.concatenate([st_ref[g * ppg + k] for k in range(ppg)], axis=1).astype(BF16)
            yoff_g = jnp.dot(cg16, st_g16, preferred_element_type=F32)
        for k in range(ppg):
            pi = g * ppg + k
            xs16 = xs_ref[0, :, pi * p2:(pi + 1) * p2].astype(BF16)
            ys, ss, ecols, etots = [], [], [], []
            for e in range(2):
                j = col0 + 2 * pi + e
                col_b = jnp.broadcast_to(cs[:, j:j + 1], (t, t))
                lmat = jnp.exp(jnp.where(mask, col_b - cs_t[j:j + 1, :], -jnp.inf))
                m16 = (gmat * lmat * dtv_t[j:j + 1, :]).astype(BF16)
                if y_ref is not None:
                    ys.append(jnp.dot(m16, xs16, preferred_element_type=F32))
                    ecols.append(jnp.exp(col_b))
                bw16 = (bgt * w_t[j:j + 1, :]).astype(BF16)
                ss.append(jnp.dot(bw16, xs16, preferred_element_type=F32))
                etots.append(etot[:, j:j + 1])
            if y_ref is not None:
                y_off = yoff_g[:, k * p2:(k + 1) * p2] * jnp.where(first_y, ecols[0], ecols[1])
                y_ref[0, :, pi * p2:(pi + 1) * p2] = jnp.where(first_y, ys[0], ys[1]) + y_off
            st_ref[pi] = (jnp.where(first_s, etots[0], etots[1]) * st_ref[pi]
                          + jnp.where(first_s, ss[0], ss[1]))


def _ssd_kernel(n_heads, emit_y, *refs):
    (dtf_ref, dtb_ref, bias_ref, a_ref, xsf_ref, xsb_ref, bmf_ref, bmb_ref, cmf_ref, cmb_ref,
     init_ref) = refs[:11]
    if emit_y:
        yf_ref, yb_ref, fin_ref, stf_ref, stb_ref = refs[11:]
    else:
        yf_ref = yb_ref = None
        fin_ref, stf_ref, stb_ref = refs[11:]
    ci = pl.program_id(1)

    @pl.when(ci == 0)
    def _():
        stf_ref[...] = init_ref[0, 0]
        stb_ref[...] = init_ref[0, 1]

    bias = bias_ref[...]
    a_neg = a_ref[...]
    _ssd_direction(False, dtf_ref[0], bias, a_neg, xsf_ref, bmf_ref, cmf_ref, stf_ref, yf_ref, n_heads)
    _ssd_direction(True, dtb_ref[0], bias, a_neg, xsb_ref, bmb_ref, cmb_ref, stb_ref, yb_ref, n_heads)

    @pl.when(ci == pl.num_programs(1) - 1)
    def _():
        fin_ref[0, 0] = stf_ref[...]
        fin_ref[0, 1] = stb_ref[...]


def _ssd(proj, dt_col0, xbc, bias_row, a_row, init, emit_y):
    b, l, _ = proj.shape
    d_ssm = xbc.shape[2] - 2 * SSM_GROUPS * SSM_STATE
    d_bc = SSM_GROUPS * SSM_STATE
    n_heads = d_ssm // SSM_HEAD_DIM
    t = SSM_CHUNK
    nc = l // t
    dtb = dt_col0 // LANE
    fwd = lambda bi, c: (bi, c, 0)
    bwd = lambda bi, c: (bi, nc - 1 - c, 0)
    off = lambda f, o: (lambda bi, c: f(bi, c)[:2] + (o,))
    st_shape = (n_heads // 2, SSM_STATE, 2 * SSM_HEAD_DIM)
    in_specs = [
        pl.BlockSpec((1, t, LANE), off(fwd, dtb)), pl.BlockSpec((1, t, LANE), off(bwd, dtb)),
        pl.BlockSpec((1, LANE), lambda bi, c: (0, 0)), pl.BlockSpec((1, LANE), lambda bi, c: (0, 0)),
        pl.BlockSpec((1, t, d_ssm), fwd), pl.BlockSpec((1, t, d_ssm), bwd),
        pl.BlockSpec((1, t, d_bc), off(fwd, d_ssm // d_bc)), pl.BlockSpec((1, t, d_bc), off(bwd, d_ssm // d_bc)),
        pl.BlockSpec((1, t, d_bc), off(fwd, d_ssm // d_bc + 1)), pl.BlockSpec((1, t, d_bc), off(bwd, d_ssm // d_bc + 1)),
        pl.BlockSpec((1, 2) + st_shape, lambda bi, c: (bi, 0, 0, 0, 0)),
    ]
    fin_spec = pl.BlockSpec((1, 2) + st_shape, lambda bi, c: (bi, 0, 0, 0, 0))
    fin_shape = jax.ShapeDtypeStruct((b, 2) + st_shape, F32)
    if emit_y:
        y_shape = jax.ShapeDtypeStruct((b, l, d_ssm), F32)
        out_specs = [pl.BlockSpec((1, t, d_ssm), fwd), pl.BlockSpec((1, t, d_ssm), bwd), fin_spec]
        out_shape = [y_shape, y_shape, fin_shape]
    else:
        out_specs = [fin_spec]
        out_shape = [fin_shape]
    return pl.pallas_call(
        functools.partial(_ssd_kernel, n_heads, emit_y),
        grid=(b, nc),
        in_specs=in_specs,
        out_specs=out_specs,
        out_shape=out_shape,
        scratch_shapes=[pltpu.VMEM(st_shape, F32), pltpu.VMEM(st_shape, F32)],
        compiler_params=_cparams(("arbitrary", "arbitrary")),
        name="ssd_scan_y" if emit_y else "ssd_scan_state",
    )(proj, proj, bias_row, a_row, xbc, xbc, xbc, xbc, xbc, xbc, init)


def _outproj_kernel(of_ref, yf_ref, yb_ref, xs_ref, z_ref, x_ref, dsk_ref, nw_ref, wo_ref, gpost_ref,
                    gm_ref, gpre_ref, cf_ref, sf_ref, wq_ref, x1_ref, h2_ref, q_ref):
    gw = yf_ref.shape[2] // SSM_GROUPS
    nw = nw_ref[...]
    n_slabs = 2
    rs = x_ref.shape[1] // n_slabs
    yxs = []
    for s in range(n_slabs):
        rows = slice(s * rs, (s + 1) * rs)
        y = yf_ref[0, rows, :] + yb_ref[0, rows, :] + dsk_ref[...] * xs_ref[0, rows, :]
        y = y * _silu(z_ref[0, rows, :])
        parts = [_rms(y[:, g * gw:(g + 1) * gw], nw[:, g * gw:(g + 1) * gw]).astype(BF16)
                 for g in range(SSM_GROUPS)]
        mixed = jnp.concatenate([of_ref[0, rows, :]] + parts, axis=1)
        yxs.append(jnp.dot(mixed, wo_ref[...], preferred_element_type=F32))
    for s in range(n_slabs):
        rows = slice(s * rs, (s + 1) * rs)
        yx = yxs[s]
        x1 = x_ref[0, rows, :] + gm_ref[0] * _rms(yx, gpost_ref[...])
        x1_ref[0, rows, :] = x1
        h2 = (_rms(x1, gpre_ref[...]) * (1.0 + cf_ref[0]) + sf_ref[0]).astype(BF16)
        h2_ref[0, rows, :] = h2
        q_ref[0, rows, :] = jnp.dot(h2, wq_ref[...], preferred_element_type=F32)


def _outproj(of, yf, yb, xbc, proj, z_col0, x, dsk, nw, w_out, g_post, gm, g_pre, cf, sf, w_query, tm=256):
    b, l, d = x.shape
    d_f = of.shape[2]
    d_s = yf.shape[2]
    nq = w_query.shape[1]
    zb = z_col0 // d_s
    row = lambda bi, i: (bi, i, 0)
    vec = lambda n: pl.BlockSpec((1, n), lambda bi, i: (0, 0))
    bvec = lambda n: pl.BlockSpec((1, 1, n), lambda bi, i: (bi, 0, 0))
    return pl.pallas_call(
        _outproj_kernel,
        grid=(b, l // tm),
        in_specs=[pl.BlockSpec((1, tm, d_f), row), pl.BlockSpec((1, tm, d_s), row), pl.BlockSpec((1, tm, d_s), row),
                  pl.BlockSpec((1, tm, d_s), row), pl.BlockSpec((1, tm, d_s), lambda bi, i: (bi, i, zb)),
                  pl.BlockSpec((1, tm, d), row), vec(d_s), vec(d_s),
                  pl.BlockSpec((d_f + d_s, d), lambda bi, i: (0, 0)), vec(d), bvec(d), vec(d), bvec(d), bvec(d),
                  pl.BlockSpec((d, nq), lambda bi, i: (0, 0))],
        out_specs=[pl.BlockSpec((1, tm, d), row), pl.BlockSpec((1, tm, d), row), pl.BlockSpec((1, tm, nq), row)],
        out_shape=[jax.ShapeDtypeStruct((b, l, d), F32), jax.ShapeDtypeStruct((b, l, d), BF16),
                   jax.ShapeDtypeStruct((b, l, nq), F32)],
        compiler_params=_cparams(("arbitrary", "arbitrary")),
        name="outproj_prenorm_query",
    )(of, yf, yb, xbc, proj, x, dsk, nw, w_out, g_post, gm, g_pre, cf, sf, w_query)


SUBLANES = 8


def _top16_ranked(s):
    n = s.shape[0]
    iota = lax.broadcasted_iota(jnp.int32, s.shape, 0).astype(F32)
    cur = s
    rank = jnp.full(s.shape, float(PEER_TOPK), F32)
    vals = []
    for k in range(PEER_TOPK):
        m = jnp.max(cur, axis=0, keepdims=True)
        idx = jnp.min(jnp.where(cur == m, iota, float(n)), axis=0, keepdims=True)
        hit = iota == idx
        rank = jnp.where(hit, float(k), rank)
        cur = jnp.where(hit, -jnp.inf, cur)
        vals.append(m)
    return vals, rank


def _candidate_pieces(v1, v2, sub):
    def stack8(vals):
        out = jnp.zeros(sub.shape, F32)
        for k, v in enumerate(vals):
            out = jnp.where(sub == k, v, out)
        return out

    v2_lo, v2_hi, v1_hi = stack8(v2[:8]), stack8(v2[8:]), stack8(v1[8:])
    return [v1[0] + v2_lo, v1[0] + v2_hi] + [v1[a] + v2_lo for a in range(1, 8)] + [v1_hi + v2[0]]


def _row_counts(sels, sub):
    n_rows = [jnp.sum(sels[0] + sels[1], axis=0, keepdims=True)]
    n_rows += [jnp.sum(sels[a + 1], axis=0, keepdims=True) for a in range(1, 8)]
    n_rows += [jnp.sum(jnp.where(sub == r, sels[9], 0.0), axis=0, keepdims=True) for r in range(8)]
    return n_rows


def _route_columns_exact(s1, s2):
    kk = PEER_TOPK
    v1, rank1 = _top16_ranked(s1)
    v2, rank2 = _top16_ranked(s2)
    sub = lax.broadcasted_iota(jnp.int32, (SUBLANES, s1.shape[1]), 0)
    subf = sub.astype(F32)
    cands = _candidate_pieces(v1, v2, sub)
    poses = [subf, subf + 8.0] + [a * kk + subf for a in range(1, 8)] + [(subf + 8.0) * kk]
    sels = [jnp.zeros(sub.shape, F32) for _ in cands]
    m0 = v1[0] + v2[0]
    zsum = jnp.zeros_like(m0)
    for _ in range(kk):
        m = jnp.max(functools.reduce(jnp.maximum, cands), axis=0, keepdims=True)
        firsts = [jnp.where(cd == m, ps, float(kk * kk)) for cd, ps in zip(cands, poses)]
        p = jnp.min(functools.reduce(jnp.minimum, firsts), axis=0, keepdims=True)
        hits = [ps == p for ps in poses]
        sels = [jnp.where(ht, 1.0, sl) for ht, sl in zip(hits, sels)]
        cands = [jnp.where(ht, -jnp.inf, cd) for ht, cd in zip(hits, cands)]
        zsum = zsum + jnp.exp(m - m0)
    n_rows = _row_counts(sels, sub)
    cnt = jnp.zeros(rank1.shape, F32)
    for a in range(kk):
        cnt = jnp.where(rank1 == float(a), n_rows[a], cnt)
    return rank2, cnt, zsum, v1[0], v2[0]


def _sorted_top16(blocks):
    v = list(blocks)
    n = len(v)

    def exchange(i, l, descending):
        hi, lo = jnp.maximum(v[i], v[l]), jnp.minimum(v[i], v[l])
        v[i], v[l] = (hi, lo) if descending else (lo, hi)

    k = 2
    while k <= n:
        j = k // 2
        while j >= 1:
            for i in range(n):
                if i ^ j > i:
                    exchange(i, i ^ j, (i & k) == 0)
            j //= 2
        k *= 2
    for shift in (4, 2, 1):
        w = [pltpu.roll(x, shift, 0) for x in v]
        v = [jnp.maximum(v[r], w[n - 1 - r]) for r in range(n)]
        j = n // 2
        while j >= 1:
            for i in range(n):
                if i ^ j > i:
                    exchange(i, i ^ j, True)
            j //= 2
    return v


def _route_columns_sorted(s1, s2):
    kk = PEER_TOPK
    nb = s1.shape[0] // SUBLANES
    tcols = s1.shape[1]
    blocks = lambda s: [s[SUBLANES * r:SUBLANES * (r + 1), :] for r in range(nb)]
    b1, b2 = blocks(s1), blocks(s2)
    v1, v2 = _sorted_top16(b1), _sorted_top16(b2)
    sub = lax.broadcasted_iota(jnp.int32, (SUBLANES, tcols), 0)
    cands = _candidate_pieces(v1, v2, sub)
    neg = jnp.full((SUBLANES, tcols), -jnp.inf, F32)
    top = _sorted_top16(cands + [neg] * (nb - len(cands)))
    tau = top[kk - 1]
    m0 = v1[0] + v2[0]
    picked = [cd >= tau for cd in cands]
    sels = [jnp.where(pk, 1.0, 0.0) for pk in picked]
    zparts = [jnp.where(pk, jnp.exp(cd - m0), 0.0) for pk, cd in zip(picked, cands)]
    zsum = jnp.sum(functools.reduce(jnp.add, zparts), axis=0, keepdims=True)
    n_rows = _row_counts(sels, sub)

    def rank_in(v, blk):
        gt = lambda pivot: pivot > blk
        c1 = gt(v[7])
        c2 = gt(jnp.where(c1, v[11], v[3]))
        c3 = gt(jnp.where(c1, jnp.where(c2, v[13], v[9]), jnp.where(c2, v[5], v[1])))
        even = [jnp.where(c3, v[4 * q + 2], v[4 * q]) for q in range(4)]
        c4 = gt(jnp.where(c1, jnp.where(c2, even[3], even[2]), jnp.where(c2, even[1], even[0])))
        c5 = gt(v[15])
        bit = lambda c, val: jnp.where(c, val, 0.0)
        return (bit(c1, 8.0) + bit(c2, 4.0)) + (bit(c3, 2.0) + bit(c4, 1.0)) + bit(c5, 1.0)

    rank2_blocks, cnt_blocks = [], []
    for blk1, blk2 in zip(b1, b2):
        rk = rank_in(v2, blk2)
        ct = jnp.zeros(blk1.shape, F32)
        for a in range(kk):
            ct = jnp.where(blk1 == v1[a], n_rows[a], ct)
        rank2_blocks.append(rk)
        cnt_blocks.append(ct)
    rank2 = jnp.concatenate(rank2_blocks, axis=0)
    cnt = jnp.concatenate(cnt_blocks, axis=0)

    count = lambda flags: jnp.sum(functools.reduce(jnp.add, flags), axis=0, keepdims=True)
    strict = lambda v: functools.reduce(jnp.logical_and, [v[k] > v[k + 1] for k in range(kk - 1)])[0:1]
    ok = (strict(v1) & strict(v2) & strict(top)
          & (count([jnp.where(blk >= v1[kk - 1], 1.0, 0.0) for blk in b1]) == float(kk))
          & (count([jnp.where(blk >= v2[kk - 1], 1.0, 0.0) for blk in b2]) == float(kk))
          & (functools.reduce(jnp.add, n_rows) == float(kk)))
    return rank2, cnt, zsum, v1[0][0:1], v2[0][0:1], ok


def _peer_route_kernel(q_ref, k1_ref, k2_ref, rank2_ref, cnt_ref, e1_ref, e2_ref, s1_ref, s2_ref):
    def split(v):
        hi = v.astype(BF16)
        return hi, (v - hi.astype(F32)).astype(BF16)

    def scores(keys, qs):
        k_hi, k_lo = split(keys)
        q_hi, q_lo = split(qs)
        return _dot_nt(k_hi, q_hi) + (_dot_nt(k_hi, q_lo) + _dot_nt(k_lo, q_hi))

    qh = q_ref[...]
    s1_ref[...] = scores(k1_ref[0], qh[:, :PEER_HALF])
    s2_ref[...] = scores(k2_ref[0], qh[:, PEER_HALF:])

    width = min(2 * LANE, s1_ref.shape[1])

    def column(ci, carry):
        cols = pl.ds(pl.multiple_of(ci * width, width), width)
        s1 = s1_ref[:, cols]
        s2 = s2_ref[:, cols]

        def emit(rank2, cnt, zsum, m1, m2):
            rank2_ref[0, :, cols] = rank2.astype(BF16)
            cnt_ref[0, :, cols] = cnt
            e1_ref[0, :, cols] = jnp.exp(s1 - m1)
            e2_ref[0, :, cols] = (jnp.exp(s2 - m2) / zsum).astype(BF16)

        *fast, ok = _route_columns_sorted(s1, s2)
        emit(*fast)
        n_bad = jnp.sum(jnp.where(ok, 0.0, 1.0))

        @pl.when(n_bad > 0.0)
        def _():
            emit(*_route_columns_exact(s1, s2))

        return carry

    lax.fori_loop(0, s1_ref.shape[1] // width, column, 0)


def _peer_route(q, k1, k2, tq=1024):
    t, _ = q.shape
    nh, nk, hd = k1.shape
    out = lambda dt: jax.ShapeDtypeStruct((nh, nk, t), dt)
    spec = pl.BlockSpec((1, nk, tq), lambda i, h: (h, 0, i))
    return pl.pallas_call(
        _peer_route_kernel,
        grid=(t // tq, nh),
        in_specs=[pl.BlockSpec((tq, PEER_KEY_DIM), lambda i, h: (i, h)),
                  pl.BlockSpec((1, nk, hd), lambda i, h: (h, 0, 0)),
                  pl.BlockSpec((1, nk, hd), lambda i, h: (h, 0, 0))],
        out_specs=[spec, spec, spec, spec],
        out_shape=[out(BF16), out(F32), out(F32), out(BF16)],
        scratch_shapes=[pltpu.VMEM((nk, tq), F32), pltpu.VMEM((nk, tq), F32)],
        compiler_params=_cparams(("arbitrary", "arbitrary")),
        name="peer_route",
    )(q, k1, k2)


BF16_ROWS = 16


def _gate_weights(w_ref, rank2_ref, e2_ref, cnt_ref, e1_ref):
    nk = PEER_N_KEYS
    tm = w_ref.shape[1]
    for i_loc in range(w_ref.shape[0] // nk):
        w = None
        for h in range(PEER_HEADS):
            cnt_b = jnp.broadcast_to(cnt_ref[h, i_loc:i_loc + 1, :], (BF16_ROWS, tm)).astype(BF16)
            e1_b = jnp.broadcast_to(e1_ref[h, i_loc:i_loc + 1, :], (BF16_ROWS, tm)).astype(BF16)
            wh = jnp.where(rank2_ref[h] < cnt_b[None], e2_ref[h], jnp.zeros((), BF16)) * e1_b[None]
            w = wh if w is None else w + wh
        w_ref[i_loc * nk:(i_loc + 1) * nk, :] = w.reshape(nk, tm)


def _peer_dense_kernel(n_sub, h2_ref, u_ref, vt_ref, rank2_ref, e2_ref, cnt_ref, e1_ref,
                       x1_ref, gf_ref, gpost_ref, o_ref, acc_ref, h2t_ref, w_ref, *g_refs):
    j = pl.program_id(1)

    @pl.when(j == 0)
    def _():
        acc_ref[...] = jnp.zeros_like(acc_ref)
        h2t_ref[...] = h2_ref[...].T

    nk = PEER_N_KEYS
    ts = u_ref.shape[0] // n_sub
    ats = [jnp.dot(u_ref[s * ts:(s + 1) * ts, :], h2t_ref[...], preferred_element_type=F32)
           for s in range(n_sub)]
    _gate_weights(w_ref, rank2_ref, e2_ref, cnt_ref, e1_ref)
    for s in range(n_sub):
        at = ats[s]
        for il in range(ts // nk):
            rows = slice(s * ts + il * nk, s * ts + (il + 1) * nk)
            a = at[il * nk:(il + 1) * nk, :]
            gelu = 0.5 * a * (1.0 + lax.erf(a * (1.0 / math.sqrt(2.0))))
            g_refs[s][il * nk:(il + 1) * nk, :] = gelu.astype(BF16) * w_ref[rows, :]
        acc_ref[...] += jnp.dot(vt_ref[:, s * ts:(s + 1) * ts], g_refs[s][...], preferred_element_type=F32)

    @pl.when(j == pl.num_programs(1) - 1)
    def _():
        y = acc_ref[...].T
        o_ref[...] = x1_ref[...] + gf_ref[0] * _rms(y, gpost_ref[...])


def _peer_dense(h2, u16, vt16, rank2, cnt, e1, e2, x1, gf, g_post, seq_len, tm=512, ts=256, n_sub=4):
    t, d = h2.shape
    ne = u16.shape[0]
    nh, nk, _ = cnt.shape
    te = ts * n_sub
    i_per_step = te // nk
    n_steps = ne // te
    aux_j = pl.BlockSpec((nh, nk // BF16_ROWS, BF16_ROWS, tm), lambda i, j: (0, 0, 0, i))
    aux_i = pl.BlockSpec((nh, i_per_step, tm), lambda i, j: (0, j, i))
    blocks_per_seq = seq_len // tm
    return pl.pallas_call(
        functools.partial(_peer_dense_kernel, n_sub),
        grid=(t // tm, n_steps),
        in_specs=[pl.BlockSpec((tm, d), lambda i, j: (i, 0)),
                  pl.BlockSpec((te, d), lambda i, j: (j, 0)),
                  pl.BlockSpec((d, te), lambda i, j: (0, j)),
                  aux_j, aux_j, aux_i, aux_i,
                  pl.BlockSpec((tm, d), lambda i, j: (i, 0), pipeline_mode=pl.Buffered(1)),
                  pl.BlockSpec((1, 1, d), lambda i, j: (i // blocks_per_seq, 0, 0)),
                  pl.BlockSpec((1, d), lambda i, j: (0, 0))],
        out_specs=pl.BlockSpec((tm, d), lambda i, j: (i, 0)),
        out_shape=jax.ShapeDtypeStruct((t, d), F32),
        scratch_shapes=[pltpu.VMEM((d, tm), F32), pltpu.VMEM((d, tm), BF16), pltpu.VMEM((te, tm), BF16)]
        + [pltpu.VMEM((ts, tm), BF16) for _ in range(n_sub)],
        compiler_params=_cparams(("arbitrary", "arbitrary")),
        name="peer_dense",
    )(h2, u16, vt16, rank2, e2, cnt, e1, x1, gf, g_post)


def _dft_tables(n):
    k = np.arange(n, dtype=np.int64)
    ph = (np.outer(k, k) % n).astype(np.float64) * (2.0 * np.pi / n)
    return np.cos(ph), np.sin(ph)


def kernel(x, c, ctx, c_ctx, w_mod, b_mod, g_pre_mix, g_post_mix, g_pre_ffn, g_post_ffn, w_in, w_fmix, conv_w, conv_b, dt_bias_f, dt_bias_b, a_log_f, a_log_b, d_skip_f, d_skip_b, ssm_norm_w, w_out, w_query, sub_keys_1, sub_keys_2, expert_u, expert_v):
    bsz, seq, d = x.shape
    ctx_len = ctx.shape[1]
    layer = 0
    n_heads = dt_bias_f.shape[1]
    d_ssm = n_heads * SSM_HEAD_DIM
    d_f = d - d_ssm
    gc = d_f // N_FOURIER_GROUPS
    d_xbc = conv_w.shape[2]
    assert w_mod.shape[0] == 1, "single-layer kernel"

    cond = jnp.zeros((8, d), F32).at[:bsz].set(c).at[bsz].set(c_ctx)
    mod = _adaln(cond, w_mod[layer], b_mod[layer])
    mods = [mod[:, i * d:(i + 1) * d] for i in range(N_MOD)]
    sm_x, cm_x, gm_x, sf_x, cf_x, gf_x = [m[:bsz, None, :] for m in mods]
    sm_c = jnp.broadcast_to(mods[0][bsz][None, None, :], (bsz, 1, d))
    cm_c = jnp.broadcast_to(mods[1][bsz][None, None, :], (bsz, 1, d))

    cos_l, sin_l = _dft_tables(seq)
    cos_c, sin_c = _dft_tables(gc)
    scale = 1.0 / math.sqrt(seq * gc)
    dt_pad = LANE
    col_z = 2 * d_f
    col_xbc = col_z + d_ssm
    col_dt = col_xbc + d_xbc
    n_all = col_dt + dt_pad
    assert (d_ssm + d_xbc) % gc == 0, "z/xBC columns must fill whole weight tiles"
    w_all = _build_weights(w_in[layer].T, w_fmix[layer], jnp.asarray(cos_c * scale, F32),
                           jnp.asarray(sin_c * scale, F32), pl.cdiv(n_all, gc) * gc)
    dft = jnp.asarray(np.concatenate([cos_l, -sin_l], axis=1), F32).astype(BF16)

    pad32 = lambda f, b_: jnp.zeros((1, LANE), F32).at[0, :n_heads].set(f).at[0, n_heads:2 * n_heads].set(b_)
    bias_row = pad32(dt_bias_f[layer], dt_bias_b[layer])
    a_row = pad32(-jnp.exp(a_log_f[layer]), -jnp.exp(a_log_b[layer]))

    w_ctx = w_all[:, col_xbc:n_all]
    proj_c = _inproj(ctx, g_pre_mix[layer], sm_c, cm_c, w_ctx, 0, n_all - col_xbc, tm=ctx_len, tn=n_all - col_xbc)
    xbc_c = _conv_silu(proj_c, 0, conv_w[layer], conv_b[layer])
    zero_state = jnp.zeros((bsz, 2, n_heads // 2, SSM_STATE, 2 * SSM_HEAD_DIM), F32)
    (fin_c,) = _ssd(proj_c, d_xbc, xbc_c, bias_row, a_row, zero_state, emit_y=False)

    proj_x = _inproj(x, g_pre_mix[layer], sm_x, cm_x, w_all, 0, n_all, tm=min(256, seq), tn=n_all)
    o_f = _seqdft(dft, proj_x, d_f)
    xbc_x = _conv_silu(proj_x, col_xbc, conv_w[layer], conv_b[layer])
    y_f, y_b, _ = _ssd(proj_x, col_dt, xbc_x, bias_row, a_row, fin_c, emit_y=True)

    dsk = jnp.repeat(d_skip_f[layer] + d_skip_b[layer], SSM_HEAD_DIM)[None, :]
    x1, h2, q = _outproj(o_f, y_f, y_b, xbc_x, proj_x, col_z, x, dsk, ssm_norm_w[layer][None, :],
                         w_out[layer].astype(BF16), g_post_mix[layer][None, :], gm_x,
                         g_pre_ffn[layer][None, :], cf_x, sf_x, w_query[layer].astype(BF16))

    t = bsz * seq
    rank2, cnt, e1, e2 = _peer_route(q.reshape(t, -1), sub_keys_1[layer], sub_keys_2[layer],
                                     tq=min(2048, t))
    tiles = lambda a: a.reshape(a.shape[0], a.shape[1] // BF16_ROWS, BF16_ROWS, t)
    u16 = expert_u[layer].astype(BF16)
    vt16 = expert_v[layer].T.astype(BF16)
    out = _peer_dense(h2.reshape(t, d), u16, vt16, tiles(rank2), cnt, e1, tiles(e2), x1.reshape(t, d), gf_x,
                      g_post_ffn[layer][None, :], seq)
    return out.reshape(bsz, seq, d)
```

```python
import functools
import math

import jax
import jax.numpy as jnp
import numpy as np
from jax import lax
from jax.experimental import pallas as pl
from jax.experimental.pallas import tpu as pltpu

F32 = jnp.float32
BF16 = jnp.bfloat16

EPS = 1e-6
N_MOD = 6
N_FOURIER_GROUPS = 4
SSM_HEAD_DIM = 64
SSM_GROUPS = 2
SSM_STATE = 128
SSM_CHUNK = 128
PEER_HEADS = 8
PEER_TOPK = 16
PEER_N_KEYS = 128
PEER_KEY_DIM = 128
PEER_HALF = PEER_KEY_DIM // 2

LANE = 128
VMEM_LIMIT = 56 * 1024 * 1024


def _cparams(sem):
    return pltpu.CompilerParams(dimension_semantics=sem, vmem_limit_bytes=VMEM_LIMIT)


def _bdot(a, b):
    return jnp.dot(a.astype(BF16), b.astype(BF16), preferred_element_type=F32)


def _dot_nt(a, b, precision=None):
    return lax.dot_general(a, b, (((1,), (1,)), ((), ())), precision=precision,
                           preferred_element_type=F32)


def _rms(u, g):
    return u * lax.rsqrt(jnp.mean(u * u, axis=-1, keepdims=True) + EPS) * g


def _silu(u):
    return u * (1.0 / (1.0 + jnp.exp(-u)))


def _adaln_kernel(c_ref, w_ref, b_ref, o_ref):
    a = _silu(c_ref[...])
    o_ref[...] = _bdot(a, w_ref[...]) + b_ref[...]


def _adaln(cond, w_mod, b_mod, tn=1024):
    rows, d = cond.shape
    n = w_mod.shape[1]
    return pl.pallas_call(
        _adaln_kernel,
        grid=(n // tn,),
        in_specs=[pl.BlockSpec((rows, d), lambda j: (0, 0)),
                  pl.BlockSpec((d, tn), lambda j: (0, j)),
                  pl.BlockSpec((1, tn), lambda j: (0, j))],
        out_specs=pl.BlockSpec((rows, tn), lambda j: (0, j)),
        out_shape=jax.ShapeDtypeStruct((rows, n), F32),
        compiler_params=_cparams(("arbitrary",)),
        name="adaln",
    )(cond, w_mod, b_mod.reshape(1, n))


def _weights_kernel(ng, n_copy, wt_ref, tail_ref, fm_ref, cc_ref, sc_ref, o_ref):
    hp = lax.Precision.HIGHEST
    g = pl.program_id(0)

    def fold(tbl_ref):
        m = jnp.dot(tbl_ref[...], fm_ref[0], precision=hp, preferred_element_type=F32)
        o_ref[...] = jnp.dot(wt_ref[...].T, m, precision=hp, preferred_element_type=F32).astype(BF16)

    pl.when(g < ng)(lambda: fold(cc_ref))
    pl.when((g >= ng) & (g < 2 * ng))(lambda: fold(sc_ref))

    @pl.when((g >= 2 * ng) & (g < 2 * ng + n_copy))
    def _():
        o_ref[...] = wt_ref[...].T.astype(BF16)

    @pl.when(g == 2 * ng + n_copy)
    def _():
        tail = tail_ref[...]
        fill = jnp.zeros((o_ref.shape[1] - tail.shape[0], tail.shape[1]), F32)
        o_ref[...] = jnp.concatenate([tail, fill], axis=0).T.astype(BF16)

    @pl.when(g > 2 * ng + n_copy)
    def _():
        o_ref[...] = jnp.zeros_like(o_ref)


def _build_weights(w_in_t, w_fmix, cos_c, sin_c, n_all):
    n_in, d = w_in_t.shape
    ng, gc, _ = w_fmix.shape
    n_copy = (n_in - ng * gc) // gc
    tail_rows = n_in - (ng + n_copy) * gc
    assert 0 < tail_rows < gc and tail_rows % SUBLANES == 0 and ((ng + n_copy) * gc) % tail_rows == 0
    n_tiles = n_all // gc
    return pl.pallas_call(
        functools.partial(_weights_kernel, ng, n_copy),
        grid=(n_tiles,),
        in_specs=[pl.BlockSpec((gc, d), lambda g: (jnp.where(g < 2 * ng, g % ng, jnp.minimum(g - ng, ng + n_copy - 1)), 0)),
                  pl.BlockSpec((tail_rows, d), lambda g: ((ng + n_copy) * gc // tail_rows, 0)),
                  pl.BlockSpec((1, gc, gc), lambda g: (jnp.where(g < 2 * ng, g % ng, 0), 0, 0)),
                  pl.BlockSpec((gc, gc), lambda g: (0, 0)),
                  pl.BlockSpec((gc, gc), lambda g: (0, 0))],
        out_specs=pl.BlockSpec((d, gc), lambda g: (0, g)),
        out_shape=jax.ShapeDtypeStruct((d, n_tiles * gc), BF16),
        compiler_params=_cparams(("arbitrary",)),
        name="inproj_weights",
    )(w_in_t, w_in_t, w_fmix, cos_c, sin_c)


def _inproj_kernel(x_ref, g_ref, sh_ref, sc_ref, w_ref, o_ref, h_ref=None):
    modulated = lambda rows: (_rms(x_ref[0, rows, :], g_ref[...]) * (1.0 + sc_ref[0]) + sh_ref[0]).astype(BF16)
    if h_ref is None:
        o_ref[0] = jnp.dot(modulated(slice(None)), w_ref[...], preferred_element_type=F32)
        return

    @pl.when(pl.program_id(2) == 0)
    def _():
        slab = min(LANE, h_ref.shape[0])

        def body(r, carry):
            rows = pl.ds(pl.multiple_of(r * slab, slab), slab)
            h_ref[rows, :] = modulated(rows)
            return carry

        lax.fori_loop(0, h_ref.shape[0] // slab, body, 0)

    o_ref[0] = jnp.dot(h_ref[...], w_ref[...], preferred_element_type=F32)


def _inproj(xin, g, shift, scale, w_all, col0, ncols, tm, tn):
    b, l, d = xin.shape
    jb = col0 // tn
    w_mode = dict(pipeline_mode=pl.Buffered(1)) if ncols == tn else {}
    return pl.pallas_call(
        _inproj_kernel,
        grid=(b, l // tm, ncols // tn),
        in_specs=[pl.BlockSpec((1, tm, d), lambda bi, i, j: (bi, i, 0)),
                  pl.BlockSpec((1, d), lambda bi, i, j: (0, 0)),
                  pl.BlockSpec((1, 1, d), lambda bi, i, j: (bi, 0, 0)),
                  pl.BlockSpec((1, 1, d), lambda bi, i, j: (bi, 0, 0)),
                  pl.BlockSpec((d, tn), lambda bi, i, j: (0, j + jb), **w_mode)],
        out_specs=pl.BlockSpec((1, tm, tn), lambda bi, i, j: (bi, i, j)),
        out_shape=jax.ShapeDtypeStruct((b, l, ncols), F32),
        scratch_shapes=[] if ncols == tn else [pltpu.VMEM((tm, d), BF16)],
        compiler_params=_cparams(("arbitrary", "arbitrary", "arbitrary")),
        name="inproj",
    )(xin, g.reshape(1, d), shift, scale, w_all)


def _seqdft_kernel(a_ref, x_ref, o_ref, acc_ref):
    k = pl.program_id(1)

    @pl.when(k == 0)
    def _():
        acc_ref[...] = jnp.zeros_like(acc_ref)

    acc_ref[...] += jnp.dot(a_ref[...], x_ref[0].astype(BF16), preferred_element_type=F32)

    @pl.when(k == pl.num_programs(1) - 1)
    def _():
        o_ref[0] = acc_ref[...].astype(BF16)


def _seqdft(dft, proj, n_f, tk=1024):
    b, l, _ = proj.shape
    tk = min(tk, l)
    kl = l // tk
    return pl.pallas_call(
        _seqdft_kernel,
        grid=(b, 2 * kl),
        in_specs=[pl.BlockSpec((l, tk), lambda bi, k: (0, k)),
                  pl.BlockSpec((1, tk, n_f), lambda bi, k: (bi, k % kl, k // kl))],
        out_specs=pl.BlockSpec((1, l, n_f), lambda bi, k: (bi, 0, 0)),
        out_shape=jax.ShapeDtypeStruct((b, l, n_f), BF16),
        scratch_shapes=[pltpu.VMEM((l, n_f), F32)],
        compiler_params=_cparams(("arbitrary", "arbitrary")),
        name="seq_dft",
    )(dft, proj)


def _conv_kernel(u_ref, w_ref, b_ref, o_ref):
    u = u_ref[0]
    l = u.shape[0]
    row = lax.broadcasted_iota(jnp.int32, u.shape, 0)
    prev = jnp.where(row == 0, 0.0, pltpu.roll(u, 1, 0))
    nxt = jnp.where(row == l - 1, 0.0, pltpu.roll(u, l - 1, 0))
    w = w_ref[...]
    o_ref[0] = _silu(prev * w[0:1] + u * w[1:2] + nxt * w[2:3] + b_ref[...])


def _conv_silu(proj, col0, conv_w, conv_b, tc=512):
    b, l, _ = proj.shape
    n = conv_w.shape[1]
    jb = col0 // tc
    return pl.pallas_call(
        _conv_kernel,
        grid=(b, n // tc),
        in_specs=[pl.BlockSpec((1, l, tc), lambda bi, j: (bi, 0, j + jb)),
                  pl.BlockSpec((3, tc), lambda bi, j: (0, j)),
                  pl.BlockSpec((1, tc), lambda bi, j: (0, j))],
        out_specs=pl.BlockSpec((1, l, tc), lambda bi, j: (bi, 0, j)),
        out_shape=jax.ShapeDtypeStruct((b, l, n), F32),
        compiler_params=_cparams(("arbitrary", "arbitrary")),
        name="conv_silu",
    )(proj, conv_w, conv_b.reshape(1, n))


def _ssd_direction(reverse, dt_raw, bias, a_neg, xs_ref, bm_ref, cm_ref, st_ref, y_ref, n_heads):
    hp = lax.Precision.HIGHEST
    t = SSM_CHUNK
    p2 = 2 * SSM_HEAD_DIM
    col0 = n_heads if reverse else 0
    r = lax.broadcasted_iota(jnp.int32, (t, t), 0)
    c = lax.broadcasted_iota(jnp.int32, (t, t), 1)
    mask = (r <= c) if reverse else (r >= c)
    tri = mask.astype(F32)
    tri_t = ((r >= c) if reverse else (r <= c)).astype(F32)
    first_y = lax.broadcasted_iota(jnp.int32, (t, p2), 1) < SSM_HEAD_DIM
    first_s = lax.broadcasted_iota(jnp.int32, (SSM_STATE, p2), 1) < SSM_HEAD_DIM

    z = dt_raw + bias
    dtv = jnp.maximum(z, 0.0) + jnp.log(1.0 + jnp.exp(-jnp.abs(z)))
    a = dtv * a_neg
    a_t = a.T
    dtv_t = dtv.T
    cs = jnp.dot(tri, a, precision=hp, preferred_element_type=F32)
    cs_t = jnp.dot(a_t, tri_t, precision=hp, preferred_element_type=F32)
    w_t = dtv_t * jnp.exp(jnp.sum(a_t, axis=1, keepdims=True) - cs_t)
    etot = jnp.exp(jnp.sum(a, axis=0, keepdims=True))

    ppg = n_heads // SSM_GROUPS // 2
    for g in range(SSM_GROUPS):
        bg = bm_ref[0, :, g * SSM_STATE:(g + 1) * SSM_STATE]
        cg16 = cm_ref[0, :, g * SSM_STATE:(g + 1) * SSM_STATE].astype(BF16)
        gmat = _dot_nt(cg16, bg.astype(BF16))
        bgt = bg.T
        if y_ref is not None:
            st_g16 = jnp.concatenate([st_ref[g * ppg + k] for k in range(ppg)], axis=1).astype(BF16)
            yoff_g = jnp.dot(cg16, st_g16, preferred_element_type=F32)
        for k in range(ppg):
            pi = g * ppg + k
            xs16 = xs_ref[0, :, pi * p2:(pi + 1) * p2].astype(BF16)
            ys, ss, ecols, etots = [], [], [], []
            for e in range(2):
                j = col0 + 2 * pi + e
                col_b = jnp.broadcast_to(cs[:, j:j + 1], (t, t))
                lmat = jnp.exp(jnp.where(mask, col_b - cs_t[j:j + 1, :], -jnp.inf))
                m16 = (gmat * lmat * dtv_t[j:j + 1, :]).astype(BF16)
                if y_ref is not None:
                    ys.append(jnp.dot(m16, xs16, preferred_element_type=F32))
                    ecols.append(jnp.exp(col_b))
                bw16 = (bgt * w_t[j:j + 1, :]).astype(BF16)
                ss.append(jnp.dot(bw16, xs16, preferred_element_type=F32))
                etots.append(etot[:, j:j + 1])
            if y_ref is not None:
                y_off = yoff_g[:, k * p2:(k + 1) * p2] * jnp.where(first_y, ecols[0], ecols[1])
                y_ref[0, :, pi * p2:(pi + 1) * p2] = jnp.where(first_y, ys[0], ys[1]) + y_off
            st_ref[pi] = (jnp.where(first_s, etots[0], etots[1]) * st_ref[pi]
                          + jnp.where(first_s, ss[0], ss[1]))


def _ssd_kernel(n_heads, emit_y, *refs):
    (dtf_ref, dtb_ref, bias_ref, a_ref, xsf_ref, xsb_ref, bmf_ref, bmb_ref, cmf_ref, cmb_ref,
     init_ref) = refs[:11]
    if emit_y:
        yf_ref, yb_ref, fin_ref, stf_ref, stb_ref = refs[11:]
    else:
        yf_ref = yb_ref = None
        fin_ref, stf_ref, stb_ref = refs[11:]
    ci = pl.program_id(1)

    @pl.when(ci == 0)
    def _():
        stf_ref[...] = init_ref[0, 0]
        stb_ref[...] = init_ref[0, 1]

    bias = bias_ref[...]
    a_neg = a_ref[...]
    _ssd_direction(False, dtf_ref[0], bias, a_neg, xsf_ref, bmf_ref, cmf_ref, stf_ref, yf_ref, n_heads)
    _ssd_direction(True, dtb_ref[0], bias, a_neg, xsb_ref, bmb_ref, cmb_ref, stb_ref, yb_ref, n_heads)

    @pl.when(ci == pl.num_programs(1) - 1)
    def _():
        fin_ref[0, 0] = stf_ref[...]
        fin_ref[0, 1] = stb_ref[...]


def _ssd(proj, dt_col0, xbc, bias_row, a_row, init, emit_y):
    b, l, _ = proj.shape
    d_ssm = xbc.shape[2] - 2 * SSM_GROUPS * SSM_STATE
    d_bc = SSM_GROUPS * SSM_STATE
    n_heads = d_ssm // SSM_HEAD_DIM
    t = SSM_CHUNK
    nc = l // t
    dtb = dt_col0 // LANE
    fwd = lambda bi, c: (bi, c, 0)
    bwd = lambda bi, c: (bi, nc - 1 - c, 0)
    off = lambda f, o: (lambda bi, c: f(bi, c)[:2] + (o,))
    st_shape = (n_heads // 2, SSM_STATE, 2 * SSM_HEAD_DIM)
    in_specs = [
        pl.BlockSpec((1, t, LANE), off(fwd, dtb)), pl.BlockSpec((1, t, LANE), off(bwd, dtb)),
        pl.BlockSpec((1, LANE), lambda bi, c: (0, 0)), pl.BlockSpec((1, LANE), lambda bi, c: (0, 0)),
        pl.BlockSpec((1, t, d_ssm), fwd), pl.BlockSpec((1, t, d_ssm), bwd),
        pl.BlockSpec((1, t, d_bc), off(fwd, d_ssm // d_bc)), pl.BlockSpec((1, t, d_bc), off(bwd, d_ssm // d_bc)),
        pl.BlockSpec((1, t, d_bc), off(fwd, d_ssm // d_bc + 1)), pl.BlockSpec((1, t, d_bc), off(bwd, d_ssm // d_bc + 1)),
        pl.BlockSpec((1, 2) + st_shape, lambda bi, c: (bi, 0, 0, 0, 0)),
    ]
    fin_spec = pl.BlockSpec((1, 2) + st_shape, lambda bi, c: (bi, 0, 0, 0, 0))
    fin_shape = jax.ShapeDtypeStruct((b, 2) + st_shape, F32)
    if emit_y:
        y_shape = jax.ShapeDtypeStruct((b, l, d_ssm), F32)
        out_specs = [pl.BlockSpec((1, t, d_ssm), fwd), pl.BlockSpec((1, t, d_ssm), bwd), fin_spec]
        out_shape = [y_shape, y_shape, fin_shape]
    else:
        out_specs = [fin_spec]
        out_shape = [fin_shape]
    return pl.pallas_call(
        functools.partial(_ssd_kernel, n_heads, emit_y),
        grid=(b, nc),
        in_specs=in_specs,
        out_specs=out_specs,
        out_shape=out_shape,
        scratch_shapes=[pltpu.VMEM(st_shape, F32), pltpu.VMEM(st_shape, F32)],
        compiler_params=_cparams(("arbitrary", "arbitrary")),
        name="ssd_scan_y" if emit_y else "ssd_scan_state",
    )(proj, proj, bias_row, a_row, xbc, xbc, xbc, xbc, xbc, xbc, init)


def _outproj_kernel(of_ref, yf_ref, yb_ref, xs_ref, z_ref, x_ref, dsk_ref, nw_ref, wo_ref, gpost_ref,
                    gm_ref, gpre_ref, cf_ref, sf_ref, wq_ref, x1_ref, h2t_ref, q_ref):
    gw = yf_ref.shape[2] // SSM_GROUPS
    nw = nw_ref[...]
    n_slabs = 2
    rs = x_ref.shape[1] // n_slabs
    yxs = []
    for s in range(n_slabs):
        rows = slice(s * rs, (s + 1) * rs)
        y = yf_ref[0, rows, :] + yb_ref[0, rows, :] + dsk_ref[...] * xs_ref[0, rows, :]
        y = y * _silu(z_ref[0, rows, :])
        parts = [_rms(y[:, g * gw:(g + 1) * gw], nw[:, g * gw:(g + 1) * gw]).astype(BF16)
                 for g in range(SSM_GROUPS)]
        mixed = jnp.concatenate([of_ref[0, rows, :]] + parts, axis=1)
        yxs.append(jnp.dot(mixed, wo_ref[...], preferred_element_type=F32))
    for s in range(n_slabs):
        rows = slice(s * rs, (s + 1) * rs)
        yx = yxs[s]
        x1 = x_ref[0, rows, :] + gm_ref[0] * _rms(yx, gpost_ref[...])
        x1_ref[0, rows, :] = x1
        h2 = (_rms(x1, gpre_ref[...]) * (1.0 + cf_ref[0]) + sf_ref[0]).astype(BF16)
        h2t_ref[:, rows] = h2.T
        q_ref[0, rows, :] = jnp.dot(h2, wq_ref[...], preferred_element_type=F32)


def _outproj(of, yf, yb, xbc, proj, z_col0, x, dsk, nw, w_out, g_post, gm, g_pre, cf, sf, w_query, tm=256):
    b, l, d = x.shape
    d_f = of.shape[2]
    d_s = yf.shape[2]
    nq = w_query.shape[1]
    zb = z_col0 // d_s
    row = lambda bi, i: (bi, i, 0)
    vec = lambda n: pl.BlockSpec((1, n), lambda bi, i: (0, 0))
    bvec = lambda n: pl.BlockSpec((1, 1, n), lambda bi, i: (bi, 0, 0))
    return pl.pallas_call(
        _outproj_kernel,
        grid=(b, l // tm),
        in_specs=[pl.BlockSpec((1, tm, d_f), row), pl.BlockSpec((1, tm, d_s), row), pl.BlockSpec((1, tm, d_s), row),
                  pl.BlockSpec((1, tm, d_s), row), pl.BlockSpec((1, tm, d_s), lambda bi, i: (bi, i, zb)),
                  pl.BlockSpec((1, tm, d), row), vec(d_s), vec(d_s),
                  pl.BlockSpec((d_f + d_s, d), lambda bi, i: (0, 0)), vec(d), bvec(d), vec(d), bvec(d), bvec(d),
                  pl.BlockSpec((d, nq), lambda bi, i: (0, 0))],
        out_specs=[pl.BlockSpec((1, tm, d), row), pl.BlockSpec((d, tm), lambda bi, i: (0, bi * (l // tm) + i)),
                   pl.BlockSpec((1, tm, nq), row)],
        out_shape=[jax.ShapeDtypeStruct((b, l, d), F32), jax.ShapeDtypeStruct((d, b * l), BF16),
                   jax.ShapeDtypeStruct((b, l, nq), F32)],
        compiler_params=_cparams(("arbitrary", "arbitrary")),
        name="outproj_prenorm_query",
    )(of, yf, yb, xbc, proj, x, dsk, nw, w_out, g_post, gm, g_pre, cf, sf, w_query)


SUBLANES = 8


def _top16_ranked(s):
    n = s.shape[0]
    iota = lax.broadcasted_iota(jnp.int32, s.shape, 0).astype(F32)
    cur = s
    rank = jnp.full(s.shape, float(PEER_TOPK), F32)
    vals = []
    for k in range(PEER_TOPK):
        m = jnp.max(cur, axis=0, keepdims=True)
        idx = jnp.min(jnp.where(cur == m, iota, float(n)), axis=0, keepdims=True)
        hit = iota == idx
        rank = jnp.where(hit, float(k), rank)
        cur = jnp.where(hit, -jnp.inf, cur)
        vals.append(m)
    return vals, rank


def _candidate_pieces(v1, v2, sub):
    def stack8(vals):
        out = jnp.zeros(sub.shape, F32)
        for k, v in enumerate(vals):
            out = jnp.where(sub == k, v, out)
        return out

    v2_lo, v2_hi, v1_hi = stack8(v2[:8]), stack8(v2[8:]), stack8(v1[8:])
    return [v1[0] + v2_lo, v1[0] + v2_hi] + [v1[a] + v2_lo for a in range(1, 8)] + [v1_hi + v2[0]]


def _row_counts(sels, sub):
    n_rows = [jnp.sum(sels[0] + sels[1], axis=0, keepdims=True)]
    n_rows += [jnp.sum(sels[a + 1], axis=0, keepdims=True) for a in range(1, 8)]
    n_rows += [jnp.sum(jnp.where(sub == r, sels[9], 0.0), axis=0, keepdims=True) for r in range(8)]
    return n_rows


def _route_columns_exact(s1, s2):
    kk = PEER_TOPK
    v1, rank1 = _top16_ranked(s1)
    v2, rank2 = _top16_ranked(s2)
    sub = lax.broadcasted_iota(jnp.int32, (SUBLANES, s1.shape[1]), 0)
    subf = sub.astype(F32)
    cands = _candidate_pieces(v1, v2, sub)
    poses = [subf, subf + 8.0] + [a * kk + subf for a in range(1, 8)] + [(subf + 8.0) * kk]
    sels = [jnp.zeros(sub.shape, F32) for _ in cands]
    m0 = v1[0] + v2[0]
    zsum = jnp.zeros_like(m0)
    for _ in range(kk):
        m = jnp.max(functools.reduce(jnp.maximum, cands), axis=0, keepdims=True)
        firsts = [jnp.where(cd == m, ps, float(kk * kk)) for cd, ps in zip(cands, poses)]
        p = jnp.min(functools.reduce(jnp.minimum, firsts), axis=0, keepdims=True)
        hits = [ps == p for ps in poses]
        sels = [jnp.where(ht, 1.0, sl) for ht, sl in zip(hits, sels)]
        cands = [jnp.where(ht, -jnp.inf, cd) for ht, cd in zip(hits, cands)]
        zsum = zsum + jnp.exp(m - m0)
    n_rows = _row_counts(sels, sub)
    cnt = jnp.zeros(rank1.shape, F32)
    for a in range(kk):
        cnt = jnp.where(rank1 == float(a), n_rows[a], cnt)
    return rank2, cnt, zsum, v1[0], v2[0]


def _sorted_top16(blocks):
    v = list(blocks)
    n = len(v)

    def exchange(i, l, descending):
        hi, lo = jnp.maximum(v[i], v[l]), jnp.minimum(v[i], v[l])
        v[i], v[l] = (hi, lo) if descending else (lo, hi)

    k = 2
    while k <= n:
        j = k // 2
        while j >= 1:
            for i in range(n):
                if i ^ j > i:
                    exchange(i, i ^ j, (i & k) == 0)
            j //= 2
        k *= 2
    for shift in (4, 2, 1):
        w = [pltpu.roll(x, shift, 0) for x in v]
        v = [jnp.maximum(v[r], w[n - 1 - r]) for r in range(n)]
        j = n // 2
        while j >= 1:
            for i in range(n):
                if i ^ j > i:
                    exchange(i, i ^ j, True)
            j //= 2
    return v


def _route_columns_sorted(s1, s2):
    kk = PEER_TOPK
    nb = s1.shape[0] // SUBLANES
    tcols = s1.shape[1]
    blocks = lambda s: [s[SUBLANES * r:SUBLANES * (r + 1), :] for r in range(nb)]
    b1, b2 = blocks(s1), blocks(s2)
    v1, v2 = _sorted_top16(b1), _sorted_top16(b2)
    sub = lax.broadcasted_iota(jnp.int32, (SUBLANES, tcols), 0)
    cands = _candidate_pieces(v1, v2, sub)
    neg = jnp.full((SUBLANES, tcols), -jnp.inf, F32)
    top = _sorted_top16(cands + [neg] * (nb - len(cands)))
    tau = top[kk - 1]
    m0 = v1[0] + v2[0]
    picked = [cd >= tau for cd in cands]
    sels = [jnp.where(pk, 1.0, 0.0) for pk in picked]
    zparts = [jnp.where(pk, jnp.exp(cd - m0), 0.0) for pk, cd in zip(picked, cands)]
    zsum = jnp.sum(functools.reduce(jnp.add, zparts), axis=0, keepdims=True)
    n_rows = _row_counts(sels, sub)

    def rank_in(v, blk):
        gt = lambda pivot: pivot > blk
        c1 = gt(v[7])
        c2 = gt(jnp.where(c1, v[11], v[3]))
        c3 = gt(jnp.where(c1, jnp.where(c2, v[13], v[9]), jnp.where(c2, v[5], v[1])))
        even = [jnp.where(c3, v[4 * q + 2], v[4 * q]) for q in range(4)]
        c4 = gt(jnp.where(c1, jnp.where(c2, even[3], even[2]), jnp.where(c2, even[1], even[0])))
        c5 = gt(v[15])
        bit = lambda c, val: jnp.where(c, val, 0.0)
        return (bit(c1, 8.0) + bit(c2, 4.0)) + (bit(c3, 2.0) + bit(c4, 1.0)) + bit(c5, 1.0)

    rank2_blocks, cnt_blocks = [], []
    for blk1, blk2 in zip(b1, b2):
        rk = rank_in(v2, blk2)
        ct = jnp.zeros(blk1.shape, F32)
        for a in range(kk):
            ct = jnp.where(blk1 == v1[a], n_rows[a], ct)
        rank2_blocks.append(rk)
        cnt_blocks.append(ct)
    rank2 = jnp.concatenate(rank2_blocks, axis=0)
    cnt = jnp.concatenate(cnt_blocks, axis=0)

    count = lambda flags: jnp.sum(functools.reduce(jnp.add, flags), axis=0, keepdims=True)
    strict = lambda v: functools.reduce(jnp.logical_and, [v[k] > v[k + 1] for k in range(kk - 1)])[0:1]
    ok = (strict(v1) & strict(v2) & strict(top)
          & (count([jnp.where(blk >= v1[kk - 1], 1.0, 0.0) for blk in b1]) == float(kk))
          & (count([jnp.where(blk >= v2[kk - 1], 1.0, 0.0) for blk in b2]) == float(kk))
          & (functools.reduce(jnp.add, n_rows) == float(kk)))
    return rank2, cnt, zsum, v1[0][0:1], v2[0][0:1], ok


def _peer_route_kernel(q_ref, k1_ref, k2_ref, rank2_ref, cnt_ref, e1_ref, e2_ref, s1_ref, s2_ref):
    def split(v):
        hi = v.astype(BF16)
        return hi, (v - hi.astype(F32)).astype(BF16)

    def scores(keys, qs):
        k_hi, k_lo = split(keys)
        q_hi, q_lo = split(qs)
        return _dot_nt(k_hi, q_hi) + (_dot_nt(k_hi, q_lo) + _dot_nt(k_lo, q_hi))

    qh = q_ref[...]
    s1_ref[...] = scores(k1_ref[0], qh[:, :PEER_HALF])
    s2_ref[...] = scores(k2_ref[0], qh[:, PEER_HALF:])

    width = min(2 * LANE, s1_ref.shape[1])

    def column(ci, carry):
        cols = pl.ds(pl.multiple_of(ci * width, width), width)
        s1 = s1_ref[:, cols]
        s2 = s2_ref[:, cols]

        def emit(rank2, cnt, zsum, m1, m2):
            rank2_ref[0, :, cols] = rank2.astype(BF16)
            cnt_ref[0, :, cols] = cnt
            e1_ref[0, :, cols] = jnp.exp(s1 - m1)
            e2_ref[0, :, cols] = (jnp.exp(s2 - m2) / zsum).astype(BF16)

        *fast, ok = _route_columns_sorted(s1, s2)
        emit(*fast)
        n_bad = jnp.sum(jnp.where(ok, 0.0, 1.0))

        @pl.when(n_bad > 0.0)
        def _():
            emit(*_route_columns_exact(s1, s2))

        return carry

    lax.fori_loop(0, s1_ref.shape[1] // width, column, 0)


def _peer_route(q, k1, k2, tq=1024):
    t, _ = q.shape
    nh, nk, hd = k1.shape
    out = lambda dt: jax.ShapeDtypeStruct((nh, nk, t), dt)
    spec = pl.BlockSpec((1, nk, tq), lambda i, h: (h, 0, i))
    return pl.pallas_call(
        _peer_route_kernel,
        grid=(t // tq, nh),
        in_specs=[pl.BlockSpec((tq, PEER_KEY_DIM), lambda i, h: (i, h)),
                  pl.BlockSpec((1, nk, hd), lambda i, h: (h, 0, 0)),
                  pl.BlockSpec((1, nk, hd), lambda i, h: (h, 0, 0))],
        out_specs=[spec, spec, spec, spec],
        out_shape=[out(BF16), out(F32), out(F32), out(BF16)],
        scratch_shapes=[pltpu.VMEM((nk, tq), F32), pltpu.VMEM((nk, tq), F32)],
        compiler_params=_cparams(("arbitrary", "arbitrary")),
        name="peer_route",
    )(q, k1, k2)


BF16_ROWS = 16


def _gate_weights(w_ref, rank2_ref, e2_ref, cnt_ref, e1_ref):
    nk = PEER_N_KEYS
    tm = w_ref.shape[1]
    for i_loc in range(w_ref.shape[0] // nk):
        w = None
        for h in range(PEER_HEADS):
            cnt_b = jnp.broadcast_to(cnt_ref[h, i_loc:i_loc + 1, :], (BF16_ROWS, tm)).astype(BF16)
            e1_b = jnp.broadcast_to(e1_ref[h, i_loc:i_loc + 1, :], (BF16_ROWS, tm)).astype(BF16)
            wh = jnp.where(rank2_ref[h] < cnt_b[None], e2_ref[h], jnp.zeros((), BF16)) * e1_b[None]
            w = wh if w is None else w + wh
        w_ref[i_loc * nk:(i_loc + 1) * nk, :] = w.reshape(nk, tm)


def _peer_dense_kernel(n_sub, h2t_ref, u_ref, vt_ref, rank2_ref, e2_ref, cnt_ref, e1_ref,
                       x1_ref, gf_ref, gpost_ref, o_ref, acc_ref, w_ref, *g_refs):
    j = pl.program_id(1)

    @pl.when(j == 0)
    def _():
        acc_ref[...] = jnp.zeros_like(acc_ref)

    nk = PEER_N_KEYS
    ts = u_ref.shape[0] // n_sub
    ats = [jnp.dot(u_ref[s * ts:(s + 1) * ts, :], h2t_ref[...], preferred_element_type=F32)
           for s in range(n_sub)]
    _gate_weights(w_ref, rank2_ref, e2_ref, cnt_ref, e1_ref)
    for s in range(n_sub):
        at = ats[s]
        for il in range(ts // nk):
            rows = slice(s * ts + il * nk, s * ts + (il + 1) * nk)
            a = at[il * nk:(il + 1) * nk, :]
            gelu = 0.5 * a * (1.0 + lax.erf(a * (1.0 / math.sqrt(2.0))))
            g_refs[s][il * nk:(il + 1) * nk, :] = gelu.astype(BF16) * w_ref[rows, :]
        acc_ref[...] += jnp.dot(vt_ref[:, s * ts:(s + 1) * ts], g_refs[s][...], preferred_element_type=F32)

    @pl.when(j == pl.num_programs(1) - 1)
    def _():
        y = acc_ref[...].T
        o_ref[...] = x1_ref[...] + gf_ref[0] * _rms(y, gpost_ref[...])


def _transpose_cast_kernel(v_ref, o_ref):
    o_ref[...] = v_ref[...].T.astype(BF16)


def _transpose_cast(v, te=512):
    ne, d = v.shape
    return pl.pallas_call(
        _transpose_cast_kernel,
        grid=(ne // te,),
        in_specs=[pl.BlockSpec((te, d), lambda j: (j, 0))],
        out_specs=pl.BlockSpec((d, te), lambda j: (0, j)),
        out_shape=jax.ShapeDtypeStruct((d, ne), BF16),
        compiler_params=_cparams(("arbitrary",)),
        name="expert_v_transpose",
    )(v)


def _peer_dense(h2t, u16, vt16, rank2, cnt, e1, e2, x1, gf, g_post, seq_len, tm=512, ts=256, n_sub=4):
    d, t = h2t.shape
    ne = u16.shape[0]
    nh, nk, _ = cnt.shape
    te = ts * n_sub
    i_per_step = te // nk
    n_steps = ne // te
    aux_j = pl.BlockSpec((nh, nk // BF16_ROWS, BF16_ROWS, tm), lambda i, j: (0, 0, 0, i))
    aux_i = pl.BlockSpec((nh, i_per_step, tm), lambda i, j: (0, j, i))
    blocks_per_seq = seq_len // tm
    return pl.pallas_call(
        functools.partial(_peer_dense_kernel, n_sub),
        grid=(t // tm, n_steps),
        in_specs=[pl.BlockSpec((d, tm), lambda i, j: (0, i)),
                  pl.BlockSpec((te, d), lambda i, j: (j, 0)),
                  pl.BlockSpec((d, te), lambda i, j: (0, j)),
                  aux_j, aux_j, aux_i, aux_i,
                  pl.BlockSpec((tm, d), lambda i, j: (i, 0), pipeline_mode=pl.Buffered(1)),
                  pl.BlockSpec((1, 1, d), lambda i, j: (i // blocks_per_seq, 0, 0)),
                  pl.BlockSpec((1, d), lambda i, j: (0, 0))],
        out_specs=pl.BlockSpec((tm, d), lambda i, j: (i, 0)),
        out_shape=jax.ShapeDtypeStruct((t, d), F32),
        scratch_shapes=[pltpu.VMEM((d, tm), F32), pltpu.VMEM((te, tm), BF16)]
        + [pltpu.VMEM((ts, tm), BF16) for _ in range(n_sub)],
        compiler_params=_cparams(("arbitrary", "arbitrary")),
        name="peer_dense",
    )(h2t, u16, vt16, rank2, e2, cnt, e1, x1, gf, g_post)


def _dft_tables(n):
    k = np.arange(n, dtype=np.int64)
    ph = (np.outer(k, k) % n).astype(np.float64) * (2.0 * np.pi / n)
    return np.cos(ph), np.sin(ph)


def kernel(x, c, ctx, c_ctx, w_mod, b_mod, g_pre_mix, g_post_mix, g_pre_ffn, g_post_ffn, w_in, w_fmix, conv_w, conv_b, dt_bias_f, dt_bias_b, a_log_f, a_log_b, d_skip_f, d_skip_b, ssm_norm_w, w_out, w_query, sub_keys_1, sub_keys_2, expert_u, expert_v):
    bsz, seq, d = x.shape
    ctx_len = ctx.shape[1]
    layer = 0
    n_heads = dt_bias_f.shape[1]
    d_ssm = n_heads * SSM_HEAD_DIM
    d_f = d - d_ssm
    gc = d_f // N_FOURIER_GROUPS
    d_xbc = conv_w.shape[2]
    assert w_mod.shape[0] == 1, "single-layer kernel"

    cond = jnp.zeros((8, d), F32).at[:bsz].set(c).at[bsz].set(c_ctx)
    mod = _adaln(cond, w_mod[layer], b_mod[layer])
    mods = [mod[:, i * d:(i + 1) * d] for i in range(N_MOD)]
    sm_x, cm_x, gm_x, sf_x, cf_x, gf_x = [m[:bsz, None, :] for m in mods]
    sm_c = jnp.broadcast_to(mods[0][bsz][None, None, :], (bsz, 1, d))
    cm_c = jnp.broadcast_to(mods[1][bsz][None, None, :], (bsz, 1, d))

    cos_l, sin_l = _dft_tables(seq)
    cos_c, sin_c = _dft_tables(gc)
    scale = 1.0 / math.sqrt(seq * gc)
    dt_pad = LANE
    col_z = 2 * d_f
    col_xbc = col_z + d_ssm
    col_dt = col_xbc + d_xbc
    n_all = col_dt + dt_pad
    assert (d_ssm + d_xbc) % gc == 0, "z/xBC columns must fill whole weight tiles"
    w_all = _build_weights(w_in[layer].T, w_fmix[layer], jnp.asarray(cos_c * scale, F32),
                           jnp.asarray(sin_c * scale, F32), pl.cdiv(n_all, gc) * gc)
    dft = jnp.asarray(np.concatenate([cos_l, -sin_l], axis=1), F32).astype(BF16)

    pad32 = lambda f, b_: jnp.zeros((1, LANE), F32).at[0, :n_heads].set(f).at[0, n_heads:2 * n_heads].set(b_)
    bias_row = pad32(dt_bias_f[layer], dt_bias_b[layer])
    a_row = pad32(-jnp.exp(a_log_f[layer]), -jnp.exp(a_log_b[layer]))

    w_ctx = w_all[:, col_xbc:n_all]
    proj_c = _inproj(ctx, g_pre_mix[layer], sm_c, cm_c, w_ctx, 0, n_all - col_xbc, tm=ctx_len, tn=n_all - col_xbc)
    xbc_c = _conv_silu(proj_c, 0, conv_w[layer], conv_b[layer])
    zero_state = jnp.zeros((bsz, 2, n_heads // 2, SSM_STATE, 2 * SSM_HEAD_DIM), F32)
    (fin_c,) = _ssd(proj_c, d_xbc, xbc_c, bias_row, a_row, zero_state, emit_y=False)

    proj_x = _inproj(x, g_pre_mix[layer], sm_x, cm_x, w_all, 0, n_all, tm=min(256, seq), tn=n_all)
    o_f = _seqdft(dft, proj_x, d_f)
    xbc_x = _conv_silu(proj_x, col_xbc, conv_w[layer], conv_b[layer])
    y_f, y_b, _ = _ssd(proj_x, col_dt, xbc_x, bias_row, a_row, fin_c, emit_y=True)

    dsk = jnp.repeat(d_skip_f[layer] + d_skip_b[layer], SSM_HEAD_DIM)[None, :]
    x1, h2t, q = _outproj(o_f, y_f, y_b, xbc_x, proj_x, col_z, x, dsk, ssm_norm_w[layer][None, :],
                         w_out[layer].astype(BF16), g_post_mix[layer][None, :], gm_x,
                         g_pre_ffn[layer][None, :], cf_x, sf_x, w_query[layer].astype(BF16))

    t = bsz * seq
    rank2, cnt, e1, e2 = _peer_route(q.reshape(t, -1), sub_keys_1[layer], sub_keys_2[layer],
                                     tq=min(2048, t))
    tiles = lambda a: a.reshape(a.shape[0], a.shape[1] // BF16_ROWS, BF16_ROWS, t)
    u16 = expert_u[layer].astype(BF16)
    vt16 = _transpose_cast(expert_v[layer])
    out = _peer_dense(h2t, u16, vt16, tiles(rank2), cnt, e1, tiles(e2), x1.reshape(t, d), gf_x,
                      g_post_ffn[layer][None, :], seq)
    return out.reshape(bsz, seq, d)
```

```python
import functools
import math

import jax
import jax.numpy as jnp
import numpy as np
from jax import lax
from jax.experimental import pallas as pl
from jax.experimental.pallas import tpu as pltpu

F32 = jnp.float32
BF16 = jnp.bfloat16

EPS = 1e-6
N_MOD = 6
N_FOURIER_GROUPS = 4
SSM_HEAD_DIM = 64
SSM_GROUPS = 2
SSM_STATE = 128
SSM_CHUNK = 128
PEER_HEADS = 8
PEER_TOPK = 16
PEER_N_KEYS = 128
PEER_KEY_DIM = 128
PEER_HALF = PEER_KEY_DIM // 2

LANE = 128
VMEM_LIMIT = 56 * 1024 * 1024


def _cparams(sem):
    return pltpu.CompilerParams(dimension_semantics=sem, vmem_limit_bytes=VMEM_LIMIT)


def _bdot(a, b):
    return jnp.dot(a.astype(BF16), b.astype(BF16), preferred_element_type=F32)


def _dot_nt(a, b, precision=None):
    return lax.dot_general(a, b, (((1,), (1,)), ((), ())), precision=precision,
                           preferred_element_type=F32)


def _rms(u, g):
    return u * lax.rsqrt(jnp.mean(u * u, axis=-1, keepdims=True) + EPS) * g


def _silu(u):
    return u * (1.0 / (1.0 + jnp.exp(-u)))


def _adaln_kernel(c_ref, w_ref, b_ref, o_ref):
    a = _silu(c_ref[...])
    o_ref[...] = _bdot(a, w_ref[...]) + b_ref[...]


def _adaln(cond, w_mod, b_mod, tn=1024):
    rows, d = cond.shape
    n = w_mod.shape[1]
    return pl.pallas_call(
        _adaln_kernel,
        grid=(n // tn,),
        in_specs=[pl.BlockSpec((rows, d), lambda j: (0, 0)),
                  pl.BlockSpec((d, tn), lambda j: (0, j)),
                  pl.BlockSpec((1, tn), lambda j: (0, j))],
        out_specs=pl.BlockSpec((rows, tn), lambda j: (0, j)),
        out_shape=jax.ShapeDtypeStruct((rows, n), F32),
        compiler_params=_cparams(("arbitrary",)),
        name="adaln",
    )(cond, w_mod, b_mod.reshape(1, n))


def _weights_kernel(ng, n_copy, wt_ref, tail_ref, fm_ref, cc_ref, sc_ref, o_ref):
    hp = lax.Precision.HIGHEST
    g = pl.program_id(0)

    def fold(tbl_ref):
        m = jnp.dot(tbl_ref[...], fm_ref[0], precision=hp, preferred_element_type=F32)
        o_ref[...] = jnp.dot(wt_ref[...].T, m, precision=hp, preferred_element_type=F32).astype(BF16)

    pl.when(g < ng)(lambda: fold(cc_ref))
    pl.when((g >= ng) & (g < 2 * ng))(lambda: fold(sc_ref))

    @pl.when((g >= 2 * ng) & (g < 2 * ng + n_copy))
    def _():
        o_ref[...] = wt_ref[...].T.astype(BF16)

    @pl.when(g == 2 * ng + n_copy)
    def _():
        tail = tail_ref[...]
        fill = jnp.zeros((o_ref.shape[1] - tail.shape[0], tail.shape[1]), F32)
        o_ref[...] = jnp.concatenate([tail, fill], axis=0).T.astype(BF16)

    @pl.when(g > 2 * ng + n_copy)
    def _():
        o_ref[...] = jnp.zeros_like(o_ref)


def _build_weights(w_in_t, w_fmix, cos_c, sin_c, n_all):
    n_in, d = w_in_t.shape
    ng, gc, _ = w_fmix.shape
    n_copy = (n_in - ng * gc) // gc
    tail_rows = n_in - (ng + n_copy) * gc
    assert 0 < tail_rows < gc and tail_rows % SUBLANES == 0 and ((ng + n_copy) * gc) % tail_rows == 0
    n_tiles = n_all // gc
    return pl.pallas_call(
        functools.partial(_weights_kernel, ng, n_copy),
        grid=(n_tiles,),
        in_specs=[pl.BlockSpec((gc, d), lambda g: (jnp.where(g < 2 * ng, g % ng, jnp.minimum(g - ng, ng + n_copy - 1)), 0)),
                  pl.BlockSpec((tail_rows, d), lambda g: ((ng + n_copy) * gc // tail_rows, 0)),
                  pl.BlockSpec((1, gc, gc), lambda g: (jnp.where(g < 2 * ng, g % ng, 0), 0, 0)),
                  pl.BlockSpec((gc, gc), lambda g: (0, 0)),
                  pl.BlockSpec((gc, gc), lambda g: (0, 0))],
        out_specs=pl.BlockSpec((d, gc), lambda g: (0, g)),
        out_shape=jax.ShapeDtypeStruct((d, n_tiles * gc), BF16),
        compiler_params=_cparams(("arbitrary",)),
        name="inproj_weights",
    )(w_in_t, w_in_t, w_fmix, cos_c, sin_c)


def _inproj_kernel(x_ref, g_ref, sh_ref, sc_ref, w_ref, o_ref, h_ref=None):
    modulated = lambda rows: (_rms(x_ref[0, rows, :], g_ref[...]) * (1.0 + sc_ref[0]) + sh_ref[0]).astype(BF16)
    if h_ref is None:
        o_ref[0] = jnp.dot(modulated(slice(None)), w_ref[...], preferred_element_type=F32)
        return

    @pl.when(pl.program_id(2) == 0)
    def _():
        slab = min(LANE, h_ref.shape[0])

        def body(r, carry):
            rows = pl.ds(pl.multiple_of(r * slab, slab), slab)
            h_ref[rows, :] = modulated(rows)
            return carry

        lax.fori_loop(0, h_ref.shape[0] // slab, body, 0)

    o_ref[0] = jnp.dot(h_ref[...], w_ref[...], preferred_element_type=F32)


def _inproj(xin, g, shift, scale, w_all, col0, ncols, tm, tn):
    b, l, d = xin.shape
    jb = col0 // tn
    w_mode = dict(pipeline_mode=pl.Buffered(1)) if ncols == tn else {}
    return pl.pallas_call(
        _inproj_kernel,
        grid=(b, l // tm, ncols // tn),
        in_specs=[pl.BlockSpec((1, tm, d), lambda bi, i, j: (bi, i, 0)),
                  pl.BlockSpec((1, d), lambda bi, i, j: (0, 0)),
                  pl.BlockSpec((1, 1, d), lambda bi, i, j: (bi, 0, 0)),
                  pl.BlockSpec((1, 1, d), lambda bi, i, j: (bi, 0, 0)),
                  pl.BlockSpec((d, tn), lambda bi, i, j: (0, j + jb), **w_mode)],
        out_specs=pl.BlockSpec((1, tm, tn), lambda bi, i, j: (bi, i, j)),
        out_shape=jax.ShapeDtypeStruct((b, l, ncols), F32),
        scratch_shapes=[] if ncols == tn else [pltpu.VMEM((tm, d), BF16)],
        compiler_params=_cparams(("arbitrary", "arbitrary", "arbitrary")),
        name="inproj",
    )(xin, g.reshape(1, d), shift, scale, w_all)


def _seqdft_kernel(a_ref, x_ref, o_ref, acc_ref):
    k = pl.program_id(1)

    @pl.when(k == 0)
    def _():
        acc_ref[...] = jnp.zeros_like(acc_ref)

    acc_ref[...] += jnp.dot(a_ref[...], x_ref[0].astype(BF16), preferred_element_type=F32)

    @pl.when(k == pl.num_programs(1) - 1)
    def _():
        o_ref[0] = acc_ref[...].astype(BF16)


def _seqdft(dft, proj, n_f, tk=1024):
    b, l, _ = proj.shape
    tk = min(tk, l)
    kl = l // tk
    return pl.pallas_call(
        _seqdft_kernel,
        grid=(b, 2 * kl),
        in_specs=[pl.BlockSpec((l, tk), lambda bi, k: (0, k)),
                  pl.BlockSpec((1, tk, n_f), lambda bi, k: (bi, k % kl, k // kl))],
        out_specs=pl.BlockSpec((1, l, n_f), lambda bi, k: (bi, 0, 0)),
        out_shape=jax.ShapeDtypeStruct((b, l, n_f), BF16),
        scratch_shapes=[pltpu.VMEM((l, n_f), F32)],
        compiler_params=_cparams(("arbitrary", "arbitrary")),
        name="seq_dft",
    )(dft, proj)


def _conv_kernel(u_ref, w_ref, b_ref, o_ref):
    u = u_ref[0]
    l = u.shape[0]
    row = lax.broadcasted_iota(jnp.int32, u.shape, 0)
    prev = jnp.where(row == 0, 0.0, pltpu.roll(u, 1, 0))
    nxt = jnp.where(row == l - 1, 0.0, pltpu.roll(u, l - 1, 0))
    w = w_ref[...]
    o_ref[0] = _silu(prev * w[0:1] + u * w[1:2] + nxt * w[2:3] + b_ref[...])


def _conv_silu(proj, col0, conv_w, conv_b, tc=512):
    b, l, _ = proj.shape
    n = conv_w.shape[1]
    jb = col0 // tc
    return pl.pallas_call(
        _conv_kernel,
        grid=(b, n // tc),
        in_specs=[pl.BlockSpec((1, l, tc), lambda bi, j: (bi, 0, j + jb)),
                  pl.BlockSpec((3, tc), lambda bi, j: (0, j)),
                  pl.BlockSpec((1, tc), lambda bi, j: (0, j))],
        out_specs=pl.BlockSpec((1, l, tc), lambda bi, j: (bi, 0, j)),
        out_shape=jax.ShapeDtypeStruct((b, l, n), F32),
        compiler_params=_cparams(("arbitrary", "arbitrary")),
        name="conv_silu",
    )(proj, conv_w, conv_b.reshape(1, n))


def _ssd_direction(reverse, dt_raw, bias, a_neg, xs_ref, bm_ref, cm_ref, st_ref, y_ref, n_heads):
    hp = lax.Precision.HIGHEST
    t = SSM_CHUNK
    p2 = 2 * SSM_HEAD_DIM
    col0 = n_heads if reverse else 0
    r = lax.broadcasted_iota(jnp.int32, (t, t), 0)
    c = lax.broadcasted_iota(jnp.int32, (t, t), 1)
    mask = (r <= c) if reverse else (r >= c)
    tri = mask.astype(F32)
    tri_t = ((r >= c) if reverse else (r <= c)).astype(F32)
    first_y = lax.broadcasted_iota(jnp.int32, (t, p2), 1) < SSM_HEAD_DIM
    first_s = lax.broadcasted_iota(jnp.int32, (SSM_STATE, p2), 1) < SSM_HEAD_DIM

    z = dt_raw + bias
    dtv = jnp.maximum(z, 0.0) + jnp.log(1.0 + jnp.exp(-jnp.abs(z)))
    a = dtv * a_neg
    a_t = a.T
    dtv_t = dtv.T
    cs = jnp.dot(tri, a, precision=hp, preferred_element_type=F32)
    cs_t = jnp.dot(a_t, tri_t, precision=hp, preferred_element_type=F32)
    w_t = dtv_t * jnp.exp(jnp.sum(a_t, axis=1, keepdims=True) - cs_t)
    etot = jnp.exp(jnp.sum(a, axis=0, keepdims=True))

    ppg = n_heads // SSM_GROUPS // 2
    for g in range(SSM_GROUPS):
        bg = bm_ref[0, :, g * SSM_STATE:(g + 1) * SSM_STATE]
        cg16 = cm_ref[0, :, g * SSM_STATE:(g + 1) * SSM_STATE].astype(BF16)
        gmat = _dot_nt(cg16, bg.astype(BF16))
        bgt = bg.T
        if y_ref is not None:
            st_g16 = jnp.concatenate([st_ref[g * ppg + k] for k in range(ppg)], axis=1).astype(BF16)
            yoff_g = jnp.dot(cg16, st_g16, preferred_element_type=F32)
        for k in range(ppg):
            pi = g * ppg + k
            xs16 = xs_ref[0, :, pi * p2:(pi + 1) * p2].astype(BF16)
            ys, ss, ecols, etots = [], [], [], []
            for e in range(2):
                j = col0 + 2 * pi + e
                col_b = jnp.broadcast_to(cs[:, j:j + 1], (t, t))
                lmat = jnp.exp(jnp.where(mask, col_b - cs_t[j:j + 1, :], -jnp.inf))
                m16 = (gmat * lmat * dtv_t[j:j + 1, :]).astype(BF16)
                if y_ref is not None:
                    ys.append(jnp.dot(m16, xs16, preferred_element_type=F32))
                    ecols.append(jnp.exp(col_b))
                bw16 = (bgt * w_t[j:j + 1, :]).astype(BF16)
                ss.append(jnp.dot(bw16, xs16, preferred_element_type=F32))
                etots.append(etot[:, j:j + 1])
            if y_ref is not None:
                y_off = yoff_g[:, k * p2:(k + 1) * p2] * jnp.where(first_y, ecols[0], ecols[1])
                y_ref[0, :, pi * p2:(pi + 1) * p2] = jnp.where(first_y, ys[0], ys[1]) + y_off
            st_ref[pi] = (jnp.where(first_s, etots[0], etots[1]) * st_ref[pi]
                          + jnp.where(first_s, ss[0], ss[1]))


def _ssd_kernel(n_heads, emit_y, *refs):
    (dtf_ref, dtb_ref, bias_ref, a_ref, xsf_ref, xsb_ref, bmf_ref, bmb_ref, cmf_ref, cmb_ref,
     init_ref) = refs[:11]
    if emit_y:
        yf_ref, yb_ref, fin_ref, stf_ref, stb_ref = refs[11:]
    else:
        yf_ref = yb_ref = None
        fin_ref, stf_ref, stb_ref = refs[11:]
    ci = pl.program_id(1)

    @pl.when(ci == 0)
    def _():
        stf_ref[...] = init_ref[0, 0]
        stb_ref[...] = init_ref[0, 1]

    bias = bias_ref[...]
    a_neg = a_ref[...]
    _ssd_direction(False, dtf_ref[0], bias, a_neg, xsf_ref, bmf_ref, cmf_ref, stf_ref, yf_ref, n_heads)
    _ssd_direction(True, dtb_ref[0], bias, a_neg, xsb_ref, bmb_ref, cmb_ref, stb_ref, yb_ref, n_heads)

    @pl.when(ci == pl.num_programs(1) - 1)
    def _():
        fin_ref[0, 0] = stf_ref[...]
        fin_ref[0, 1] = stb_ref[...]


def _ssd(proj, dt_col0, xbc, bias_row, a_row, init, emit_y):
    b, l, _ = proj.shape
    d_ssm = xbc.shape[2] - 2 * SSM_GROUPS * SSM_STATE
    d_bc = SSM_GROUPS * SSM_STATE
    n_heads = d_ssm // SSM_HEAD_DIM
    t = SSM_CHUNK
    nc = l // t
    dtb = dt_col0 // LANE
    fwd = lambda bi, c: (bi, c, 0)
    bwd = lambda bi, c: (bi, nc - 1 - c, 0)
    off = lambda f, o: (lambda bi, c: f(bi, c)[:2] + (o,))
    st_shape = (n_heads // 2, SSM_STATE, 2 * SSM_HEAD_DIM)
    in_specs = [
        pl.BlockSpec((1, t, LANE), off(fwd, dtb)), pl.BlockSpec((1, t, LANE), off(bwd, dtb)),
        pl.BlockSpec((1, LANE), lambda bi, c: (0, 0)), pl.BlockSpec((1, LANE), lambda bi, c: (0, 0)),
        pl.BlockSpec((1, t, d_ssm), fwd), pl.BlockSpec((1, t, d_ssm), bwd),
        pl.BlockSpec((1, t, d_bc), off(fwd, d_ssm // d_bc)), pl.BlockSpec((1, t, d_bc), off(bwd, d_ssm // d_bc)),
        pl.BlockSpec((1, t, d_bc), off(fwd, d_ssm // d_bc + 1)), pl.BlockSpec((1, t, d_bc), off(bwd, d_ssm // d_bc + 1)),
        pl.BlockSpec((1, 2) + st_shape, lambda bi, c: (bi, 0, 0, 0, 0)),
    ]
    fin_spec = pl.BlockSpec((1, 2) + st_shape, lambda bi, c: (bi, 0, 0, 0, 0))
    fin_shape = jax.ShapeDtypeStruct((b, 2) + st_shape, F32)
    if emit_y:
        y_shape = jax.ShapeDtypeStruct((b, l, d_ssm), F32)
        out_specs = [pl.BlockSpec((1, t, d_ssm), fwd), pl.BlockSpec((1, t, d_ssm), bwd), fin_spec]
        out_shape = [y_shape, y_shape, fin_shape]
    else:
        out_specs = [fin_spec]
        out_shape = [fin_shape]
    return pl.pallas_call(
        functools.partial(_ssd_kernel, n_heads, emit_y),
        grid=(b, nc),
        in_specs=in_specs,
        out_specs=out_specs,
        out_shape=out_shape,
        scratch_shapes=[pltpu.VMEM(st_shape, F32), pltpu.VMEM(st_shape, F32)],
        compiler_params=_cparams(("arbitrary", "arbitrary")),
        name="ssd_scan_y" if emit_y else "ssd_scan_state",
    )(proj, proj, bias_row, a_row, xbc, xbc, xbc, xbc, xbc, xbc, init)


def _outproj_kernel(of_ref, yf_ref, yb_ref, xs_ref, z_ref, x_ref, dsk_ref, nw_ref, wo_ref, gpost_ref,
                    gm_ref, gpre_ref, cf_ref, sf_ref, wq_ref, x1_ref, h2t_ref, q_ref):
    gw = yf_ref.shape[2] // SSM_GROUPS
    nw = nw_ref[...]
    n_slabs = 2
    rs = x_ref.shape[1] // n_slabs
    yxs = []
    for s in range(n_slabs):
        rows = slice(s * rs, (s + 1) * rs)
        y = yf_ref[0, rows, :] + yb_ref[0, rows, :] + dsk_ref[...] * xs_ref[0, rows, :]
        y = y * _silu(z_ref[0, rows, :])
        parts = [_rms(y[:, g * gw:(g + 1) * gw], nw[:, g * gw:(g + 1) * gw]).astype(BF16)
                 for g in range(SSM_GROUPS)]
        mixed = jnp.concatenate([of_ref[0, rows, :]] + parts, axis=1)
        yxs.append(jnp.dot(mixed, wo_ref[...], preferred_element_type=F32))
    for s in range(n_slabs):
        rows = slice(s * rs, (s + 1) * rs)
        yx = yxs[s]
        x1 = x_ref[0, rows, :] + gm_ref[0] * _rms(yx, gpost_ref[...])
        x1_ref[0, rows, :] = x1
        h2 = (_rms(x1, gpre_ref[...]) * (1.0 + cf_ref[0]) + sf_ref[0]).astype(BF16)
        h2t_ref[:, rows] = h2.T
        q_ref[0, rows, :] = jnp.dot(h2, wq_ref[...], preferred_element_type=F32)


def _outproj(of, yf, yb, xbc, proj, z_col0, x, dsk, nw, w_out, g_post, gm, g_pre, cf, sf, w_query, tm=256):
    b, l, d = x.shape
    d_f = of.shape[2]
    d_s = yf.shape[2]
    nq = w_query.shape[1]
    zb = z_col0 // d_s
    row = lambda bi, i: (bi, i, 0)
    vec = lambda n: pl.BlockSpec((1, n), lambda bi, i: (0, 0))
    bvec = lambda n: pl.BlockSpec((1, 1, n), lambda bi, i: (bi, 0, 0))
    return pl.pallas_call(
        _outproj_kernel,
        grid=(b, l // tm),
        in_specs=[pl.BlockSpec((1, tm, d_f), row), pl.BlockSpec((1, tm, d_s), row), pl.BlockSpec((1, tm, d_s), row),
                  pl.BlockSpec((1, tm, d_s), row), pl.BlockSpec((1, tm, d_s), lambda bi, i: (bi, i, zb)),
                  pl.BlockSpec((1, tm, d), row), vec(d_s), vec(d_s),
                  pl.BlockSpec((d_f + d_s, d), lambda bi, i: (0, 0)), vec(d), bvec(d), vec(d), bvec(d), bvec(d),
                  pl.BlockSpec((d, nq), lambda bi, i: (0, 0))],
        out_specs=[pl.BlockSpec((1, tm, d), row), pl.BlockSpec((d, tm), lambda bi, i: (0, bi * (l // tm) + i)),
                   pl.BlockSpec((1, tm, nq), row)],
        out_shape=[jax.ShapeDtypeStruct((b, l, d), F32), jax.ShapeDtypeStruct((d, b * l), BF16),
                   jax.ShapeDtypeStruct((b, l, nq), F32)],
        compiler_params=_cparams(("arbitrary", "arbitrary")),
        name="outproj_prenorm_query",
    )(of, yf, yb, xbc, proj, x, dsk, nw, w_out, g_post, gm, g_pre, cf, sf, w_query)


SUBLANES = 8


def _top16_ranked(s):
    n = s.shape[0]
    iota = lax.broadcasted_iota(jnp.int32, s.shape, 0).astype(F32)
    cur = s
    rank = jnp.full(s.shape, float(PEER_TOPK), F32)
    vals = []
    for k in range(PEER_TOPK):
        m = jnp.max(cur, axis=0, keepdims=True)
        idx = jnp.min(jnp.where(cur == m, iota, float(n)), axis=0, keepdims=True)
        hit = iota == idx
        rank = jnp.where(hit, float(k), rank)
        cur = jnp.where(hit, -jnp.inf, cur)
        vals.append(m)
    return vals, rank


def _candidate_pieces(v1, v2, sub):
    def stack8(vals):
        out = jnp.zeros(sub.shape, F32)
        for k, v in enumerate(vals):
            out = jnp.where(sub == k, v, out)
        return out

    v2_lo, v2_hi, v1_hi = stack8(v2[:8]), stack8(v2[8:]), stack8(v1[8:])
    return [v1[0] + v2_lo, v1[0] + v2_hi] + [v1[a] + v2_lo for a in range(1, 8)] + [v1_hi + v2[0]]


def _row_counts(sels, sub):
    n_rows = [jnp.sum(sels[0] + sels[1], axis=0, keepdims=True)]
    n_rows += [jnp.sum(sels[a + 1], axis=0, keepdims=True) for a in range(1, 8)]
    n_rows += [jnp.sum(jnp.where(sub == r, sels[9], 0.0), axis=0, keepdims=True) for r in range(8)]
    return n_rows


def _route_columns_exact(s1, s2):
    kk = PEER_TOPK
    v1, rank1 = _top16_ranked(s1)
    v2, rank2 = _top16_ranked(s2)
    sub = lax.broadcasted_iota(jnp.int32, (SUBLANES, s1.shape[1]), 0)
    subf = sub.astype(F32)
    cands = _candidate_pieces(v1, v2, sub)
    poses = [subf, subf + 8.0] + [a * kk + subf for a in range(1, 8)] + [(subf + 8.0) * kk]
    sels = [jnp.zeros(sub.shape, F32) for _ in cands]
    m0 = v1[0] + v2[0]
    zsum = jnp.zeros_like(m0)
    for _ in range(kk):
        m = jnp.max(functools.reduce(jnp.maximum, cands), axis=0, keepdims=True)
        firsts = [jnp.where(cd == m, ps, float(kk * kk)) for cd, ps in zip(cands, poses)]
        p = jnp.min(functools.reduce(jnp.minimum, firsts), axis=0, keepdims=True)
        hits = [ps == p for ps in poses]
        sels = [jnp.where(ht, 1.0, sl) for ht, sl in zip(hits, sels)]
        cands = [jnp.where(ht, -jnp.inf, cd) for ht, cd in zip(hits, cands)]
        zsum = zsum + jnp.exp(m - m0)
    n_rows = _row_counts(sels, sub)
    cnt = jnp.zeros(rank1.shape, F32)
    for a in range(kk):
        cnt = jnp.where(rank1 == float(a), n_rows[a], cnt)
    return rank2, cnt, zsum, v1[0], v2[0]


def _sorted_top16(blocks):
    v = list(blocks)
    n = len(v)

    def exchange(i, l, descending):
        hi, lo = jnp.maximum(v[i], v[l]), jnp.minimum(v[i], v[l])
        v[i], v[l] = (hi, lo) if descending else (lo, hi)

    k = 2
    while k <= n:
        j = k // 2
        while j >= 1:
            for i in range(n):
                if i ^ j > i:
                    exchange(i, i ^ j, (i & k) == 0)
            j //= 2
        k *= 2
    for shift in (4, 2, 1):
        w = [pltpu.roll(x, shift, 0) for x in v]
        v = [jnp.maximum(v[r], w[n - 1 - r]) for r in range(n)]
        j = n // 2
        while j >= 1:
            for i in range(n):
                if i ^ j > i:
                    exchange(i, i ^ j, True)
            j //= 2
    return v


def _route_columns_sorted(s1, s2):
    kk = PEER_TOPK
    nb = s1.shape[0] // SUBLANES
    tcols = s1.shape[1]
    blocks = lambda s: [s[SUBLANES * r:SUBLANES * (r + 1), :] for r in range(nb)]
    b1, b2 = blocks(s1), blocks(s2)
    v1, v2 = _sorted_top16(b1), _sorted_top16(b2)
    sub = lax.broadcasted_iota(jnp.int32, (SUBLANES, tcols), 0)
    cands = _candidate_pieces(v1, v2, sub)
    neg = jnp.full((SUBLANES, tcols), -jnp.inf, F32)
    top = _sorted_top16(cands + [neg] * (nb - len(cands)))
    tau = top[kk - 1]
    m0 = v1[0] + v2[0]
    picked = [cd >= tau for cd in cands]
    sels = [jnp.where(pk, 1.0, 0.0) for pk in picked]
    zparts = [jnp.where(pk, jnp.exp(cd - m0), 0.0) for pk, cd in zip(picked, cands)]
    zsum = jnp.sum(functools.reduce(jnp.add, zparts), axis=0, keepdims=True)
    n_rows = _row_counts(sels, sub)

    def rank_in(v, blk):
        gt = lambda pivot: pivot > blk
        c1 = gt(v[7])
        c2 = gt(jnp.where(c1, v[11], v[3]))
        c3 = gt(jnp.where(c1, jnp.where(c2, v[13], v[9]), jnp.where(c2, v[5], v[1])))
        even = [jnp.where(c3, v[4 * q + 2], v[4 * q]) for q in range(4)]
        c4 = gt(jnp.where(c1, jnp.where(c2, even[3], even[2]), jnp.where(c2, even[1], even[0])))
        c5 = gt(v[15])
        bit = lambda c, val: jnp.where(c, val, 0.0)
        return (bit(c1, 8.0) + bit(c2, 4.0)) + (bit(c3, 2.0) + bit(c4, 1.0)) + bit(c5, 1.0)

    rank2_blocks, cnt_blocks = [], []
    for blk1, blk2 in zip(b1, b2):
        rk = rank_in(v2, blk2)
        ct = jnp.zeros(blk1.shape, F32)
        for a in range(kk):
            ct = jnp.where(blk1 == v1[a], n_rows[a], ct)
        rank2_blocks.append(rk)
        cnt_blocks.append(ct)
    rank2 = jnp.concatenate(rank2_blocks, axis=0)
    cnt = jnp.concatenate(cnt_blocks, axis=0)

    count = lambda flags: jnp.sum(functools.reduce(jnp.add, flags), axis=0, keepdims=True)
    strict = lambda v: functools.reduce(jnp.logical_and, [v[k] > v[k + 1] for k in range(kk - 1)])[0:1]
    ok = (strict(v1) & strict(v2) & strict(top)
          & (count([jnp.where(blk >= v1[kk - 1], 1.0, 0.0) for blk in b1]) == float(kk))
          & (count([jnp.where(blk >= v2[kk - 1], 1.0, 0.0) for blk in b2]) == float(kk))
          & (functools.reduce(jnp.add, n_rows) == float(kk)))
    return rank2, cnt, zsum, v1[0][0:1], v2[0][0:1], ok


def _peer_route_kernel(q_ref, k1_ref, k2_ref, u_ref, v_ref, rank2_ref, cnt_ref, e1_ref, e2_ref, u16_ref, vt16_ref,
                       s1_ref, s2_ref):
    u16_ref[...] = u_ref[...].astype(BF16)
    vt16_ref[...] = v_ref[...].T.astype(BF16)

    def split(v):
        hi = v.astype(BF16)
        return hi, (v - hi.astype(F32)).astype(BF16)

    def scores(keys, qs):
        k_hi, k_lo = split(keys)
        q_hi, q_lo = split(qs)
        return _dot_nt(k_hi, q_hi) + (_dot_nt(k_hi, q_lo) + _dot_nt(k_lo, q_hi))

    qh = q_ref[...]
    s1_ref[...] = scores(k1_ref[0], qh[:, :PEER_HALF])
    s2_ref[...] = scores(k2_ref[0], qh[:, PEER_HALF:])

    width = min(2 * LANE, s1_ref.shape[1])

    def column(ci, carry):
        cols = pl.ds(pl.multiple_of(ci * width, width), width)
        s1 = s1_ref[:, cols]
        s2 = s2_ref[:, cols]

        def emit(rank2, cnt, zsum, m1, m2):
            rank2_ref[0, :, cols] = rank2.astype(BF16)
            cnt_ref[0, :, cols] = cnt
            e1_ref[0, :, cols] = jnp.exp(s1 - m1)
            e2_ref[0, :, cols] = (jnp.exp(s2 - m2) / zsum).astype(BF16)

        *fast, ok = _route_columns_sorted(s1, s2)
        emit(*fast)
        n_bad = jnp.sum(jnp.where(ok, 0.0, 1.0))

        @pl.when(n_bad > 0.0)
        def _():
            emit(*_route_columns_exact(s1, s2))

        return carry

    lax.fori_loop(0, s1_ref.shape[1] // width, column, 0)


def _peer_route(q, k1, k2, u, v, tq=1024):
    t, _ = q.shape
    nh, nk, hd = k1.shape
    ne, d = u.shape
    n_steps = (t // tq) * nh
    er = ne // n_steps
    assert er * n_steps == ne and er % LANE == 0, "expert rows must split evenly over the routing steps"
    out = lambda dt: jax.ShapeDtypeStruct((nh, nk, t), dt)
    spec = pl.BlockSpec((1, nk, tq), lambda i, h: (h, 0, i))
    rows = pl.BlockSpec((er, d), lambda i, h: (i * nh + h, 0))
    return pl.pallas_call(
        _peer_route_kernel,
        grid=(t // tq, nh),
        in_specs=[pl.BlockSpec((tq, PEER_KEY_DIM), lambda i, h: (i, h)),
                  pl.BlockSpec((1, nk, hd), lambda i, h: (h, 0, 0)),
                  pl.BlockSpec((1, nk, hd), lambda i, h: (h, 0, 0)),
                  rows, rows],
        out_specs=[spec, spec, spec, spec, rows, pl.BlockSpec((d, er), lambda i, h: (0, i * nh + h))],
        out_shape=[out(BF16), out(F32), out(F32), out(BF16),
                   jax.ShapeDtypeStruct((ne, d), BF16), jax.ShapeDtypeStruct((d, ne), BF16)],
        scratch_shapes=[pltpu.VMEM((nk, tq), F32), pltpu.VMEM((nk, tq), F32)],
        compiler_params=_cparams(("arbitrary", "arbitrary")),
        name="peer_route",
    )(q, k1, k2, u, v)


BF16_ROWS = 16


def _gate_weights(w_ref, rank2_ref, e2_ref, cnt_ref, e1_ref):
    nk = PEER_N_KEYS
    tm = w_ref.shape[1]
    for i_loc in range(w_ref.shape[0] // nk):
        w = None
        for h in range(PEER_HEADS):
            cnt_b = jnp.broadcast_to(cnt_ref[h, i_loc:i_loc + 1, :], (BF16_ROWS, tm)).astype(BF16)
            e1_b = jnp.broadcast_to(e1_ref[h, i_loc:i_loc + 1, :], (BF16_ROWS, tm)).astype(BF16)
            wh = jnp.where(rank2_ref[h] < cnt_b[None], e2_ref[h], jnp.zeros((), BF16)) * e1_b[None]
            w = wh if w is None else w + wh
        w_ref[i_loc * nk:(i_loc + 1) * nk, :] = w.reshape(nk, tm)


def _peer_dense_kernel(n_sub, h2t_ref, u_ref, vt_ref, rank2_ref, e2_ref, cnt_ref, e1_ref,
                       x1_ref, gf_ref, gpost_ref, o_ref, acc_ref, w_ref, *g_refs):
    j = pl.program_id(1)

    @pl.when(j == 0)
    def _():
        acc_ref[...] = jnp.zeros_like(acc_ref)

    nk = PEER_N_KEYS
    ts = u_ref.shape[0] // n_sub
    ats = [jnp.dot(u_ref[s * ts:(s + 1) * ts, :], h2t_ref[...], preferred_element_type=F32)
           for s in range(n_sub)]
    _gate_weights(w_ref, rank2_ref, e2_ref, cnt_ref, e1_ref)
    for s in range(n_sub):
        at = ats[s]
        for il in range(ts // nk):
            rows = slice(s * ts + il * nk, s * ts + (il + 1) * nk)
            a = at[il * nk:(il + 1) * nk, :]
            gelu = 0.5 * a * (1.0 + lax.erf(a * (1.0 / math.sqrt(2.0))))
            g_refs[s][il * nk:(il + 1) * nk, :] = gelu.astype(BF16) * w_ref[rows, :]
        acc_ref[...] += jnp.dot(vt_ref[:, s * ts:(s + 1) * ts], g_refs[s][...], preferred_element_type=F32)

    @pl.when(j == pl.num_programs(1) - 1)
    def _():
        y = acc_ref[...].T
        o_ref[...] = x1_ref[...] + gf_ref[0] * _rms(y, gpost_ref[...])


def _peer_dense(h2t, u16, vt16, rank2, cnt, e1, e2, x1, gf, g_post, seq_len, tm=512, ts=256, n_sub=4):
    d, t = h2t.shape
    ne = u16.shape[0]
    nh, nk, _ = cnt.shape
    te = ts * n_sub
    i_per_step = te // nk
    n_steps = ne // te
    aux_j = pl.BlockSpec((nh, nk // BF16_ROWS, BF16_ROWS, tm), lambda i, j: (0, 0, 0, i))
    aux_i = pl.BlockSpec((nh, i_per_step, tm), lambda i, j: (0, j, i))
    blocks_per_seq = seq_len // tm
    return pl.pallas_call(
        functools.partial(_peer_dense_kernel, n_sub),
        grid=(t // tm, n_steps),
        in_specs=[pl.BlockSpec((d, tm), lambda i, j: (0, i)),
                  pl.BlockSpec((te, d), lambda i, j: (j, 0)),
                  pl.BlockSpec((d, te), lambda i, j: (0, j)),
                  aux_j, aux_j, aux_i, aux_i,
                  pl.BlockSpec((tm, d), lambda i, j: (i, 0), pipeline_mode=pl.Buffered(1)),
                  pl.BlockSpec((1, 1, d), lambda i, j: (i // blocks_per_seq, 0, 0)),
                  pl.BlockSpec((1, d), lambda i, j: (0, 0))],
        out_specs=pl.BlockSpec((tm, d), lambda i, j: (i, 0)),
        out_shape=jax.ShapeDtypeStruct((t, d), F32),
        scratch_shapes=[pltpu.VMEM((d, tm), F32), pltpu.VMEM((te, tm), BF16)]
        + [pltpu.VMEM((ts, tm), BF16) for _ in range(n_sub)],
        compiler_params=_cparams(("arbitrary", "arbitrary")),
        name="peer_dense",
    )(h2t, u16, vt16, rank2, e2, cnt, e1, x1, gf, g_post)


def _dft_tables(n):
    k = np.arange(n, dtype=np.int64)
    ph = (np.outer(k, k) % n).astype(np.float64) * (2.0 * np.pi / n)
    return np.cos(ph), np.sin(ph)


def kernel(x, c, ctx, c_ctx, w_mod, b_mod, g_pre_mix, g_post_mix, g_pre_ffn, g_post_ffn, w_in, w_fmix, conv_w, conv_b, dt_bias_f, dt_bias_b, a_log_f, a_log_b, d_skip_f, d_skip_b, ssm_norm_w, w_out, w_query, sub_keys_1, sub_keys_2, expert_u, expert_v):
    bsz, seq, d = x.shape
    ctx_len = ctx.shape[1]
    layer = 0
    n_heads = dt_bias_f.shape[1]
    d_ssm = n_heads * SSM_HEAD_DIM
    d_f = d - d_ssm
    gc = d_f // N_FOURIER_GROUPS
    d_xbc = conv_w.shape[2]
    assert w_mod.shape[0] == 1, "single-layer kernel"

    cond = jnp.zeros((8, d), F32).at[:bsz].set(c).at[bsz].set(c_ctx)
    mod = _adaln(cond, w_mod[layer], b_mod[layer])
    mods = [mod[:, i * d:(i + 1) * d] for i in range(N_MOD)]
    sm_x, cm_x, gm_x, sf_x, cf_x, gf_x = [m[:bsz, None, :] for m in mods]
    sm_c = jnp.broadcast_to(mods[0][bsz][None, None, :], (bsz, 1, d))
    cm_c = jnp.broadcast_to(mods[1][bsz][None, None, :], (bsz, 1, d))

    cos_l, sin_l = _dft_tables(seq)
    cos_c, sin_c = _dft_tables(gc)
    scale = 1.0 / math.sqrt(seq * gc)
    dt_pad = LANE
    col_z = 2 * d_f
    col_xbc = col_z + d_ssm
    col_dt = col_xbc + d_xbc
    n_all = col_dt + dt_pad
    assert (d_ssm + d_xbc) % gc == 0, "z/xBC columns must fill whole weight tiles"
    w_all = _build_weights(w_in[layer].T, w_fmix[layer], jnp.asarray(cos_c * scale, F32),
                           jnp.asarray(sin_c * scale, F32), pl.cdiv(n_all, gc) * gc)
    dft = jnp.asarray(np.concatenate([cos_l, -sin_l], axis=1), F32).astype(BF16)

    pad32 = lambda f, b_: jnp.zeros((1, LANE), F32).at[0, :n_heads].set(f).at[0, n_heads:2 * n_heads].set(b_)
    bias_row = pad32(dt_bias_f[layer], dt_bias_b[layer])
    a_row = pad32(-jnp.exp(a_log_f[layer]), -jnp.exp(a_log_b[layer]))

    w_ctx = w_all[:, col_xbc:n_all]
    proj_c = _inproj(ctx, g_pre_mix[layer], sm_c, cm_c, w_ctx, 0, n_all - col_xbc, tm=ctx_len, tn=n_all - col_xbc)
    xbc_c = _conv_silu(proj_c, 0, conv_w[layer], conv_b[layer])
    zero_state = jnp.zeros((bsz, 2, n_heads // 2, SSM_STATE, 2 * SSM_HEAD_DIM), F32)
    (fin_c,) = _ssd(proj_c, d_xbc, xbc_c, bias_row, a_row, zero_state, emit_y=False)

    proj_x = _inproj(x, g_pre_mix[layer], sm_x, cm_x, w_all, 0, n_all, tm=min(256, seq), tn=n_all)
    o_f = _seqdft(dft, proj_x, d_f)
    xbc_x = _conv_silu(proj_x, col_xbc, conv_w[layer], conv_b[layer])
    y_f, y_b, _ = _ssd(proj_x, col_dt, xbc_x, bias_row, a_row, fin_c, emit_y=True)

    dsk = jnp.repeat(d_skip_f[layer] + d_skip_b[layer], SSM_HEAD_DIM)[None, :]
    x1, h2t, q = _outproj(o_f, y_f, y_b, xbc_x, proj_x, col_z, x, dsk, ssm_norm_w[layer][None, :],
                         w_out[layer].astype(BF16), g_post_mix[layer][None, :], gm_x,
                         g_pre_ffn[layer][None, :], cf_x, sf_x, w_query[layer].astype(BF16))

    t = bsz * seq
    rank2, cnt, e1, e2, u16, vt16 = _peer_route(q.reshape(t, -1), sub_keys_1[layer], sub_keys_2[layer],
                                                expert_u[layer], expert_v[layer], tq=min(2048, t))
    tiles = lambda a: a.reshape(a.shape[0], a.shape[1] // BF16_ROWS, BF16_ROWS, t)
    out = _peer_dense(h2t, u16, vt16, tiles(rank2), cnt, e1, tiles(e2), x1.reshape(t, d), gf_x,
                      g_post_ffn[layer][None, :], seq)
    return out.reshape(bsz, seq, d)
```

```python
import functools
import math

import jax
import jax.numpy as jnp
import numpy as np
from jax import lax
from jax.experimental import pallas as pl
from jax.experimental.pallas import tpu as pltpu

F32 = jnp.float32
BF16 = jnp.bfloat16

EPS = 1e-6
N_MOD = 6
N_FOURIER_GROUPS = 4
SSM_HEAD_DIM = 64
SSM_GROUPS = 2
SSM_STATE = 128
SSM_CHUNK = 128
PEER_HEADS = 8
PEER_TOPK = 16
PEER_N_KEYS = 128
PEER_KEY_DIM = 128
PEER_HALF = PEER_KEY_DIM // 2

LANE = 128
VMEM_LIMIT = 56 * 1024 * 1024


def _cparams(sem):
    return pltpu.CompilerParams(dimension_semantics=sem, vmem_limit_bytes=VMEM_LIMIT)


def _bdot(a, b):
    return jnp.dot(a.astype(BF16), b.astype(BF16), preferred_element_type=F32)


def _dot_nt(a, b, precision=None):
    return lax.dot_general(a, b, (((1,), (1,)), ((), ())), precision=precision,
                           preferred_element_type=F32)


def _rms(u, g):
    return u * lax.rsqrt(jnp.mean(u * u, axis=-1, keepdims=True) + EPS) * g


def _silu(u):
    return u * (1.0 / (1.0 + jnp.exp(-u)))


def _prep_kernel(ng, n_copy, n_mod, c_ref, wm_ref, bm_ref, wt_ref, tail_ref, fm_ref, cc_ref, sc_ref, mod_ref, o_ref):
    hp = lax.Precision.HIGHEST
    g = pl.program_id(0)

    @pl.when(g < n_mod)
    def _():
        mod_ref[...] = _bdot(_silu(c_ref[...]), wm_ref[...]) + bm_ref[...]

    def fold(tbl_ref):
        m = jnp.dot(tbl_ref[...], fm_ref[0], precision=hp, preferred_element_type=F32)
        o_ref[...] = jnp.dot(wt_ref[...].T, m, precision=hp, preferred_element_type=F32).astype(BF16)

    pl.when(g < ng)(lambda: fold(cc_ref))
    pl.when((g >= ng) & (g < 2 * ng))(lambda: fold(sc_ref))

    @pl.when((g >= 2 * ng) & (g < 2 * ng + n_copy))
    def _():
        o_ref[...] = wt_ref[...].T.astype(BF16)

    @pl.when(g == 2 * ng + n_copy)
    def _():
        tail = tail_ref[...]
        fill = jnp.zeros((o_ref.shape[1] - tail.shape[0], tail.shape[1]), F32)
        o_ref[...] = jnp.concatenate([tail, fill], axis=0).T.astype(BF16)

    @pl.when(g > 2 * ng + n_copy)
    def _():
        o_ref[...] = jnp.zeros_like(o_ref)


def _prepare(cond, w_mod, b_mod, w_in_t, w_fmix, cos_c, sin_c, n_all, tn=1024):
    rows, d = cond.shape
    n_mod_cols = w_mod.shape[1]
    n_mod = n_mod_cols // tn
    n_in = w_in_t.shape[0]
    ng, gc, _ = w_fmix.shape
    n_copy = (n_in - ng * gc) // gc
    tail_rows = n_in - (ng + n_copy) * gc
    assert 0 < tail_rows < gc and tail_rows % SUBLANES == 0 and ((ng + n_copy) * gc) % tail_rows == 0
    n_tiles = n_all // gc
    assert n_mod <= n_tiles, "the adaLN column tiles ride on the weight-tile steps"
    mod_tile = lambda g: (0, jnp.minimum(g, n_mod - 1))
    return pl.pallas_call(
        functools.partial(_prep_kernel, ng, n_copy, n_mod),
        grid=(n_tiles,),
        in_specs=[pl.BlockSpec((rows, d), lambda g: (0, 0)),
                  pl.BlockSpec((d, tn), mod_tile),
                  pl.BlockSpec((1, tn), mod_tile),
                  pl.BlockSpec((gc, d), lambda g: (jnp.where(g < 2 * ng, g % ng, jnp.minimum(g - ng, ng + n_copy - 1)), 0)),
                  pl.BlockSpec((tail_rows, d), lambda g: ((ng + n_copy) * gc // tail_rows, 0)),
                  pl.BlockSpec((1, gc, gc), lambda g: (jnp.where(g < 2 * ng, g % ng, 0), 0, 0)),
                  pl.BlockSpec((gc, gc), lambda g: (0, 0)),
                  pl.BlockSpec((gc, gc), lambda g: (0, 0))],
        out_specs=[pl.BlockSpec((rows, tn), mod_tile), pl.BlockSpec((d, gc), lambda g: (0, g))],
        out_shape=[jax.ShapeDtypeStruct((rows, n_mod_cols), F32), jax.ShapeDtypeStruct((d, n_tiles * gc), BF16)],
        compiler_params=_cparams(("arbitrary",)),
        name="adaln_and_inproj_weights",
    )(cond, w_mod, b_mod.reshape(1, n_mod_cols), w_in_t, w_in_t, w_fmix, cos_c, sin_c)


def _inproj_kernel(x_ref, g_ref, sh_ref, sc_ref, w_ref, o_ref, h_ref=None):
    modulated = lambda rows: (_rms(x_ref[0, rows, :], g_ref[...]) * (1.0 + sc_ref[0]) + sh_ref[0]).astype(BF16)
    if h_ref is None:
        o_ref[0] = jnp.dot(modulated(slice(None)), w_ref[...], preferred_element_type=F32)
        return

    @pl.when(pl.program_id(2) == 0)
    def _():
        slab = min(LANE, h_ref.shape[0])

        def body(r, carry):
            rows = pl.ds(pl.multiple_of(r * slab, slab), slab)
            h_ref[rows, :] = modulated(rows)
            return carry

        lax.fori_loop(0, h_ref.shape[0] // slab, body, 0)

    o_ref[0] = jnp.dot(h_ref[...], w_ref[...], preferred_element_type=F32)


def _inproj(xin, g, shift, scale, w_all, col0, ncols, tm, tn):
    b, l, d = xin.shape
    jb = col0 // tn
    w_mode = dict(pipeline_mode=pl.Buffered(1)) if ncols == tn else {}
    return pl.pallas_call(
        _inproj_kernel,
        grid=(b, l // tm, ncols // tn),
        in_specs=[pl.BlockSpec((1, tm, d), lambda bi, i, j: (bi, i, 0)),
                  pl.BlockSpec((1, d), lambda bi, i, j: (0, 0)),
                  pl.BlockSpec((1, 1, d), lambda bi, i, j: (bi, 0, 0)),
                  pl.BlockSpec((1, 1, d), lambda bi, i, j: (bi, 0, 0)),
                  pl.BlockSpec((d, tn), lambda bi, i, j: (0, j + jb), **w_mode)],
        out_specs=pl.BlockSpec((1, tm, tn), lambda bi, i, j: (bi, i, j)),
        out_shape=jax.ShapeDtypeStruct((b, l, ncols), F32),
        scratch_shapes=[] if ncols == tn else [pltpu.VMEM((tm, d), BF16)],
        compiler_params=_cparams(("arbitrary", "arbitrary", "arbitrary")),
        name="inproj",
    )(xin, g.reshape(1, d), shift, scale, w_all)


def _conv_kernel(u_ref, w_ref, b_ref, o_ref):
    u = u_ref[0]
    l = u.shape[0]
    row = lax.broadcasted_iota(jnp.int32, u.shape, 0)
    prev = jnp.where(row == 0, 0.0, pltpu.roll(u, 1, 0))
    nxt = jnp.where(row == l - 1, 0.0, pltpu.roll(u, l - 1, 0))
    w = w_ref[...]
    o_ref[0] = _silu(prev * w[0:1] + u * w[1:2] + nxt * w[2:3] + b_ref[...])


def _seqdft_kernel(a_ref, x_ref, u_ref, w_ref, b_ref, o_ref, c_ref, acc_ref):
    k = pl.program_id(1)

    @pl.when(k == 0)
    def _():
        acc_ref[...] = jnp.zeros_like(acc_ref)

    acc_ref[...] += jnp.dot(a_ref[...], x_ref[0].astype(BF16), preferred_element_type=F32)
    _conv_kernel(u_ref, w_ref, b_ref, c_ref)

    @pl.when(k == pl.num_programs(1) - 1)
    def _():
        o_ref[0] = acc_ref[...].astype(BF16)


def _seqdft_conv(dft, proj, n_f, conv_col0, conv_w, conv_b, tk=1024):
    b, l, _ = proj.shape
    tk = min(tk, l)
    kl = l // tk
    n_conv = conv_w.shape[1]
    tc = n_conv // (2 * kl)
    assert tc * 2 * kl == n_conv and tc % LANE == 0 and conv_col0 % tc == 0
    jb = conv_col0 // tc
    return pl.pallas_call(
        _seqdft_kernel,
        grid=(b, 2 * kl),
        in_specs=[pl.BlockSpec((l, tk), lambda bi, k: (0, k)),
                  pl.BlockSpec((1, tk, n_f), lambda bi, k: (bi, k % kl, k // kl)),
                  pl.BlockSpec((1, l, tc), lambda bi, k: (bi, 0, k + jb)),
                  pl.BlockSpec((3, tc), lambda bi, k: (0, k)),
                  pl.BlockSpec((1, tc), lambda bi, k: (0, k))],
        out_specs=[pl.BlockSpec((1, l, n_f), lambda bi, k: (bi, 0, 0)),
                   pl.BlockSpec((1, l, tc), lambda bi, k: (bi, 0, k))],
        out_shape=[jax.ShapeDtypeStruct((b, l, n_f), BF16), jax.ShapeDtypeStruct((b, l, n_conv), F32)],
        scratch_shapes=[pltpu.VMEM((l, n_f), F32)],
        compiler_params=_cparams(("arbitrary", "arbitrary")),
        name="seq_dft_conv",
    )(dft, proj, proj, conv_w, conv_b.reshape(1, n_conv))


def _conv_silu(proj, col0, conv_w, conv_b, tc=512):
    b, l, _ = proj.shape
    n = conv_w.shape[1]
    jb = col0 // tc
    return pl.pallas_call(
        _conv_kernel,
        grid=(b, n // tc),
        in_specs=[pl.BlockSpec((1, l, tc), lambda bi, j: (bi, 0, j + jb)),
                  pl.BlockSpec((3, tc), lambda bi, j: (0, j)),
                  pl.BlockSpec((1, tc), lambda bi, j: (0, j))],
        out_specs=pl.BlockSpec((1, l, tc), lambda bi, j: (bi, 0, j)),
        out_shape=jax.ShapeDtypeStruct((b, l, n), F32),
        compiler_params=_cparams(("arbitrary", "arbitrary")),
        name="conv_silu",
    )(proj, conv_w, conv_b.reshape(1, n))


def _ssd_direction(reverse, dt_raw, bias, a_neg, xs_ref, bm_ref, cm_ref, st_ref, y_ref, n_heads):
    hp = lax.Precision.HIGHEST
    t = SSM_CHUNK
    p2 = 2 * SSM_HEAD_DIM
    col0 = n_heads if reverse else 0
    r = lax.broadcasted_iota(jnp.int32, (t, t), 0)
    c = lax.broadcasted_iota(jnp.int32, (t, t), 1)
    mask = (r <= c) if reverse else (r >= c)
    tri = mask.astype(F32)
    tri_t = ((r >= c) if reverse else (r <= c)).astype(F32)
    first_y = lax.broadcasted_iota(jnp.int32, (t, p2), 1) < SSM_HEAD_DIM
    first_s = lax.broadcasted_iota(jnp.int32, (SSM_STATE, p2), 1) < SSM_HEAD_DIM

    z = dt_raw + bias
    dtv = jnp.maximum(z, 0.0) + jnp.log(1.0 + jnp.exp(-jnp.abs(z)))
    a = dtv * a_neg
    a_t = a.T
    dtv_t = dtv.T
    cs = jnp.dot(tri, a, precision=hp, preferred_element_type=F32)
    cs_t = jnp.dot(a_t, tri_t, precision=hp, preferred_element_type=F32)
    w_t = dtv_t * jnp.exp(jnp.sum(a_t, axis=1, keepdims=True) - cs_t)
    etot = jnp.exp(jnp.sum(a, axis=0, keepdims=True))

    ppg = n_heads // SSM_GROUPS // 2
    for g in range(SSM_GROUPS):
        bg = bm_ref[0, :, g * SSM_STATE:(g + 1) * SSM_STATE]
        cg16 = cm_ref[0, :, g * SSM_STATE:(g + 1) * SSM_STATE].astype(BF16)
        gmat = _dot_nt(cg16, bg.astype(BF16))
        bgt = bg.T
        if y_ref is not None:
            st_g16 = jnp.concatenate([st_ref[g * ppg + k] for k in range(ppg)], axis=1).astype(BF16)
            yoff_g = jnp.dot(cg16, st_g16, preferred_element_type=F32)
        for k in range(ppg):
            pi = g * ppg + k
            xs16 = xs_ref[0, :, pi * p2:(pi + 1) * p2].astype(BF16)
            ys, ss, ecols, etots = [], [], [], []
            for e in range(2):
                j = col0 + 2 * pi + e
                col_b = jnp.broadcast_to(cs[:, j:j + 1], (t, t))
                lmat = jnp.exp(jnp.where(mask, col_b - cs_t[j:j + 1, :], -jnp.inf))
                m16 = (gmat * lmat * dtv_t[j:j + 1, :]).astype(BF16)
                if y_ref is not None:
                    ys.append(jnp.dot(m16, xs16, preferred_element_type=F32))
                    ecols.append(jnp.exp(col_b))
                bw16 = (bgt * w_t[j:j + 1, :]).astype(BF16)
                ss.append(jnp.dot(bw16, xs16, preferred_element_type=F32))
                etots.append(etot[:, j:j + 1])
            if y_ref is not None:
                y_off = yoff_g[:, k * p2:(k + 1) * p2] * jnp.where(first_y, ecols[0], ecols[1])
                y_ref[0, :, pi * p2:(pi + 1) * p2] = jnp.where(first_y, ys[0], ys[1]) + y_off
            st_ref[pi] = (jnp.where(first_s, etots[0], etots[1]) * st_ref[pi]
                          + jnp.where(first_s, ss[0], ss[1]))


def _ssd_kernel(n_heads, emit_y, *refs):
    (dtf_ref, dtb_ref, bias_ref, a_ref, xsf_ref, xsb_ref, bmf_ref, bmb_ref, cmf_ref, cmb_ref,
     init_ref) = refs[:11]
    if emit_y:
        yf_ref, yb_ref, fin_ref, stf_ref, stb_ref = refs[11:]
    else:
        yf_ref = yb_ref = None
        fin_ref, stf_ref, stb_ref = refs[11:]
    ci = pl.program_id(1)

    @pl.when(ci == 0)
    def _():
        stf_ref[...] = init_ref[0, 0]
        stb_ref[...] = init_ref[0, 1]

    bias = bias_ref[...]
    a_neg = a_ref[...]
    _ssd_direction(False, dtf_ref[0], bias, a_neg, xsf_ref, bmf_ref, cmf_ref, stf_ref, yf_ref, n_heads)
    _ssd_direction(True, dtb_ref[0], bias, a_neg, xsb_ref, bmb_ref, cmb_ref, stb_ref, yb_ref, n_heads)

    @pl.when(ci == pl.num_programs(1) - 1)
    def _():
        fin_ref[0, 0] = stf_ref[...]
        fin_ref[0, 1] = stb_ref[...]


def _ssd(proj, dt_col0, xbc, bias_row, a_row, init, emit_y):
    b, l, _ = proj.shape
    d_ssm = xbc.shape[2] - 2 * SSM_GROUPS * SSM_STATE
    d_bc = SSM_GROUPS * SSM_STATE
    n_heads = d_ssm // SSM_HEAD_DIM
    t = SSM_CHUNK
    nc = l // t
    dtb = dt_col0 // LANE
    fwd = lambda bi, c: (bi, c, 0)
    bwd = lambda bi, c: (bi, nc - 1 - c, 0)
    off = lambda f, o: (lambda bi, c: f(bi, c)[:2] + (o,))
    st_shape = (n_heads // 2, SSM_STATE, 2 * SSM_HEAD_DIM)
    in_specs = [
        pl.BlockSpec((1, t, LANE), off(fwd, dtb)), pl.BlockSpec((1, t, LANE), off(bwd, dtb)),
        pl.BlockSpec((1, LANE), lambda bi, c: (0, 0)), pl.BlockSpec((1, LANE), lambda bi, c: (0, 0)),
        pl.BlockSpec((1, t, d_ssm), fwd), pl.BlockSpec((1, t, d_ssm), bwd),
        pl.BlockSpec((1, t, d_bc), off(fwd, d_ssm // d_bc)), pl.BlockSpec((1, t, d_bc), off(bwd, d_ssm // d_bc)),
        pl.BlockSpec((1, t, d_bc), off(fwd, d_ssm // d_bc + 1)), pl.BlockSpec((1, t, d_bc), off(bwd, d_ssm // d_bc + 1)),
        pl.BlockSpec((1, 2) + st_shape, lambda bi, c: (bi, 0, 0, 0, 0)),
    ]
    fin_spec = pl.BlockSpec((1, 2) + st_shape, lambda bi, c: (bi, 0, 0, 0, 0))
    fin_shape = jax.ShapeDtypeStruct((b, 2) + st_shape, F32)
    if emit_y:
        y_shape = jax.ShapeDtypeStruct((b, l, d_ssm), F32)
        out_specs = [pl.BlockSpec((1, t, d_ssm), fwd), pl.BlockSpec((1, t, d_ssm), bwd), fin_spec]
        out_shape = [y_shape, y_shape, fin_shape]
    else:
        out_specs = [fin_spec]
        out_shape = [fin_shape]
    return pl.pallas_call(
        functools.partial(_ssd_kernel, n_heads, emit_y),
        grid=(b, nc),
        in_specs=in_specs,
        out_specs=out_specs,
        out_shape=out_shape,
        scratch_shapes=[pltpu.VMEM(st_shape, F32), pltpu.VMEM(st_shape, F32)],
        compiler_params=_cparams(("arbitrary", "arbitrary")),
        name="ssd_scan_y" if emit_y else "ssd_scan_state",
    )(proj, proj, bias_row, a_row, xbc, xbc, xbc, xbc, xbc, xbc, init)


def _outproj_kernel(of_ref, yf_ref, yb_ref, xs_ref, z_ref, x_ref, dsk_ref, nw_ref, wo_ref, gpost_ref,
                    gm_ref, gpre_ref, cf_ref, sf_ref, wq_ref, x1_ref, h2t_ref, q_ref):
    gw = yf_ref.shape[2] // SSM_GROUPS
    nw = nw_ref[...]
    n_slabs = 2
    rs = x_ref.shape[1] // n_slabs
    yxs = []
    for s in range(n_slabs):
        rows = slice(s * rs, (s + 1) * rs)
        y = yf_ref[0, rows, :] + yb_ref[0, rows, :] + dsk_ref[...] * xs_ref[0, rows, :]
        y = y * _silu(z_ref[0, rows, :])
        parts = [_rms(y[:, g * gw:(g + 1) * gw], nw[:, g * gw:(g + 1) * gw]).astype(BF16)
                 for g in range(SSM_GROUPS)]
        mixed = jnp.concatenate([of_ref[0, rows, :]] + parts, axis=1)
        yxs.append(jnp.dot(mixed, wo_ref[...], preferred_element_type=F32))
    for s in range(n_slabs):
        rows = slice(s * rs, (s + 1) * rs)
        yx = yxs[s]
        x1 = x_ref[0, rows, :] + gm_ref[0] * _rms(yx, gpost_ref[...])
        x1_ref[0, rows, :] = x1
        h2 = (_rms(x1, gpre_ref[...]) * (1.0 + cf_ref[0]) + sf_ref[0]).astype(BF16)
        h2t_ref[:, rows] = h2.T
        q_ref[0, rows, :] = jnp.dot(h2, wq_ref[...], preferred_element_type=F32)


def _outproj(of, yf, yb, xbc, proj, z_col0, x, dsk, nw, w_out, g_post, gm, g_pre, cf, sf, w_query, tm=256):
    b, l, d = x.shape
    d_f = of.shape[2]
    d_s = yf.shape[2]
    nq = w_query.shape[1]
    zb = z_col0 // d_s
    row = lambda bi, i: (bi, i, 0)
    vec = lambda n: pl.BlockSpec((1, n), lambda bi, i: (0, 0))
    bvec = lambda n: pl.BlockSpec((1, 1, n), lambda bi, i: (bi, 0, 0))
    return pl.pallas_call(
        _outproj_kernel,
        grid=(b, l // tm),
        in_specs=[pl.BlockSpec((1, tm, d_f), row), pl.BlockSpec((1, tm, d_s), row), pl.BlockSpec((1, tm, d_s), row),
                  pl.BlockSpec((1, tm, d_s), row), pl.BlockSpec((1, tm, d_s), lambda bi, i: (bi, i, zb)),
                  pl.BlockSpec((1, tm, d), row), vec(d_s), vec(d_s),
                  pl.BlockSpec((d_f + d_s, d), lambda bi, i: (0, 0)), vec(d), bvec(d), vec(d), bvec(d), bvec(d),
                  pl.BlockSpec((d, nq), lambda bi, i: (0, 0))],
        out_specs=[pl.BlockSpec((1, tm, d), row), pl.BlockSpec((d, tm), lambda bi, i: (0, bi * (l // tm) + i)),
                   pl.BlockSpec((1, tm, nq), row)],
        out_shape=[jax.ShapeDtypeStruct((b, l, d), F32), jax.ShapeDtypeStruct((d, b * l), BF16),
                   jax.ShapeDtypeStruct((b, l, nq), F32)],
        compiler_params=_cparams(("arbitrary", "arbitrary")),
        name="outproj_prenorm_query",
    )(of, yf, yb, xbc, proj, x, dsk, nw, w_out, g_post, gm, g_pre, cf, sf, w_query)


SUBLANES = 8


def _top16_ranked(s):
    n = s.shape[0]
    iota = lax.broadcasted_iota(jnp.int32, s.shape, 0).astype(F32)
    cur = s
    rank = jnp.full(s.shape, float(PEER_TOPK), F32)
    vals = []
    for k in range(PEER_TOPK):
        m = jnp.max(cur, axis=0, keepdims=True)
        idx = jnp.min(jnp.where(cur == m, iota, float(n)), axis=0, keepdims=True)
        hit = iota == idx
        rank = jnp.where(hit, float(k), rank)
        cur = jnp.where(hit, -jnp.inf, cur)
        vals.append(m)
    return vals, rank


def _candidate_pieces(v1, v2, sub):
    def stack8(vals):
        out = jnp.zeros(sub.shape, F32)
        for k, v in enumerate(vals):
            out = jnp.where(sub == k, v, out)
        return out

    v2_lo, v2_hi, v1_hi = stack8(v2[:8]), stack8(v2[8:]), stack8(v1[8:])
    return [v1[0] + v2_lo, v1[0] + v2_hi] + [v1[a] + v2_lo for a in range(1, 8)] + [v1_hi + v2[0]]


def _row_counts(sels, sub):
    n_rows = [jnp.sum(sels[0] + sels[1], axis=0, keepdims=True)]
    n_rows += [jnp.sum(sels[a + 1], axis=0, keepdims=True) for a in range(1, 8)]
    n_rows += [jnp.sum(jnp.where(sub == r, sels[9], 0.0), axis=0, keepdims=True) for r in range(8)]
    return n_rows


def _route_columns_exact(s1, s2):
    kk = PEER_TOPK
    v1, rank1 = _top16_ranked(s1)
    v2, rank2 = _top16_ranked(s2)
    sub = lax.broadcasted_iota(jnp.int32, (SUBLANES, s1.shape[1]), 0)
    subf = sub.astype(F32)
    cands = _candidate_pieces(v1, v2, sub)
    poses = [subf, subf + 8.0] + [a * kk + subf for a in range(1, 8)] + [(subf + 8.0) * kk]
    sels = [jnp.zeros(sub.shape, F32) for _ in cands]
    m0 = v1[0] + v2[0]
    zsum = jnp.zeros_like(m0)
    for _ in range(kk):
        m = jnp.max(functools.reduce(jnp.maximum, cands), axis=0, keepdims=True)
        firsts = [jnp.where(cd == m, ps, float(kk * kk)) for cd, ps in zip(cands, poses)]
        p = jnp.min(functools.reduce(jnp.minimum, firsts), axis=0, keepdims=True)
        hits = [ps == p for ps in poses]
        sels = [jnp.where(ht, 1.0, sl) for ht, sl in zip(hits, sels)]
        cands = [jnp.where(ht, -jnp.inf, cd) for ht, cd in zip(hits, cands)]
        zsum = zsum + jnp.exp(m - m0)
    n_rows = _row_counts(sels, sub)
    cnt = jnp.zeros(rank1.shape, F32)
    for a in range(kk):
        cnt = jnp.where(rank1 == float(a), n_rows[a], cnt)
    return rank2, cnt, zsum, v1[0], v2[0]


def _sorted_top16(blocks):
    v = list(blocks)
    n = len(v)

    def exchange(i, l, descending):
        hi, lo = jnp.maximum(v[i], v[l]), jnp.minimum(v[i], v[l])
        v[i], v[l] = (hi, lo) if descending else (lo, hi)

    k = 2
    while k <= n:
        j = k // 2
        while j >= 1:
            for i in range(n):
                if i ^ j > i:
                    exchange(i, i ^ j, (i & k) == 0)
            j //= 2
        k *= 2
    for shift in (4, 2, 1):
        w = [pltpu.roll(x, shift, 0) for x in v]
        v = [jnp.maximum(v[r], w[n - 1 - r]) for r in range(n)]
        j = n // 2
        while j >= 1:
            for i in range(n):
                if i ^ j > i:
                    exchange(i, i ^ j, True)
            j //= 2
    return v


def _route_columns_sorted(s1, s2):
    kk = PEER_TOPK
    nb = s1.shape[0] // SUBLANES
    tcols = s1.shape[1]
    blocks = lambda s: [s[SUBLANES * r:SUBLANES * (r + 1), :] for r in range(nb)]
    b1, b2 = blocks(s1), blocks(s2)
    v1, v2 = _sorted_top16(b1), _sorted_top16(b2)
    sub = lax.broadcasted_iota(jnp.int32, (SUBLANES, tcols), 0)
    cands = _candidate_pieces(v1, v2, sub)
    neg = jnp.full((SUBLANES, tcols), -jnp.inf, F32)
    top = _sorted_top16(cands + [neg] * (nb - len(cands)))
    tau = top[kk - 1]
    m0 = v1[0] + v2[0]
    picked = [cd >= tau for cd in cands]
    sels = [jnp.where(pk, 1.0, 0.0) for pk in picked]
    zparts = [jnp.where(pk, jnp.exp(cd - m0), 0.0) for pk, cd in zip(picked, cands)]
    zsum = jnp.sum(functools.reduce(jnp.add, zparts), axis=0, keepdims=True)
    n_rows = _row_counts(sels, sub)

    def rank_in(v, blk):
        gt = lambda pivot: pivot > blk
        c1 = gt(v[7])
        c2 = gt(jnp.where(c1, v[11], v[3]))
        c3 = gt(jnp.where(c1, jnp.where(c2, v[13], v[9]), jnp.where(c2, v[5], v[1])))
        even = [jnp.where(c3, v[4 * q + 2], v[4 * q]) for q in range(4)]
        c4 = gt(jnp.where(c1, jnp.where(c2, even[3], even[2]), jnp.where(c2, even[1], even[0])))
        c5 = gt(v[15])
        bit = lambda c, val: jnp.where(c, val, 0.0)
        return (bit(c1, 8.0) + bit(c2, 4.0)) + (bit(c3, 2.0) + bit(c4, 1.0)) + bit(c5, 1.0)

    rank2_blocks, cnt_blocks = [], []
    for blk1, blk2 in zip(b1, b2):
        rk = rank_in(v2, blk2)
        ct = jnp.zeros(blk1.shape, F32)
        for a in range(kk):
            ct = jnp.where(blk1 == v1[a], n_rows[a], ct)
        rank2_blocks.append(rk)
        cnt_blocks.append(ct)
    rank2 = jnp.concatenate(rank2_blocks, axis=0)
    cnt = jnp.concatenate(cnt_blocks, axis=0)

    count = lambda flags: jnp.sum(functools.reduce(jnp.add, flags), axis=0, keepdims=True)
    strict = lambda v: functools.reduce(jnp.logical_and, [v[k] > v[k + 1] for k in range(kk - 1)])[0:1]
    ok = (strict(v1) & strict(v2) & strict(top)
          & (count([jnp.where(blk >= v1[kk - 1], 1.0, 0.0) for blk in b1]) == float(kk))
          & (count([jnp.where(blk >= v2[kk - 1], 1.0, 0.0) for blk in b2]) == float(kk))
          & (functools.reduce(jnp.add, n_rows) == float(kk)))
    return rank2, cnt, zsum, v1[0][0:1], v2[0][0:1], ok


def _peer_route_kernel(q_ref, k1_ref, k2_ref, u_ref, v_ref, rank2_ref, cnt_ref, e1_ref, e2_ref, u16_ref, vt16_ref,
                       s1_ref, s2_ref):
    u16_ref[...] = u_ref[...].astype(BF16)
    vt16_ref[...] = v_ref[...].T.astype(BF16)

    def split(v):
        hi = v.astype(BF16)
        return hi, (v - hi.astype(F32)).astype(BF16)

    def scores(keys, qs):
        k_hi, k_lo = split(keys)
        q_hi, q_lo = split(qs)
        return _dot_nt(k_hi, q_hi) + (_dot_nt(k_hi, q_lo) + _dot_nt(k_lo, q_hi))

    qh = q_ref[...]
    s1_ref[...] = scores(k1_ref[0], qh[:, :PEER_HALF])
    s2_ref[...] = scores(k2_ref[0], qh[:, PEER_HALF:])

    width = min(2 * LANE, s1_ref.shape[1])

    def column(ci, carry):
        cols = pl.ds(pl.multiple_of(ci * width, width), width)
        s1 = s1_ref[:, cols]
        s2 = s2_ref[:, cols]

        def emit(rank2, cnt, zsum, m1, m2):
            rank2_ref[0, :, cols] = rank2.astype(BF16)
            cnt_ref[0, :, cols] = cnt
            e1_ref[0, :, cols] = jnp.exp(s1 - m1)
            e2_ref[0, :, cols] = (jnp.exp(s2 - m2) / zsum).astype(BF16)

        *fast, ok = _route_columns_sorted(s1, s2)
        emit(*fast)
        n_bad = jnp.sum(jnp.where(ok, 0.0, 1.0))

        @pl.when(n_bad > 0.0)
        def _():
            emit(*_route_columns_exact(s1, s2))

        return carry

    lax.fori_loop(0, s1_ref.shape[1] // width, column, 0)


def _peer_route(q, k1, k2, u, v, tq=1024):
    t, _ = q.shape
    nh, nk, hd = k1.shape
    ne, d = u.shape
    n_steps = (t // tq) * nh
    er = ne // n_steps
    assert er * n_steps == ne and er % LANE == 0, "expert rows must split evenly over the routing steps"
    out = lambda dt: jax.ShapeDtypeStruct((nh, nk, t), dt)
    spec = pl.BlockSpec((1, nk, tq), lambda i, h: (h, 0, i))
    rows = pl.BlockSpec((er, d), lambda i, h: (i * nh + h, 0))
    return pl.pallas_call(
        _peer_route_kernel,
        grid=(t // tq, nh),
        in_specs=[pl.BlockSpec((tq, PEER_KEY_DIM), lambda i, h: (i, h)),
                  pl.BlockSpec((1, nk, hd), lambda i, h: (h, 0, 0)),
                  pl.BlockSpec((1, nk, hd), lambda i, h: (h, 0, 0)),
                  rows, rows],
        out_specs=[spec, spec, spec, spec, rows, pl.BlockSpec((d, er), lambda i, h: (0, i * nh + h))],
        out_shape=[out(BF16), out(F32), out(F32), out(BF16),
                   jax.ShapeDtypeStruct((ne, d), BF16), jax.ShapeDtypeStruct((d, ne), BF16)],
        scratch_shapes=[pltpu.VMEM((nk, tq), F32), pltpu.VMEM((nk, tq), F32)],
        compiler_params=_cparams(("arbitrary", "arbitrary")),
        name="peer_route",
    )(q, k1, k2, u, v)


BF16_ROWS = 16


def _gate_weights(w_ref, rank2_ref, e2_ref, cnt_ref, e1_ref):
    nk = PEER_N_KEYS
    tm = w_ref.shape[1]
    for i_loc in range(w_ref.shape[0] // nk):
        w = None
        for h in range(PEER_HEADS):
            cnt_b = jnp.broadcast_to(cnt_ref[h, i_loc:i_loc + 1, :], (BF16_ROWS, tm)).astype(BF16)
            e1_b = jnp.broadcast_to(e1_ref[h, i_loc:i_loc + 1, :], (BF16_ROWS, tm)).astype(BF16)
            wh = jnp.where(rank2_ref[h] < cnt_b[None], e2_ref[h], jnp.zeros((), BF16)) * e1_b[None]
            w = wh if w is None else w + wh
        w_ref[i_loc * nk:(i_loc + 1) * nk, :] = w.reshape(nk, tm)


def _peer_dense_kernel(n_sub, h2t_ref, u_ref, vt_ref, rank2_ref, e2_ref, cnt_ref, e1_ref,
                       x1_ref, gf_ref, gpost_ref, o_ref, acc_ref, w_ref, *g_refs):
    j = pl.program_id(1)

    @pl.when(j == 0)
    def _():
        acc_ref[...] = jnp.zeros_like(acc_ref)

    nk = PEER_N_KEYS
    ts = u_ref.shape[0] // n_sub
    ats = [jnp.dot(u_ref[s * ts:(s + 1) * ts, :], h2t_ref[...], preferred_element_type=F32)
           for s in range(n_sub)]
    _gate_weights(w_ref, rank2_ref, e2_ref, cnt_ref, e1_ref)
    for s in range(n_sub):
        at = ats[s]
        for il in range(ts // nk):
            rows = slice(s * ts + il * nk, s * ts + (il + 1) * nk)
            a = at[il * nk:(il + 1) * nk, :]
            gelu = 0.5 * a * (1.0 + lax.erf(a * (1.0 / math.sqrt(2.0))))
            g_refs[s][il * nk:(il + 1) * nk, :] = gelu.astype(BF16) * w_ref[rows, :]
        acc_ref[...] += jnp.dot(vt_ref[:, s * ts:(s + 1) * ts], g_refs[s][...], preferred_element_type=F32)

    @pl.when(j == pl.num_programs(1) - 1)
    def _():
        y = acc_ref[...].T
        o_ref[...] = x1_ref[...] + gf_ref[0] * _rms(y, gpost_ref[...])


def _peer_dense(h2t, u16, vt16, rank2, cnt, e1, e2, x1, gf, g_post, seq_len, tm=512, ts=256, n_sub=4):
    d, t = h2t.shape
    ne = u16.shape[0]
    nh, nk, _ = cnt.shape
    te = ts * n_sub
    i_per_step = te // nk
    n_steps = ne // te
    aux_j = pl.BlockSpec((nh, nk // BF16_ROWS, BF16_ROWS, tm), lambda i, j: (0, 0, 0, i))
    aux_i = pl.BlockSpec((nh, i_per_step, tm), lambda i, j: (0, j, i))
    blocks_per_seq = seq_len // tm
    return pl.pallas_call(
        functools.partial(_peer_dense_kernel, n_sub),
        grid=(t // tm, n_steps),
        in_specs=[pl.BlockSpec((d, tm), lambda i, j: (0, i)),
                  pl.BlockSpec((te, d), lambda i, j: (j, 0)),
                  pl.BlockSpec((d, te), lambda i, j: (0, j)),
                  aux_j, aux_j, aux_i, aux_i,
                  pl.BlockSpec((tm, d), lambda i, j: (i, 0), pipeline_mode=pl.Buffered(1)),
                  pl.BlockSpec((1, 1, d), lambda i, j: (i // blocks_per_seq, 0, 0)),
                  pl.BlockSpec((1, d), lambda i, j: (0, 0))],
        out_specs=pl.BlockSpec((tm, d), lambda i, j: (i, 0)),
        out_shape=jax.ShapeDtypeStruct((t, d), F32),
        scratch_shapes=[pltpu.VMEM((d, tm), F32), pltpu.VMEM((te, tm), BF16)]
        + [pltpu.VMEM((ts, tm), BF16) for _ in range(n_sub)],
        compiler_params=_cparams(("arbitrary", "arbitrary")),
        name="peer_dense",
    )(h2t, u16, vt16, rank2, e2, cnt, e1, x1, gf, g_post)


def _dft_tables(n):
    k = np.arange(n, dtype=np.int64)
    ph = (np.outer(k, k) % n).astype(np.float64) * (2.0 * np.pi / n)
    return np.cos(ph), np.sin(ph)


def kernel(x, c, ctx, c_ctx, w_mod, b_mod, g_pre_mix, g_post_mix, g_pre_ffn, g_post_ffn, w_in, w_fmix, conv_w, conv_b, dt_bias_f, dt_bias_b, a_log_f, a_log_b, d_skip_f, d_skip_b, ssm_norm_w, w_out, w_query, sub_keys_1, sub_keys_2, expert_u, expert_v):
    bsz, seq, d = x.shape
    ctx_len = ctx.shape[1]
    layer = 0
    n_heads = dt_bias_f.shape[1]
    d_ssm = n_heads * SSM_HEAD_DIM
    d_f = d - d_ssm
    gc = d_f // N_FOURIER_GROUPS
    d_xbc = conv_w.shape[2]
    assert w_mod.shape[0] == 1, "single-layer kernel"

    cond = jnp.zeros((8, d), F32).at[:bsz].set(c).at[bsz].set(c_ctx)
    cos_l, sin_l = _dft_tables(seq)
    cos_c, sin_c = _dft_tables(gc)
    scale = 1.0 / math.sqrt(seq * gc)
    dt_pad = LANE
    col_z = 2 * d_f
    col_xbc = col_z + d_ssm
    col_dt = col_xbc + d_xbc
    n_all = col_dt + dt_pad
    assert (d_ssm + d_xbc) % gc == 0, "z/xBC columns must fill whole weight tiles"
    mod, w_all = _prepare(cond, w_mod[layer], b_mod[layer], w_in[layer].T, w_fmix[layer],
                          jnp.asarray(cos_c * scale, F32), jnp.asarray(sin_c * scale, F32), pl.cdiv(n_all, gc) * gc)
    mods = [mod[:, i * d:(i + 1) * d] for i in range(N_MOD)]
    sm_x, cm_x, gm_x, sf_x, cf_x, gf_x = [m[:bsz, None, :] for m in mods]
    sm_c = jnp.broadcast_to(mods[0][bsz][None, None, :], (bsz, 1, d))
    cm_c = jnp.broadcast_to(mods[1][bsz][None, None, :], (bsz, 1, d))
    dft = jnp.asarray(np.concatenate([cos_l, -sin_l], axis=1), F32).astype(BF16)

    pad32 = lambda f, b_: jnp.zeros((1, LANE), F32).at[0, :n_heads].set(f).at[0, n_heads:2 * n_heads].set(b_)
    bias_row = pad32(dt_bias_f[layer], dt_bias_b[layer])
    a_row = pad32(-jnp.exp(a_log_f[layer]), -jnp.exp(a_log_b[layer]))

    w_ctx = w_all[:, col_xbc:n_all]
    proj_c = _inproj(ctx, g_pre_mix[layer], sm_c, cm_c, w_ctx, 0, n_all - col_xbc, tm=ctx_len, tn=n_all - col_xbc)
    xbc_c = _conv_silu(proj_c, 0, conv_w[layer], conv_b[layer])
    zero_state = jnp.zeros((bsz, 2, n_heads // 2, SSM_STATE, 2 * SSM_HEAD_DIM), F32)
    (fin_c,) = _ssd(proj_c, d_xbc, xbc_c, bias_row, a_row, zero_state, emit_y=False)

    proj_x = _inproj(x, g_pre_mix[layer], sm_x, cm_x, w_all, 0, n_all, tm=min(256, seq), tn=n_all)
    o_f, xbc_x = _seqdft_conv(dft, proj_x, d_f, col_xbc, conv_w[layer], conv_b[layer])
    y_f, y_b, _ = _ssd(proj_x, col_dt, xbc_x, bias_row, a_row, fin_c, emit_y=True)

    dsk = jnp.repeat(d_skip_f[layer] + d_skip_b[layer], SSM_HEAD_DIM)[None, :]
    x1, h2t, q = _outproj(o_f, y_f, y_b, xbc_x, proj_x, col_z, x, dsk, ssm_norm_w[layer][None, :],
                         w_out[layer].astype(BF16), g_post_mix[layer][None, :], gm_x,
                         g_pre_ffn[layer][None, :], cf_x, sf_x, w_query[layer].astype(BF16))

    t = bsz * seq
    rank2, cnt, e1, e2, u16, vt16 = _peer_route(q.reshape(t, -1), sub_keys_1[layer], sub_keys_2[layer],
                                                expert_u[layer], expert_v[layer], tq=min(2048, t))
    tiles = lambda a: a.reshape(a.shape[0], a.shape[1] // BF16_ROWS, BF16_ROWS, t)
    out = _peer_dense(h2t, u16, vt16, tiles(rank2), cnt, e1, tiles(e2), x1.reshape(t, d), gf_x,
                      g_post_ffn[layer][None, :], seq)
    return out.reshape(bsz, seq, d)
```

```python
import functools
import math

import jax
import jax.numpy as jnp
import numpy as np
from jax import lax
from jax.experimental import pallas as pl
from jax.experimental.pallas import tpu as pltpu

F32 = jnp.float32
BF16 = jnp.bfloat16

EPS = 1e-6
N_MOD = 6
N_FOURIER_GROUPS = 4
SSM_HEAD_DIM = 64
SSM_GROUPS = 2
SSM_STATE = 128
SSM_CHUNK = 128
PEER_HEADS = 8
PEER_TOPK = 16
PEER_N_KEYS = 128
PEER_KEY_DIM = 128
PEER_HALF = PEER_KEY_DIM // 2

LANE = 128
SUBLANES = 8
BF16_ROWS = 16
VMEM_LIMIT = 56 * 1024 * 1024

PREP_TN = 1024
INPROJ_TM = 256
DFT_TK = 1024
CONV_TC = 512
OUTPROJ_TM = 256
ROUTE_TQ = 2048
DENSE_TM = 512
DENSE_TS = 256
DENSE_NSUB = 4


def _cparams(sem):
    return pltpu.CompilerParams(dimension_semantics=sem, vmem_limit_bytes=VMEM_LIMIT)


def _bdot(a, b):
    return jnp.dot(a.astype(BF16), b.astype(BF16), preferred_element_type=F32)


def _dot_nt(a, b, precision=None):
    return lax.dot_general(a, b, (((1,), (1,)), ((), ())), precision=precision,
                           preferred_element_type=F32)


def _rms(u, g):
    return u * lax.rsqrt(jnp.mean(u * u, axis=-1, keepdims=True) + EPS) * g


def _silu(u):
    return u * (1.0 / (1.0 + jnp.exp(-u)))


def _prep_kernel(ng, n_copy, n_mod, c_ref, wm_ref, bm_ref, wt_ref, tail_ref, fm_ref, cc_ref, sc_ref, mod_ref, o_ref):
    hp = lax.Precision.HIGHEST
    g = pl.program_id(0)

    @pl.when(g < n_mod)
    def _():
        mod_ref[...] = _bdot(_silu(c_ref[...]), wm_ref[...]) + bm_ref[...]

    def fold(tbl_ref):
        m = jnp.dot(tbl_ref[...], fm_ref[0], precision=hp, preferred_element_type=F32)
        o_ref[...] = jnp.dot(wt_ref[...].T, m, precision=hp, preferred_element_type=F32).astype(BF16)

    pl.when(g < ng)(lambda: fold(cc_ref))
    pl.when((g >= ng) & (g < 2 * ng))(lambda: fold(sc_ref))

    @pl.when((g >= 2 * ng) & (g < 2 * ng + n_copy))
    def _():
        o_ref[...] = wt_ref[...].T.astype(BF16)

    @pl.when(g == 2 * ng + n_copy)
    def _():
        tail = tail_ref[...]
        fill = jnp.zeros((o_ref.shape[1] - tail.shape[0], tail.shape[1]), F32)
        o_ref[...] = jnp.concatenate([tail, fill], axis=0).T.astype(BF16)

    @pl.when(g > 2 * ng + n_copy)
    def _():
        o_ref[...] = jnp.zeros_like(o_ref)


def _prepare(cond, w_mod, b_mod, w_in_t, w_fmix, cos_c, sin_c, n_all, tn=PREP_TN):
    rows, d = cond.shape
    n_mod_cols = w_mod.shape[1]
    n_mod = n_mod_cols // tn
    n_in = w_in_t.shape[0]
    ng, gc, _ = w_fmix.shape
    n_copy = (n_in - ng * gc) // gc
    tail_rows = n_in - (ng + n_copy) * gc
    assert 0 < tail_rows < gc and tail_rows % SUBLANES == 0 and ((ng + n_copy) * gc) % tail_rows == 0
    n_tiles = n_all // gc
    assert n_mod <= n_tiles, "the adaLN column tiles ride on the weight-tile steps"
    mod_tile = lambda g: (0, jnp.minimum(g, n_mod - 1))
    return pl.pallas_call(
        functools.partial(_prep_kernel, ng, n_copy, n_mod),
        grid=(n_tiles,),
        in_specs=[pl.BlockSpec((rows, d), lambda g: (0, 0)),
                  pl.BlockSpec((d, tn), mod_tile),
                  pl.BlockSpec((1, tn), mod_tile),
                  pl.BlockSpec((gc, d), lambda g: (jnp.where(g < 2 * ng, g % ng, jnp.minimum(g - ng, ng + n_copy - 1)), 0)),
                  pl.BlockSpec((tail_rows, d), lambda g: ((ng + n_copy) * gc // tail_rows, 0)),
                  pl.BlockSpec((1, gc, gc), lambda g: (jnp.where(g < 2 * ng, g % ng, 0), 0, 0)),
                  pl.BlockSpec((gc, gc), lambda g: (0, 0)),
                  pl.BlockSpec((gc, gc), lambda g: (0, 0))],
        out_specs=[pl.BlockSpec((rows, tn), mod_tile), pl.BlockSpec((d, gc), lambda g: (0, g))],
        out_shape=[jax.ShapeDtypeStruct((rows, n_mod_cols), F32), jax.ShapeDtypeStruct((d, n_tiles * gc), BF16)],
        compiler_params=_cparams(("arbitrary",)),
        name="adaln_and_inproj_weights",
    )(cond, w_mod, b_mod.reshape(1, n_mod_cols), w_in_t, w_in_t, w_fmix, cos_c, sin_c)


def _inproj_kernel(x_ref, g_ref, sh_ref, sc_ref, w_ref, o_ref, h_ref=None):
    modulated = lambda rows: (_rms(x_ref[0, rows, :], g_ref[...]) * (1.0 + sc_ref[0]) + sh_ref[0]).astype(BF16)
    if h_ref is None:
        o_ref[0] = jnp.dot(modulated(slice(None)), w_ref[...], preferred_element_type=F32)
        return

    @pl.when(pl.program_id(2) == 0)
    def _():
        slab = min(LANE, h_ref.shape[0])

        def body(r, carry):
            rows = pl.ds(pl.multiple_of(r * slab, slab), slab)
            h_ref[rows, :] = modulated(rows)
            return carry

        lax.fori_loop(0, h_ref.shape[0] // slab, body, 0)

    o_ref[0] = jnp.dot(h_ref[...], w_ref[...], preferred_element_type=F32)


def _inproj(xin, g, shift, scale, w_all, col0, ncols, tm, tn):
    b, l, d = xin.shape
    jb = col0 // tn
    w_mode = dict(pipeline_mode=pl.Buffered(1)) if ncols == tn else {}
    return pl.pallas_call(
        _inproj_kernel,
        grid=(b, l // tm, ncols // tn),
        in_specs=[pl.BlockSpec((1, tm, d), lambda bi, i, j: (bi, i, 0)),
                  pl.BlockSpec((1, d), lambda bi, i, j: (0, 0)),
                  pl.BlockSpec((1, 1, d), lambda bi, i, j: (bi, 0, 0)),
                  pl.BlockSpec((1, 1, d), lambda bi, i, j: (bi, 0, 0)),
                  pl.BlockSpec((d, tn), lambda bi, i, j: (0, j + jb), **w_mode)],
        out_specs=pl.BlockSpec((1, tm, tn), lambda bi, i, j: (bi, i, j)),
        out_shape=jax.ShapeDtypeStruct((b, l, ncols), F32),
        scratch_shapes=[] if ncols == tn else [pltpu.VMEM((tm, d), BF16)],
        compiler_params=_cparams(("arbitrary", "arbitrary", "arbitrary")),
        name="inproj",
    )(xin, g.reshape(1, d), shift, scale, w_all)


def _conv_kernel(u_ref, w_ref, b_ref, o_ref):
    u = u_ref[0]
    l = u.shape[0]
    row = lax.broadcasted_iota(jnp.int32, u.shape, 0)
    prev = jnp.where(row == 0, 0.0, pltpu.roll(u, 1, 0))
    nxt = jnp.where(row == l - 1, 0.0, pltpu.roll(u, l - 1, 0))
    w = w_ref[...]
    o_ref[0] = _silu(prev * w[0:1] + u * w[1:2] + nxt * w[2:3] + b_ref[...])


def _seqdft_kernel(a_ref, x_ref, wo_ref, wq_ref, o_ref, wo16_ref, wq16_ref, acc_ref):
    k = pl.program_id(1)

    @pl.when(k == 0)
    def _():
        acc_ref[...] = jnp.zeros_like(acc_ref)

    acc_ref[...] += jnp.dot(a_ref[...], x_ref[0].astype(BF16), preferred_element_type=F32)
    wo16_ref[...] = wo_ref[...].astype(BF16)
    wq16_ref[...] = wq_ref[...].astype(BF16)

    @pl.when(k == pl.num_programs(1) - 1)
    def _():
        o_ref[0] = acc_ref[...].astype(BF16)


def _seqdft(dft, proj, n_f, w_out, w_query, tk=DFT_TK):
    b, l, _ = proj.shape
    tk = min(tk, l)
    kl = l // tk
    n_steps = b * 2 * kl
    rows = w_out.shape[0] // n_steps
    assert rows * n_steps == w_out.shape[0] == w_query.shape[0] and rows % BF16_ROWS == 0
    share = lambda a: pl.BlockSpec((rows, a.shape[1]), lambda bi, k: (bi * 2 * kl + k, 0))
    return pl.pallas_call(
        _seqdft_kernel,
        grid=(b, 2 * kl),
        in_specs=[pl.BlockSpec((l, tk), lambda bi, k: (0, k)),
                  pl.BlockSpec((1, tk, n_f), lambda bi, k: (bi, k % kl, k // kl)),
                  share(w_out), share(w_query)],
        out_specs=[pl.BlockSpec((1, l, n_f), lambda bi, k: (bi, 0, 0)), share(w_out), share(w_query)],
        out_shape=[jax.ShapeDtypeStruct((b, l, n_f), BF16), jax.ShapeDtypeStruct(w_out.shape, BF16),
                   jax.ShapeDtypeStruct(w_query.shape, BF16)],
        scratch_shapes=[pltpu.VMEM((l, n_f), F32)],
        compiler_params=_cparams(("arbitrary", "arbitrary")),
        name="seq_dft",
    )(dft, proj, w_out, w_query)


def _conv_silu(proj, col0, conv_w, conv_b, tc=CONV_TC):
    b, l, _ = proj.shape
    n = conv_w.shape[1]
    jb = col0 // tc
    return pl.pallas_call(
        _conv_kernel,
        grid=(b, n // tc),
        in_specs=[pl.BlockSpec((1, l, tc), lambda bi, j: (bi, 0, j + jb)),
                  pl.BlockSpec((3, tc), lambda bi, j: (0, j)),
                  pl.BlockSpec((1, tc), lambda bi, j: (0, j))],
        out_specs=pl.BlockSpec((1, l, tc), lambda bi, j: (bi, 0, j)),
        out_shape=jax.ShapeDtypeStruct((b, l, n), F32),
        compiler_params=_cparams(("arbitrary", "arbitrary")),
        name="conv_silu",
    )(proj, conv_w, conv_b.reshape(1, n))


def _ssd_direction(reverse, dt_raw, bias, a_neg, xs_ref, bm_ref, cm_ref, st_ref, y_ref, n_heads):
    hp = lax.Precision.HIGHEST
    t = SSM_CHUNK
    p2 = 2 * SSM_HEAD_DIM
    col0 = n_heads if reverse else 0
    r = lax.broadcasted_iota(jnp.int32, (t, t), 0)
    c = lax.broadcasted_iota(jnp.int32, (t, t), 1)
    mask = (r <= c) if reverse else (r >= c)
    tri = mask.astype(F32)
    tri_t = ((r >= c) if reverse else (r <= c)).astype(F32)
    first_y = lax.broadcasted_iota(jnp.int32, (t, p2), 1) < SSM_HEAD_DIM
    first_s = lax.broadcasted_iota(jnp.int32, (SSM_STATE, p2), 1) < SSM_HEAD_DIM

    z = dt_raw + bias
    dtv = jnp.maximum(z, 0.0) + jnp.log(1.0 + jnp.exp(-jnp.abs(z)))
    a = dtv * a_neg
    a_t = a.T
    dtv_t = dtv.T
    cs = jnp.dot(tri, a, precision=hp, preferred_element_type=F32)
    cs_t = jnp.dot(a_t, tri_t, precision=hp, preferred_element_type=F32)
    w_t = dtv_t * jnp.exp(jnp.sum(a_t, axis=1, keepdims=True) - cs_t)
    etot = jnp.exp(jnp.sum(a, axis=0, keepdims=True))

    ppg = n_heads // SSM_GROUPS // 2
    for g in range(SSM_GROUPS):
        bg = bm_ref[0, :, g * SSM_STATE:(g + 1) * SSM_STATE]
        cg16 = cm_ref[0, :, g * SSM_STATE:(g + 1) * SSM_STATE].astype(BF16)
        gmat = _dot_nt(cg16, bg.astype(BF16))
        bgt = bg.T
        if y_ref is not None:
            st_g16 = jnp.concatenate([st_ref[g * ppg + k] for k in range(ppg)], axis=1).astype(BF16)
            yoff_g = jnp.dot(cg16, st_g16, preferred_element_type=F32)
        for k in range(ppg):
            pi = g * ppg + k
            xs16 = xs_ref[0, :, pi * p2:(pi + 1) * p2].astype(BF16)
            ys, ss, ecols, etots = [], [], [], []
            for e in range(2):
                j = col0 + 2 * pi + e
                col_b = jnp.broadcast_to(cs[:, j:j + 1], (t, t))
                lmat = jnp.exp(jnp.where(mask, col_b - cs_t[j:j + 1, :], -jnp.inf))
                m16 = (gmat * lmat * dtv_t[j:j + 1, :]).astype(BF16)
                if y_ref is not None:
                    ys.append(jnp.dot(m16, xs16, preferred_element_type=F32))
                    ecols.append(jnp.exp(col_b))
                bw16 = (bgt * w_t[j:j + 1, :]).astype(BF16)
                ss.append(jnp.dot(bw16, xs16, preferred_element_type=F32))
                etots.append(etot[:, j:j + 1])
            if y_ref is not None:
                y_off = yoff_g[:, k * p2:(k + 1) * p2] * jnp.where(first_y, ecols[0], ecols[1])
                y_ref[0, :, pi * p2:(pi + 1) * p2] = jnp.where(first_y, ys[0], ys[1]) + y_off
            st_ref[pi] = (jnp.where(first_s, etots[0], etots[1]) * st_ref[pi]
                          + jnp.where(first_s, ss[0], ss[1]))


def _ssd_kernel(n_heads, emit_y, *refs):
    (dtf_ref, dtb_ref, bias_ref, a_ref, xsf_ref, xsb_ref, bmf_ref, bmb_ref, cmf_ref, cmb_ref,
     init_ref) = refs[:11]
    if emit_y:
        yf_ref, yb_ref, fin_ref, stf_ref, stb_ref = refs[11:]
    else:
        yf_ref = yb_ref = None
        fin_ref, stf_ref, stb_ref = refs[11:]
    ci = pl.program_id(1)

    @pl.when(ci == 0)
    def _():
        stf_ref[...] = init_ref[0, 0]
        stb_ref[...] = init_ref[0, 1]

    bias = bias_ref[...]
    a_neg = a_ref[...]
    _ssd_direction(False, dtf_ref[0], bias, a_neg, xsf_ref, bmf_ref, cmf_ref, stf_ref, yf_ref, n_heads)
    _ssd_direction(True, dtb_ref[0], bias, a_neg, xsb_ref, bmb_ref, cmb_ref, stb_ref, yb_ref, n_heads)

    @pl.when(ci == pl.num_programs(1) - 1)
    def _():
        fin_ref[0, 0] = stf_ref[...]
        fin_ref[0, 1] = stb_ref[...]


def _ssd(proj, dt_col0, xbc, bias_row, a_row, init, emit_y):
    b, l, _ = proj.shape
    d_ssm = xbc.shape[2] - 2 * SSM_GROUPS * SSM_STATE
    d_bc = SSM_GROUPS * SSM_STATE
    n_heads = d_ssm // SSM_HEAD_DIM
    t = SSM_CHUNK
    nc = l // t
    dtb = dt_col0 // LANE
    fwd = lambda bi, c: (bi, c, 0)
    bwd = lambda bi, c: (bi, nc - 1 - c, 0)
    off = lambda f, o: (lambda bi, c: f(bi, c)[:2] + (o,))
    st_shape = (n_heads // 2, SSM_STATE, 2 * SSM_HEAD_DIM)
    in_specs = [
        pl.BlockSpec((1, t, LANE), off(fwd, dtb)), pl.BlockSpec((1, t, LANE), off(bwd, dtb)),
        pl.BlockSpec((1, LANE), lambda bi, c: (0, 0)), pl.BlockSpec((1, LANE), lambda bi, c: (0, 0)),
        pl.BlockSpec((1, t, d_ssm), fwd), pl.BlockSpec((1, t, d_ssm), bwd),
        pl.BlockSpec((1, t, d_bc), off(fwd, d_ssm // d_bc)), pl.BlockSpec((1, t, d_bc), off(bwd, d_ssm // d_bc)),
        pl.BlockSpec((1, t, d_bc), off(fwd, d_ssm // d_bc + 1)), pl.BlockSpec((1, t, d_bc), off(bwd, d_ssm // d_bc + 1)),
        pl.BlockSpec((1, 2) + st_shape, lambda bi, c: (bi, 0, 0, 0, 0)),
    ]
    fin_spec = pl.BlockSpec((1, 2) + st_shape, lambda bi, c: (bi, 0, 0, 0, 0))
    fin_shape = jax.ShapeDtypeStruct((b, 2) + st_shape, F32)
    if emit_y:
        y_shape = jax.ShapeDtypeStruct((b, l, d_ssm), F32)
        out_specs = [pl.BlockSpec((1, t, d_ssm), fwd), pl.BlockSpec((1, t, d_ssm), bwd), fin_spec]
        out_shape = [y_shape, y_shape, fin_shape]
    else:
        out_specs = [fin_spec]
        out_shape = [fin_shape]
    return pl.pallas_call(
        functools.partial(_ssd_kernel, n_heads, emit_y),
        grid=(b, nc),
        in_specs=in_specs,
        out_specs=out_specs,
        out_shape=out_shape,
        scratch_shapes=[pltpu.VMEM(st_shape, F32), pltpu.VMEM(st_shape, F32)],
        compiler_params=_cparams(("arbitrary", "arbitrary")),
        name="ssd_scan_y" if emit_y else "ssd_scan_state",
    )(proj, proj, bias_row, a_row, xbc, xbc, xbc, xbc, xbc, xbc, init)


def _outproj_kernel(of_ref, yf_ref, yb_ref, xs_ref, z_ref, x_ref, dsk_ref, nw_ref, wo_ref, gpost_ref,
                    gm_ref, gpre_ref, cf_ref, sf_ref, wq_ref, x1_ref, h2t_ref, q_ref):
    gw = yf_ref.shape[2] // SSM_GROUPS
    nw = nw_ref[...]
    n_slabs = 2
    rs = x_ref.shape[1] // n_slabs
    yxs = []
    for s in range(n_slabs):
        rows = slice(s * rs, (s + 1) * rs)
        y = yf_ref[0, rows, :] + yb_ref[0, rows, :] + dsk_ref[...] * xs_ref[0, rows, :]
        y = y * _silu(z_ref[0, rows, :])
        parts = [_rms(y[:, g * gw:(g + 1) * gw], nw[:, g * gw:(g + 1) * gw]).astype(BF16)
                 for g in range(SSM_GROUPS)]
        mixed = jnp.concatenate([of_ref[0, rows, :]] + parts, axis=1)
        yxs.append(jnp.dot(mixed, wo_ref[...], preferred_element_type=F32))
    for s in range(n_slabs):
        rows = slice(s * rs, (s + 1) * rs)
        yx = yxs[s]
        x1 = x_ref[0, rows, :] + gm_ref[0] * _rms(yx, gpost_ref[...])
        x1_ref[0, rows, :] = x1
        h2 = (_rms(x1, gpre_ref[...]) * (1.0 + cf_ref[0]) + sf_ref[0]).astype(BF16)
        h2t_ref[:, rows] = h2.T
        q_ref[0, rows, :] = jnp.dot(h2, wq_ref[...], preferred_element_type=F32)


def _outproj(of, yf, yb, xbc, proj, z_col0, x, dsk, nw, w_out, g_post, gm, g_pre, cf, sf, w_query, tm=OUTPROJ_TM):
    b, l, d = x.shape
    d_f = of.shape[2]
    d_s = yf.shape[2]
    nq = w_query.shape[1]
    zb = z_col0 // d_s
    row = lambda bi, i: (bi, i, 0)
    vec = lambda n: pl.BlockSpec((1, n), lambda bi, i: (0, 0))
    bvec = lambda n: pl.BlockSpec((1, 1, n), lambda bi, i: (bi, 0, 0))
    return pl.pallas_call(
        _outproj_kernel,
        grid=(b, l // tm),
        in_specs=[pl.BlockSpec((1, tm, d_f), row), pl.BlockSpec((1, tm, d_s), row), pl.BlockSpec((1, tm, d_s), row),
                  pl.BlockSpec((1, tm, d_s), row), pl.BlockSpec((1, tm, d_s), lambda bi, i: (bi, i, zb)),
                  pl.BlockSpec((1, tm, d), row), vec(d_s), vec(d_s),
                  pl.BlockSpec((d_f + d_s, d), lambda bi, i: (0, 0)), vec(d), bvec(d), vec(d), bvec(d), bvec(d),
                  pl.BlockSpec((d, nq), lambda bi, i: (0, 0))],
        out_specs=[pl.BlockSpec((1, tm, d), row), pl.BlockSpec((d, tm), lambda bi, i: (0, bi * (l // tm) + i)),
                   pl.BlockSpec((1, tm, nq), row)],
        out_shape=[jax.ShapeDtypeStruct((b, l, d), F32), jax.ShapeDtypeStruct((d, b * l), BF16),
                   jax.ShapeDtypeStruct((b, l, nq), F32)],
        compiler_params=_cparams(("arbitrary", "arbitrary")),
        name="outproj_prenorm_query",
    )(of, yf, yb, xbc, proj, x, dsk, nw, w_out, g_post, gm, g_pre, cf, sf, w_query)


def _top16_ranked(s):
    n = s.shape[0]
    iota = lax.broadcasted_iota(jnp.int32, s.shape, 0).astype(F32)
    cur = s
    rank = jnp.full(s.shape, float(PEER_TOPK), F32)
    vals = []
    for k in range(PEER_TOPK):
        m = jnp.max(cur, axis=0, keepdims=True)
        idx = jnp.min(jnp.where(cur == m, iota, float(n)), axis=0, keepdims=True)
        hit = iota == idx
        rank = jnp.where(hit, float(k), rank)
        cur = jnp.where(hit, -jnp.inf, cur)
        vals.append(m)
    return vals, rank


def _candidate_pieces(v1, v2, sub):
    def stack8(vals):
        out = jnp.zeros(sub.shape, F32)
        for k, v in enumerate(vals):
            out = jnp.where(sub == k, v, out)
        return out

    v2_lo, v2_hi, v1_hi = stack8(v2[:8]), stack8(v2[8:]), stack8(v1[8:])
    return [v1[0] + v2_lo, v1[0] + v2_hi] + [v1[a] + v2_lo for a in range(1, 8)] + [v1_hi + v2[0]]


def _row_counts(sels, sub):
    n_rows = [jnp.sum(sels[0] + sels[1], axis=0, keepdims=True)]
    n_rows += [jnp.sum(sels[a + 1], axis=0, keepdims=True) for a in range(1, 8)]
    n_rows += [jnp.sum(jnp.where(sub == r, sels[9], 0.0), axis=0, keepdims=True) for r in range(8)]
    return n_rows


def _route_columns_exact(s1, s2):
    kk = PEER_TOPK
    v1, rank1 = _top16_ranked(s1)
    v2, rank2 = _top16_ranked(s2)
    sub = lax.broadcasted_iota(jnp.int32, (SUBLANES, s1.shape[1]), 0)
    subf = sub.astype(F32)
    cands = _candidate_pieces(v1, v2, sub)
    poses = [subf, subf + 8.0] + [a * kk + subf for a in range(1, 8)] + [(subf + 8.0) * kk]
    sels = [jnp.zeros(sub.shape, F32) for _ in cands]
    m0 = v1[0] + v2[0]
    zsum = jnp.zeros_like(m0)
    for _ in range(kk):
        m = jnp.max(functools.reduce(jnp.maximum, cands), axis=0, keepdims=True)
        firsts = [jnp.where(cd == m, ps, float(kk * kk)) for cd, ps in zip(cands, poses)]
        p = jnp.min(functools.reduce(jnp.minimum, firsts), axis=0, keepdims=True)
        hits = [ps == p for ps in poses]
        sels = [jnp.where(ht, 1.0, sl) for ht, sl in zip(hits, sels)]
        cands = [jnp.where(ht, -jnp.inf, cd) for ht, cd in zip(hits, cands)]
        zsum = zsum + jnp.exp(m - m0)
    n_rows = _row_counts(sels, sub)
    cnt = jnp.zeros(rank1.shape, F32)
    for a in range(kk):
        cnt = jnp.where(rank1 == float(a), n_rows[a], cnt)
    return rank2, cnt, zsum, v1[0], v2[0]


def _sorted_top16(blocks):
    v = list(blocks)
    n = len(v)

    def exchange(i, l, descending):
        hi, lo = jnp.maximum(v[i], v[l]), jnp.minimum(v[i], v[l])
        v[i], v[l] = (hi, lo) if descending else (lo, hi)

    k = 2
    while k <= n:
        j = k // 2
        while j >= 1:
            for i in range(n):
                if i ^ j > i:
                    exchange(i, i ^ j, (i & k) == 0)
            j //= 2
        k *= 2
    for shift in (4, 2, 1):
        w = [pltpu.roll(x, shift, 0) for x in v]
        v = [jnp.maximum(v[r], w[n - 1 - r]) for r in range(n)]
        j = n // 2
        while j >= 1:
            for i in range(n):
                if i ^ j > i:
                    exchange(i, i ^ j, True)
            j //= 2
    return v


def _route_columns_sorted(s1, s2):
    kk = PEER_TOPK
    nb = s1.shape[0] // SUBLANES
    tcols = s1.shape[1]
    blocks = lambda s: [s[SUBLANES * r:SUBLANES * (r + 1), :] for r in range(nb)]
    b1, b2 = blocks(s1), blocks(s2)
    v1, v2 = _sorted_top16(b1), _sorted_top16(b2)
    sub = lax.broadcasted_iota(jnp.int32, (SUBLANES, tcols), 0)
    cands = _candidate_pieces(v1, v2, sub)
    neg = jnp.full((SUBLANES, tcols), -jnp.inf, F32)
    top = _sorted_top16(cands + [neg] * (nb - len(cands)))
    tau = top[kk - 1]
    m0 = v1[0] + v2[0]
    picked = [cd >= tau for cd in cands]
    sels = [jnp.where(pk, 1.0, 0.0) for pk in picked]
    zparts = [jnp.where(pk, jnp.exp(cd - m0), 0.0) for pk, cd in zip(picked, cands)]
    zsum = jnp.sum(functools.reduce(jnp.add, zparts), axis=0, keepdims=True)
    n_rows = _row_counts(sels, sub)

    def rank_in(v, blk):
        gt = lambda pivot: pivot > blk
        c1 = gt(v[7])
        c2 = gt(jnp.where(c1, v[11], v[3]))
        c3 = gt(jnp.where(c1, jnp.where(c2, v[13], v[9]), jnp.where(c2, v[5], v[1])))
        even = [jnp.where(c3, v[4 * q + 2], v[4 * q]) for q in range(4)]
        c4 = gt(jnp.where(c1, jnp.where(c2, even[3], even[2]), jnp.where(c2, even[1], even[0])))
        c5 = gt(v[15])
        bit = lambda c, val: jnp.where(c, val, 0.0)
        return (bit(c1, 8.0) + bit(c2, 4.0)) + (bit(c3, 2.0) + bit(c4, 1.0)) + bit(c5, 1.0)

    rank2_blocks, cnt_blocks = [], []
    for blk1, blk2 in zip(b1, b2):
        rk = rank_in(v2, blk2)
        ct = jnp.zeros(blk1.shape, F32)
        for a in range(kk):
            ct = jnp.where(blk1 == v1[a], n_rows[a], ct)
        rank2_blocks.append(rk)
        cnt_blocks.append(ct)
    rank2 = jnp.concatenate(rank2_blocks, axis=0)
    cnt = jnp.concatenate(cnt_blocks, axis=0)

    count = lambda flags: jnp.sum(functools.reduce(jnp.add, flags), axis=0, keepdims=True)
    strict = lambda v: functools.reduce(jnp.logical_and, [v[k] > v[k + 1] for k in range(kk - 1)])[0:1]
    ok = (strict(v1) & strict(v2) & strict(top)
          & (count([jnp.where(blk >= v1[kk - 1], 1.0, 0.0) for blk in b1]) == float(kk))
          & (count([jnp.where(blk >= v2[kk - 1], 1.0, 0.0) for blk in b2]) == float(kk))
          & (functools.reduce(jnp.add, n_rows) == float(kk)))
    return rank2, cnt, zsum, v1[0][0:1], v2[0][0:1], ok


def _peer_route_kernel(q_ref, k1_ref, k2_ref, u_ref, v_ref, rank2_ref, cnt_ref, e1_ref, e2_ref, u16_ref, vt16_ref,
                       s1_ref, s2_ref):
    u16_ref[...] = u_ref[...].astype(BF16)
    vt16_ref[...] = v_ref[...].T.astype(BF16)

    def split(v):
        hi = v.astype(BF16)
        return hi, (v - hi.astype(F32)).astype(BF16)

    def scores(keys, qs):
        k_hi, k_lo = split(keys)
        q_hi, q_lo = split(qs)
        return _dot_nt(k_hi, q_hi) + (_dot_nt(k_hi, q_lo) + _dot_nt(k_lo, q_hi))

    qh = q_ref[...]
    s1_ref[...] = scores(k1_ref[0], qh[:, :PEER_HALF])
    s2_ref[...] = scores(k2_ref[0], qh[:, PEER_HALF:])

    width = min(2 * LANE, s1_ref.shape[1])

    def column(ci, carry):
        cols = pl.ds(pl.multiple_of(ci * width, width), width)
        s1 = s1_ref[:, cols]
        s2 = s2_ref[:, cols]

        def emit(rank2, cnt, zsum, m1, m2):
            rank2_ref[0, :, cols] = rank2.astype(BF16)
            cnt_ref[0, :, cols] = cnt
            e1_ref[0, :, cols] = jnp.exp(s1 - m1)
            e2_ref[0, :, cols] = (jnp.exp(s2 - m2) / zsum).astype(BF16)

        *fast, ok = _route_columns_sorted(s1, s2)
        emit(*fast)
        n_bad = jnp.sum(jnp.where(ok, 0.0, 1.0))

        @pl.when(n_bad > 0.0)
        def _():
            emit(*_route_columns_exact(s1, s2))

        return carry

    lax.fori_loop(0, s1_ref.shape[1] // width, column, 0)


def _peer_route(q, k1, k2, u, v, tq=ROUTE_TQ):
    t, _ = q.shape
    nh, nk, hd = k1.shape
    ne, d = u.shape
    n_steps = (t // tq) * nh
    er = ne // n_steps
    assert er * n_steps == ne and er % LANE == 0, "expert rows must split evenly over the routing steps"
    out = lambda dt: jax.ShapeDtypeStruct((nh, nk, t), dt)
    spec = pl.BlockSpec((1, nk, tq), lambda i, h: (h, 0, i))
    rows = pl.BlockSpec((er, d), lambda i, h: (i * nh + h, 0))
    return pl.pallas_call(
        _peer_route_kernel,
        grid=(t // tq, nh),
        in_specs=[pl.BlockSpec((tq, PEER_KEY_DIM), lambda i, h: (i, h)),
                  pl.BlockSpec((1, nk, hd), lambda i, h: (h, 0, 0)),
                  pl.BlockSpec((1, nk, hd), lambda i, h: (h, 0, 0)),
                  rows, rows],
        out_specs=[spec, spec, spec, spec, rows, pl.BlockSpec((d, er), lambda i, h: (0, i * nh + h))],
        out_shape=[out(BF16), out(F32), out(F32), out(BF16),
                   jax.ShapeDtypeStruct((ne, d), BF16), jax.ShapeDtypeStruct((d, ne), BF16)],
        scratch_shapes=[pltpu.VMEM((nk, tq), F32), pltpu.VMEM((nk, tq), F32)],
        compiler_params=_cparams(("arbitrary", "arbitrary")),
        name="peer_route",
    )(q, k1, k2, u, v)


def _gate_weights(w_ref, rank2_ref, e2_ref, cnt_ref, e1_ref):
    nk = PEER_N_KEYS
    tm = w_ref.shape[1]
    for i_loc in range(w_ref.shape[0] // nk):
        w = None
        for h in range(PEER_HEADS):
            cnt_b = jnp.broadcast_to(cnt_ref[h, i_loc:i_loc + 1, :], (BF16_ROWS, tm)).astype(BF16)
            e1_b = jnp.broadcast_to(e1_ref[h, i_loc:i_loc + 1, :], (BF16_ROWS, tm)).astype(BF16)
            wh = jnp.where(rank2_ref[h] < cnt_b[None], e2_ref[h], jnp.zeros((), BF16)) * e1_b[None]
            w = wh if w is None else w + wh
        w_ref[i_loc * nk:(i_loc + 1) * nk, :] = w.reshape(nk, tm)


def _peer_dense_kernel(n_sub, h2t_ref, u_ref, vt_ref, rank2_ref, e2_ref, cnt_ref, e1_ref,
                       x1_ref, gf_ref, gpost_ref, o_ref, acc_ref, w_ref, *g_refs):
    j = pl.program_id(1)

    @pl.when(j == 0)
    def _():
        acc_ref[...] = jnp.zeros_like(acc_ref)

    nk = PEER_N_KEYS
    ts = u_ref.shape[0] // n_sub
    ats = [jnp.dot(u_ref[s * ts:(s + 1) * ts, :], h2t_ref[...], preferred_element_type=F32)
           for s in range(n_sub)]
    _gate_weights(w_ref, rank2_ref, e2_ref, cnt_ref, e1_ref)
    for s in range(n_sub):
        at = ats[s]
        for il in range(ts // nk):
            rows = slice(s * ts + il * nk, s * ts + (il + 1) * nk)
            a = at[il * nk:(il + 1) * nk, :]
            gelu = 0.5 * a * (1.0 + lax.erf(a * (1.0 / math.sqrt(2.0))))
            g_refs[s][il * nk:(il + 1) * nk, :] = gelu.astype(BF16) * w_ref[rows, :]
        acc_ref[...] += jnp.dot(vt_ref[:, s * ts:(s + 1) * ts], g_refs[s][...], preferred_element_type=F32)

    @pl.when(j == pl.num_programs(1) - 1)
    def _():
        y = acc_ref[...].T
        o_ref[...] = x1_ref[...] + gf_ref[0] * _rms(y, gpost_ref[...])


def _peer_dense(h2t, u16, vt16, rank2, cnt, e1, e2, x1, gf, g_post, seq_len, tm=DENSE_TM, ts=DENSE_TS, n_sub=DENSE_NSUB):
    d, t = h2t.shape
    ne = u16.shape[0]
    nh, nk, _ = cnt.shape
    te = ts * n_sub
    i_per_step = te // nk
    n_steps = ne // te
    aux_j = pl.BlockSpec((nh, nk // BF16_ROWS, BF16_ROWS, tm), lambda i, j: (0, 0, 0, i))
    aux_i = pl.BlockSpec((nh, i_per_step, tm), lambda i, j: (0, j, i))
    blocks_per_seq = seq_len // tm
    return pl.pallas_call(
        functools.partial(_peer_dense_kernel, n_sub),
        grid=(t // tm, n_steps),
        in_specs=[pl.BlockSpec((d, tm), lambda i, j: (0, i)),
                  pl.BlockSpec((te, d), lambda i, j: (j, 0)),
                  pl.BlockSpec((d, te), lambda i, j: (0, j)),
                  aux_j, aux_j, aux_i, aux_i,
                  pl.BlockSpec((tm, d), lambda i, j: (i, 0), pipeline_mode=pl.Buffered(1)),
                  pl.BlockSpec((1, 1, d), lambda i, j: (i // blocks_per_seq, 0, 0)),
                  pl.BlockSpec((1, d), lambda i, j: (0, 0))],
        out_specs=pl.BlockSpec((tm, d), lambda i, j: (i, 0)),
        out_shape=jax.ShapeDtypeStruct((t, d), F32),
        scratch_shapes=[pltpu.VMEM((d, tm), F32), pltpu.VMEM((te, tm), BF16)]
        + [pltpu.VMEM((ts, tm), BF16) for _ in range(n_sub)],
        compiler_params=_cparams(("arbitrary", "arbitrary")),
        name="peer_dense",
    )(h2t, u16, vt16, rank2, e2, cnt, e1, x1, gf, g_post)


def _dft_tables(n):
    k = np.arange(n, dtype=np.int64)
    ph = (np.outer(k, k) % n).astype(np.float64) * (2.0 * np.pi / n)
    return np.cos(ph), np.sin(ph)


def kernel(x, c, ctx, c_ctx, w_mod, b_mod, g_pre_mix, g_post_mix, g_pre_ffn, g_post_ffn, w_in, w_fmix, conv_w, conv_b, dt_bias_f, dt_bias_b, a_log_f, a_log_b, d_skip_f, d_skip_b, ssm_norm_w, w_out, w_query, sub_keys_1, sub_keys_2, expert_u, expert_v):
    bsz, seq, d = x.shape
    ctx_len = ctx.shape[1]
    layer = 0
    n_heads = dt_bias_f.shape[1]
    d_ssm = n_heads * SSM_HEAD_DIM
    d_f = d - d_ssm
    gc = d_f // N_FOURIER_GROUPS
    d_xbc = conv_w.shape[2]
    assert w_mod.shape[0] == 1, "single-layer kernel"

    cond = jnp.zeros((8, d), F32).at[:bsz].set(c).at[bsz].set(c_ctx)
    cos_l, sin_l = _dft_tables(seq)
    cos_c, sin_c = _dft_tables(gc)
    scale = 1.0 / math.sqrt(seq * gc)
    dt_pad = LANE
    col_z = 2 * d_f
    col_xbc = col_z + d_ssm
    col_dt = col_xbc + d_xbc
    n_all = col_dt + dt_pad
    assert (d_ssm + d_xbc) % gc == 0, "z/xBC columns must fill whole weight tiles"
    mod, w_all = _prepare(cond, w_mod[layer], b_mod[layer], w_in[layer].T, w_fmix[layer],
                          jnp.asarray(cos_c * scale, F32), jnp.asarray(sin_c * scale, F32), pl.cdiv(n_all, gc) * gc)
    mods = [mod[:, i * d:(i + 1) * d] for i in range(N_MOD)]
    sm_x, cm_x, gm_x, sf_x, cf_x, gf_x = [m[:bsz, None, :] for m in mods]
    sm_c = jnp.broadcast_to(mods[0][bsz][None, None, :], (bsz, 1, d))
    cm_c = jnp.broadcast_to(mods[1][bsz][None, None, :], (bsz, 1, d))
    dft = jnp.asarray(np.concatenate([cos_l, -sin_l], axis=1), F32).astype(BF16)

    pad32 = lambda f, b_: jnp.zeros((1, LANE), F32).at[0, :n_heads].set(f).at[0, n_heads:2 * n_heads].set(b_)
    bias_row = pad32(dt_bias_f[layer], dt_bias_b[layer])
    a_row = pad32(-jnp.exp(a_log_f[layer]), -jnp.exp(a_log_b[layer]))

    w_ctx = w_all[:, col_xbc:n_all]
    proj_c = _inproj(ctx, g_pre_mix[layer], sm_c, cm_c, w_ctx, 0, n_all - col_xbc, tm=ctx_len, tn=n_all - col_xbc)
    xbc_c = _conv_silu(proj_c, 0, conv_w[layer], conv_b[layer])
    zero_state = jnp.zeros((bsz, 2, n_heads // 2, SSM_STATE, 2 * SSM_HEAD_DIM), F32)
    (fin_c,) = _ssd(proj_c, d_xbc, xbc_c, bias_row, a_row, zero_state, emit_y=False)

    proj_x = _inproj(x, g_pre_mix[layer], sm_x, cm_x, w_all, 0, n_all, tm=min(INPROJ_TM, seq), tn=n_all)
    o_f, w_out16, w_query16 = _seqdft(dft, proj_x, d_f, w_out[layer], w_query[layer])
    xbc_x = _conv_silu(proj_x, col_xbc, conv_w[layer], conv_b[layer])
    y_f, y_b, _ = _ssd(proj_x, col_dt, xbc_x, bias_row, a_row, fin_c, emit_y=True)

    dsk = jnp.repeat(d_skip_f[layer] + d_skip_b[layer], SSM_HEAD_DIM)[None, :]
    x1, h2t, q = _outproj(o_f, y_f, y_b, xbc_x, proj_x, col_z, x, dsk, ssm_norm_w[layer][None, :],
                         w_out16, g_post_mix[layer][None, :], gm_x,
                         g_pre_ffn[layer][None, :], cf_x, sf_x, w_query16)

    t = bsz * seq
    rank2, cnt, e1, e2, u16, vt16 = _peer_route(q.reshape(t, -1), sub_keys_1[layer], sub_keys_2[layer],
                                                expert_u[layer], expert_v[layer], tq=min(ROUTE_TQ, t))
    tiles = lambda a: a.reshape(a.shape[0], a.shape[1] // BF16_ROWS, BF16_ROWS, t)
    out = _peer_dense(h2t, u16, vt16, tiles(rank2), cnt, e1, tiles(e2), x1.reshape(t, d), gf_x,
                      g_post_ffn[layer][None, :], seq)
    return out.reshape(bsz, seq, d)
```
